```python
import math
import jax
import jax.numpy as jnp
from jax import lax
import numpy as np

D_MODEL = 2048
BATCH = 4
SEQ = 2048
DEPTH = 4
DEC_BATCH = 32
DEC_SEQ = 1
PAST_LEN = 16384
PAGE_SIZE = 128

N_EVEN = (DEPTH + 1) // 2
N_ODD = DEPTH // 2
D_CONV = 1024
CONV_WIDTH = 3
HGRN_HEADS = 8
HGRN_DK = 128
HGRN_DV = 128
HGRN_CHUNK = 64
D_HK = HGRN_HEADS * HGRN_DK
D_HV = HGRN_HEADS * HGRN_DV
ATTN_HEADS = 16
ATTN_KV_HEADS = 4
HEAD_DIM = 64
WINDOW = 128
ROPE_THETA = 10000.0
D_Q = ATTN_HEADS * HEAD_DIM
D_KV = ATTN_KV_HEADS * HEAD_DIM
RWKV_HEADS = 16
RWKV_N = 64
D_RWKV = RWKV_HEADS * RWKV_N
RWKV_W_RANK = 64
RWKV_A_RANK = 64
RWKV_G_RANK = 64
RWKV_GN_EPS = 64e-5
D_SHIFT = 3 * D_RWKV + RWKV_W_RANK + RWKV_A_RANK + RWKV_G_RANK
D_IN_EVEN = 3 * D_CONV + 2 * D_HK + 2 * D_HV
D_MIX_EVEN = D_CONV + D_HV
D_IN_ODD = D_Q + 2 * D_KV + D_SHIFT
D_MIX_ODD = D_Q + D_RWKV
N_GROUPS = 4
EXPERTS_PER_GROUP = 8
N_EXPERTS = N_GROUPS * EXPERTS_PER_GROUP
TOP_K = 2
D_EXPERT = 512
MOE_BLOCK = 128
ALPHA = (2 * DEPTH) ** 0.25
BETA = (8 * DEPTH) ** -0.25
LN_EPS = 1e-5
RMS_EPS = 1e-6

kernel_name = 'hybrid_conv_hgrn2_swa_rwkv7_hmoe_step'


def layer_norm(z, g, b):
    zf = z.astype(jnp.float32)
    mu = jnp.mean(zf, axis=-1, keepdims=True)
    var = jnp.mean(jnp.square(zf - mu), axis=-1, keepdims=True)
    return ((zf - mu) * lax.rsqrt(var + LN_EPS) * g.astype(jnp.float32) + b.astype(jnp.float32)).astype(z.dtype)


def rope(z, pos):
    half = HEAD_DIM // 2
    inv = jnp.exp(-math.log(ROPE_THETA) * jnp.arange(half, dtype=jnp.float32) / half)
    ang = pos.astype(jnp.float32)[:, None] * inv[None, :]
    cos = jnp.cos(ang)[None, :, None, :]
    sin = jnp.sin(ang)[None, :, None, :]
    z1 = z[..., :half].astype(jnp.float32)
    z2 = z[..., half:].astype(jnp.float32)
    return jnp.concatenate([z1 * cos - z2 * sin, z2 * cos + z1 * sin], axis=-1).astype(z.dtype)


def short_conv(u, buf, w):
    T = u.shape[1]
    ext = jnp.concatenate([buf.astype(u.dtype), u], axis=1)
    y = w[0] * ext[:, 0:T]
    for j in range(1, CONV_WIDTH):
        y = y + w[j] * ext[:, j:j + T]
    return y, ext[:, T:]


def hgrn2_chunked(q, logf, k, v, s0):
    B, T, H, DK = q.shape
    DV = v.shape[-1]
    L = math.gcd(T, HGRN_CHUNK)
    nc = T // L

    def blocks(z):
        return jnp.moveaxis(z.astype(jnp.float32).reshape(B, nc, L, H, z.shape[-1]), 1, 0)

    qc, bc, kc, vc = blocks(q), blocks(logf), blocks(k), blocks(v)
    bc = jnp.cumsum(bc, axis=2)
    causal = jnp.tril(jnp.ones((L, L), dtype=bool))

    def step(S, xs):
        qb, bb, kb, vb = xs
        o_inter = jnp.einsum('bthk,bhkv->bthv', qb * jnp.exp(bb), S)
        diff = bb[:, :, None] - bb[:, None, :]
        decay = jnp.exp(jnp.where(causal[None, :, :, None, None], diff, -jnp.inf))
        scores = jnp.einsum('bthk,btshk,bshk->bhts', qb, decay, kb)
        o_intra = jnp.einsum('bhts,bshv->bthv', scores, vb)
        b_last = bb[:, -1]
        S_new = jnp.exp(b_last)[..., None] * S + jnp.einsum('bshk,bshv->bhkv', kb * jnp.exp(b_last[:, None] - bb), vb)
        return S_new, o_inter + o_intra

    S_final, o = lax.scan(step, s0.astype(jnp.float32), (qc, bc, kc, vc))
    return jnp.moveaxis(o, 0, 1).reshape(B, T, H, DV), S_final


def rwkv7_scan(r, decay, k, v, kk, a, s0):
    def tm(z):
        return jnp.moveaxis(z.astype(jnp.float32), 1, 0)

    def step(S, xs):
        r_t, w_t, k_t, v_t, kk_t, a_t = xs
        sa = jnp.einsum('bhij,bhj->bhi', S, -kk_t)
        S = S * w_t[:, :, None, :] + sa[..., None] * (kk_t * a_t)[:, :, None, :] + v_t[..., None] * k_t[:, :, None, :]
        return S, jnp.einsum('bhij,bhj->bhi', S, r_t)

    S_final, o = lax.scan(step, s0.astype(jnp.float32), (tm(r), tm(decay), tm(k), tm(v), tm(kk), tm(a)))
    return jnp.moveaxis(o, 0, 1), S_final


def sink_attention(q, k, v, q_pos, k_pos, sinks):
    B, N, TQ, H, HD = q.shape
    KVH = k.shape[3]
    G = H // KVH
    qg = q.reshape(B, N, TQ, KVH, G, HD)
    s = jnp.einsum('bnqkgd,bnskd->bnkgqs', qg, k).astype(jnp.float32) * (HD ** -0.5)
    delta = q_pos[:, :, None] - k_pos[:, None, :]
    valid = (delta >= 0) & (delta <= WINDOW) & (k_pos[:, None, :] >= 0)
    s = jnp.where(valid[None, :, None, None], s, -jnp.inf)
    sink = jnp.broadcast_to(sinks.astype(jnp.float32).reshape(1, 1, KVH, G, 1, 1), s.shape[:-1] + (1,))
    p = jax.nn.softmax(jnp.concatenate([s, sink], axis=-1), axis=-1)[..., :-1]
    o = jnp.einsum('bnkgqs,bnskd->bnqkgd', p.astype(v.dtype), v)
    return o.reshape(B, N, TQ, H * HD)


def even_mixer(h, w_in, w_out, conv_w, lb, norm_w, conv_buf, s0):
    B, T, _ = h.shape
    idx = [D_CONV, 2 * D_CONV, 3 * D_CONV, 3 * D_CONV + D_HK, 3 * D_CONV + 2 * D_HK, 3 * D_CONV + 2 * D_HK + D_HV]
    v_a, b_gate, c_gate, q, f_raw, i_in, g = jnp.split(h @ w_in, idx, axis=-1)
    conv_out, new_buf = short_conv(c_gate * v_a, conv_buf, conv_w)
    a_out = b_gate * conv_out
    lbf = lb.astype(jnp.float32)
    fr = f_raw.astype(jnp.float32)
    logf = jnp.log(lbf + (1.0 - lbf) * jax.nn.sigmoid(fr))
    k_in = (1.0 - lbf) * jax.nn.sigmoid(-fr)

    def hs(z, d):
        return z.reshape(B, T, HGRN_HEADS, d)

    o, S_new = hgrn2_chunked(hs(jax.nn.silu(q), HGRN_DK), hs(logf, HGRN_DK), hs(k_in, HGRN_DK), hs(i_in, HGRN_DV), s0)
    o = o * lax.rsqrt(jnp.mean(o * o, axis=-1, keepdims=True) + RMS_EPS) * norm_w.astype(jnp.float32)
    b_out = o.reshape(B, T, D_HV).astype(h.dtype) * jax.nn.silu(g)
    y = jnp.concatenate([a_out, b_out], axis=-1) @ w_out
    return y, new_buf, S_new.astype(h.dtype)


def odd_mixer(h, pos, j, P, kv_cache, s0, shift_buf):
    B, T, _ = h.shape
    q, k, v, p_d = jnp.split(h @ P['w_in_odd'][j], [D_Q, D_Q + D_KV, D_Q + 2 * D_KV], axis=-1)
    q = rope(q.reshape(B, T, ATTN_HEADS, HEAD_DIM), pos)
    k = rope(k.reshape(B, T, ATTN_KV_HEADS, HEAD_DIM), pos)
    v = v.reshape(B, T, ATTN_KV_HEADS, HEAD_DIM)
    sinks = P['attn_sinks'][j]
    if kv_cache is None:
        nb = T // WINDOW

        def band(z):
            zb = z.reshape(B, nb, WINDOW, ATTN_KV_HEADS, HEAD_DIM)
            prev = jnp.concatenate([jnp.zeros_like(zb[:, :1]), zb[:, :-1]], axis=1)
            return jnp.concatenate([prev, zb], axis=2)

        q_pos = pos.reshape(nb, WINDOW)
        k_pos = q_pos[:, :1] - WINDOW + jnp.arange(2 * WINDOW, dtype=jnp.int32)[None, :]
        attn = sink_attention(q.reshape(B, nb, WINDOW, ATTN_HEADS, HEAD_DIM), band(k), band(v), q_pos, k_pos, sinks)
        new_k, new_v = k[:, T - WINDOW:], v[:, T - WINDOW:]
    else:
        k_all = jnp.concatenate([kv_cache[0].astype(k.dtype), k], axis=1)
        v_all = jnp.concatenate([kv_cache[1].astype(v.dtype), v], axis=1)
        k_pos = jnp.concatenate([pos[0] - WINDOW + jnp.arange(WINDOW, dtype=jnp.int32), pos])
        attn = sink_attention(q[:, None], k_all[:, None], v_all[:, None], pos[None], k_pos[None], sinks)
        new_k, new_v = k_all[:, T:], v_all[:, T:]
    attn = attn.reshape(B, T, D_Q)

    prev = shift_buf.astype(p_d.dtype)[:, None]
    shifted = jnp.concatenate([prev, p_d[:, :-1]], axis=1)
    xs = p_d + P['rwkv_mu'][j] * (shifted - p_d)
    r, kr, vr, wd, ad, gd = jnp.split(
        xs, [D_RWKV, 2 * D_RWKV, 3 * D_RWKV, 3 * D_RWKV + RWKV_W_RANK, 3 * D_RWKV + RWKV_W_RANK + RWKV_A_RANK], axis=-1)
    w_log = -jax.nn.softplus(-(P['rwkv_w0'][j] + jnp.tanh(wd) @ P['rwkv_w2'][j]).astype(jnp.float32)) - 0.5
    decay = jnp.exp(-jnp.exp(w_log))
    a = jax.nn.sigmoid((P['rwkv_a0'][j] + ad @ P['rwkv_a2'][j]).astype(jnp.float32))
    g = jax.nn.sigmoid(gd) @ P['rwkv_g2'][j]

    def heads(z):
        return z.reshape(B, T, RWKV_HEADS, RWKV_N)

    kf = kr.astype(jnp.float32)
    kk = heads(kf * P['rwkv_kk'][j].astype(jnp.float32))
    kk = kk / jnp.maximum(jnp.sqrt(jnp.sum(kk * kk, axis=-1, keepdims=True)), 1e-12)
    kf = kf * (1.0 + (a - 1.0) * P['rwkv_ka'][j].astype(jnp.float32))
    rf = r.astype(jnp.float32)
    vf = vr.astype(jnp.float32)
    o, S_new = rwkv7_scan(heads(rf), heads(decay), heads(kf), heads(vf), kk, heads(a), s0)
    mu_o = jnp.mean(o, axis=-1, keepdims=True)
    var_o = jnp.mean(jnp.square(o - mu_o), axis=-1, keepdims=True)
    o = ((o - mu_o) * lax.rsqrt(var_o + RWKV_GN_EPS)).reshape(B, T, D_RWKV) * P['rwkv_lnx_g'][j] + P['rwkv_lnx_b'][j]
    bonus = jnp.sum(heads(rf) * heads(kf) * P['rwkv_rk'][j].astype(jnp.float32), axis=-1, keepdims=True) * heads(vf)
    o = (o + bonus.reshape(B, T, D_RWKV)) * g
    y = jnp.concatenate([attn, o.astype(h.dtype)], axis=-1) @ P['w_out_odd'][j]
    return y, new_k, new_v, S_new.astype(h.dtype), p_d[:, -1]


def hier_moe(h, w_grp, b_grp, w_exp, b_exp, w1, w3, w2):
    B, T, D = h.shape
    n = B * T
    xf = h.reshape(n, D)
    grp_p = jax.nn.softmax((xf @ w_grp + b_grp).astype(jnp.float32), axis=-1)
    g_idx = jnp.argmax(grp_p, axis=-1).astype(jnp.int32)
    p_g = jnp.take_along_axis(grp_p, g_idx[:, None], axis=1)
    e_logit = (xf @ w_exp + b_exp).astype(jnp.float32).reshape(n, N_GROUPS, EXPERTS_PER_GROUP)
    in_grp = jnp.take_along_axis(e_logit, g_idx[:, None, None], axis=1)[:, 0]
    top_v, top_i = lax.top_k(in_grp, TOP_K)
    gate = jax.nn.softmax(top_v, axis=-1) * p_g
    expert_id = (g_idx[:, None] * EXPERTS_PER_GROUP + top_i).reshape(-1).astype(jnp.int32)
    nk = n * TOP_K
    token = jnp.arange(nk, dtype=jnp.int32) // TOP_K
    onehot = (expert_id[:, None] == jnp.arange(N_EXPERTS, dtype=jnp.int32)[None, :]).astype(jnp.int32)
    rank = jnp.sum(jnp.cumsum(onehot, axis=0) * onehot, axis=-1) - 1
    counts = jnp.sum(onehot, axis=0)
    padded = (counts + MOE_BLOCK - 1) // MOE_BLOCK * MOE_BLOCK
    pad_end = jnp.cumsum(padded)
    dest = pad_end[expert_id] - padded[expert_id] + rank
    n_blocks = -(-nk // MOE_BLOCK) + N_EXPERTS
    xd = jnp.zeros((n_blocks * MOE_BLOCK, D), h.dtype).at[dest].set(xf[token])
    block_start = jnp.arange(n_blocks, dtype=jnp.int32) * MOE_BLOCK
    block_expert = jnp.minimum(jnp.sum(block_start[:, None] >= pad_end[None, :], axis=-1), N_EXPERTS - 1)

    def run(args):
        xb, e = args
        return (jax.nn.silu(xb @ w1[e]) * (xb @ w3[e])) @ w2[e]

    yd = lax.map(run, (xd.reshape(n_blocks, MOE_BLOCK, D), block_expert)).reshape(n_blocks * MOE_BLOCK, D)
    out = jnp.sum(yd[dest].reshape(n, TOP_K, D) * gate[..., None].astype(h.dtype), axis=1)
    return out.reshape(B, T, D)


def trunk(x, c, pos, P, conv_st, hgrn_st, k_cache, v_cache, rwkv_st, shift_st):
    lb_soft = jax.nn.softmax(P['hgrn_lb'].astype(jnp.float32), axis=0)
    lb_all = jnp.cumsum(lb_soft, axis=0) - lb_soft[0:1]
    conv_new, hgrn_new, k_new, v_new, rwkv_new, shift_new = [], [], [], [], [], []
    for l in range(DEPTH):
        j = l // 2
        mod = jax.nn.silu(c) @ P['ada_w'][l] + P['ada_b'][l]
        sh1, sc1, gt1, sh2, sc2, gt2 = jnp.split(mod[:, None, :], 6, axis=-1)
        h = x * (1 + sc1) + sh1
        if l % 2 == 0:
            y, cb, hsn = even_mixer(h, P['w_in_even'][j], P['w_out_even'][j], P['conv_w'][j], lb_all[j],
                                    P['hgrn_norm'][j], conv_st[j], hgrn_st[j])
            conv_new.append(cb)
            hgrn_new.append(hsn)
        else:
            kv = None if k_cache is None else (k_cache[j], v_cache[j])
            y, nk_, nv_, rs, sb = odd_mixer(h, pos, j, P, kv, rwkv_st[j], shift_st[j])
            k_new.append(nk_)
            v_new.append(nv_)
            rwkv_new.append(rs)
            shift_new.append(sb)
        x = layer_norm(ALPHA * x + (1 + gt1) * y, P['ln_g'][l, 0], P['ln_b'][l, 0])
        h = x * (1 + sc2) + sh2
        y = hier_moe(h, P['moe_w_grp'][l], P['moe_b_grp'][l], P['moe_w_exp'][l], P['moe_b_exp'][l],
                     P['moe_w1'][l], P['moe_w3'][l], P['moe_w2'][l])
        x = layer_norm(ALPHA * x + (1 + gt2) * y, P['ln_g'][l, 1], P['ln_b'][l, 1])
    return x, jnp.stack(conv_new), jnp.stack(hgrn_new), jnp.stack(k_new), jnp.stack(v_new), jnp.stack(rwkv_new), jnp.stack(shift_new)


def setup_inputs(seed: int = 0) -> dict:
    key = jax.random.key(seed)
    ks = iter(jax.random.split(key, 48))

    def nrm(shape, scale=1.0):
        return scale * jax.random.normal(next(ks), shape, jnp.float32)

    fan = D_MODEL ** -0.5
    return {
        'x_prompt': nrm((BATCH, SEQ, D_MODEL)),
        'x_sample': nrm((DEC_BATCH, DEC_SEQ, D_MODEL)),
        'c_prompt': nrm((BATCH, D_MODEL)),
        'c_sample': nrm((DEC_BATCH, D_MODEL)),
        'state_conv': nrm((N_EVEN, DEC_BATCH, CONV_WIDTH - 1, D_CONV)),
        'state_hgrn': nrm((N_EVEN, DEC_BATCH, HGRN_HEADS, HGRN_DK, HGRN_DV), 0.3),
        'cache_swa_k': nrm((N_ODD, DEC_BATCH, WINDOW, ATTN_KV_HEADS, HEAD_DIM)),
        'cache_swa_v': nrm((N_ODD, DEC_BATCH, WINDOW, ATTN_KV_HEADS, HEAD_DIM)),
        'state_rwkv': nrm((N_ODD, DEC_BATCH, RWKV_HEADS, RWKV_N, RWKV_N), 0.3),
        'state_shift': nrm((N_ODD, DEC_BATCH, D_SHIFT)),
        'ada_w': nrm((DEPTH, D_MODEL, 6 * D_MODEL), 0.2 * fan),
        'ada_b': nrm((DEPTH, 6 * D_MODEL), 0.02),
        'ln_g': 1.0 + nrm((DEPTH, 2, D_MODEL), 0.05),
        'ln_b': nrm((DEPTH, 2, D_MODEL), 0.02),
        'w_in_even': nrm((N_EVEN, D_MODEL, D_IN_EVEN), fan),
        'w_out_even': nrm((N_EVEN, D_MIX_EVEN, D_MODEL), BETA * D_MIX_EVEN ** -0.5),
        'conv_w': nrm((N_EVEN, CONV_WIDTH, D_CONV), CONV_WIDTH ** -0.5),
        'hgrn_lb': nrm((N_EVEN, D_HK), 0.5),
        'hgrn_norm': 1.0 + nrm((N_EVEN, HGRN_DV), 0.05),
        'w_in_odd': nrm((N_ODD, D_MODEL, D_IN_ODD), fan),
        'w_out_odd': nrm((N_ODD, D_MIX_ODD, D_MODEL), BETA * D_MIX_ODD ** -0.5),
        'attn_sinks': nrm((N_ODD, ATTN_HEADS), 0.5),
        'rwkv_mu': jax.random.uniform(next(ks), (N_ODD, D_SHIFT), jnp.float32),
        'rwkv_w0': nrm((N_ODD, D_RWKV), 0.5),
        'rwkv_w2': nrm((N_ODD, RWKV_W_RANK, D_RWKV), 0.5 * RWKV_W_RANK ** -0.5),
        'rwkv_a0': nrm((N_ODD, D_RWKV), 0.5),
        'rwkv_a2': nrm((N_ODD, RWKV_A_RANK, D_RWKV), 0.5 * RWKV_A_RANK ** -0.5),
        'rwkv_g2': nrm((N_ODD, RWKV_G_RANK, D_RWKV), RWKV_G_RANK ** -0.5),
        'rwkv_kk': 0.85 + nrm((N_ODD, D_RWKV), 0.05),
        'rwkv_ka': 1.0 + nrm((N_ODD, D_RWKV), 0.05),
        'rwkv_rk': nrm((N_ODD, RWKV_HEADS, RWKV_N), 0.1),
        'rwkv_lnx_g': 1.0 + nrm((N_ODD, D_RWKV), 0.05),
        'rwkv_lnx_b': nrm((N_ODD, D_RWKV), 0.02),
        'moe_w_grp': nrm((DEPTH, D_MODEL, N_GROUPS), fan),
        'moe_b_grp': nrm((DEPTH, N_GROUPS), 0.01),
        'moe_w_exp': nrm((DEPTH, D_MODEL, N_EXPERTS), fan),
        'moe_b_exp': nrm((DEPTH, N_EXPERTS), 0.01),
        'moe_w1': nrm((DEPTH, N_EXPERTS, D_MODEL, D_EXPERT), fan),
        'moe_w3': nrm((DEPTH, N_EXPERTS, D_MODEL, D_EXPERT), fan),
        'moe_w2': nrm((DEPTH, N_EXPERTS, D_EXPERT, D_MODEL), BETA * D_EXPERT ** -0.5),
    }


def reference(x_prompt, x_sample, c_prompt, c_sample, state_conv, state_hgrn, cache_swa_k, cache_swa_v,
              state_rwkv, state_shift, ada_w, ada_b, ln_g, ln_b, w_in_even, w_out_even, conv_w, hgrn_lb,
              hgrn_norm, w_in_odd, w_out_odd, attn_sinks, rwkv_mu, rwkv_w0, rwkv_w2, rwkv_a0, rwkv_a2,
              rwkv_g2, rwkv_kk, rwkv_ka, rwkv_rk, rwkv_lnx_g, rwkv_lnx_b, moe_w_grp, moe_b_grp, moe_w_exp,
              moe_b_exp, moe_w1, moe_w3, moe_w2):
    P = dict(ada_w=ada_w, ada_b=ada_b, ln_g=ln_g, ln_b=ln_b, w_in_even=w_in_even, w_out_even=w_out_even,
             conv_w=conv_w, hgrn_lb=hgrn_lb, hgrn_norm=hgrn_norm, w_in_odd=w_in_odd, w_out_odd=w_out_odd,
             attn_sinks=attn_sinks, rwkv_mu=rwkv_mu, rwkv_w0=rwkv_w0, rwkv_w2=rwkv_w2, rwkv_a0=rwkv_a0,
             rwkv_a2=rwkv_a2, rwkv_g2=rwkv_g2, rwkv_kk=rwkv_kk, rwkv_ka=rwkv_ka, rwkv_rk=rwkv_rk,
             rwkv_lnx_g=rwkv_lnx_g, rwkv_lnx_b=rwkv_lnx_b, moe_w_grp=moe_w_grp, moe_b_grp=moe_b_grp,
             moe_w_exp=moe_w_exp, moe_b_exp=moe_b_exp, moe_w1=moe_w1, moe_w3=moe_w3, moe_w2=moe_w2)
    bp, tp = x_prompt.shape[0], x_prompt.shape[1]
    dt = x_prompt.dtype
    y_prompt, conv_p, hgrn_p, swa_k_p, swa_v_p, rwkv_p, shift_p = trunk(
        x_prompt, c_prompt, jnp.arange(tp, dtype=jnp.int32), P,
        jnp.zeros((N_EVEN, bp, CONV_WIDTH - 1, D_CONV), dt),
        jnp.zeros((N_EVEN, bp, HGRN_HEADS, HGRN_DK, HGRN_DV), dt),
        None, None,
        jnp.zeros((N_ODD, bp, RWKV_HEADS, RWKV_N, RWKV_N), dt),
        jnp.zeros((N_ODD, bp, D_SHIFT), dt))
    ts = x_sample.shape[1]
    y_sample, conv_s, hgrn_s, swa_k_s, swa_v_s, rwkv_s, shift_s = trunk(
        x_sample, c_sample, PAST_LEN + jnp.arange(ts, dtype=jnp.int32), P,
        state_conv, state_hgrn, cache_swa_k, cache_swa_v, state_rwkv, state_shift)
    return (y_prompt, y_sample, conv_p, hgrn_p, swa_k_p, swa_v_p, rwkv_p, shift_p,
            conv_s, hgrn_s, swa_k_s, swa_v_s, rwkv_s, shift_s)
```

```python
import functools
import math

import jax
import jax.numpy as jnp
from jax import lax
from jax.experimental import pallas as pl
from jax.experimental.pallas import tpu as pltpu

F32 = jnp.float32
BF16 = jnp.bfloat16
I32 = jnp.int32

D_MODEL = 2048
DEPTH = 4
PAST_LEN = 16384
N_EVEN = (DEPTH + 1) // 2
N_ODD = DEPTH // 2
D_CONV = 1024
CONV_WIDTH = 3
HGRN_HEADS = 8
HGRN_DK = 128
HGRN_DV = 128
D_HK = HGRN_HEADS * HGRN_DK
D_HV = HGRN_HEADS * HGRN_DV
ATTN_HEADS = 16
ATTN_KV_HEADS = 4
HEAD_DIM = 64
WINDOW = 128
ROPE_THETA = 10000.0
D_Q = ATTN_HEADS * HEAD_DIM
D_KV = ATTN_KV_HEADS * HEAD_DIM
RWKV_HEADS = 16
RWKV_N = 64
D_RWKV = RWKV_HEADS * RWKV_N
RWKV_RANK = 64
RWKV_GN_EPS = 64e-5
D_SHIFT = 3 * D_RWKV + 3 * RWKV_RANK
D_IN_EVEN = 3 * D_CONV + 2 * D_HK + 2 * D_HV
D_IN_ODD = D_Q + 2 * D_KV + D_SHIFT
N_GROUPS = 4
EXPERTS_PER_GROUP = 8
N_EXPERTS = N_GROUPS * EXPERTS_PER_GROUP
D_EXPERT = 512
ALPHA = (2 * DEPTH) ** 0.25
LN_EPS = 1e-5
RMS_EPS = 1e-6

LANES = 128
MOE_ROWS = 128
VMEM_LIMIT = 48 * 1024 * 1024

D_IN_ODD_PAD = -(-D_IN_ODD // LANES) * LANES
D_SHIFT_PAD = D_IN_ODD_PAD - (D_Q + 2 * D_KV)

NN = (((1,), (0,)), ((), ()))
NT = (((1,), (1,)), ((), ()))
TN = (((0,), (0,)), ((), ()))


def _cp(sem, vmem=VMEM_LIMIT):
    return pltpu.CompilerParams(dimension_semantics=sem, vmem_limit_bytes=vmem)


def _dot(a, b, dims=NN):
    return lax.dot_general(a.astype(BF16), b.astype(BF16), dims, preferred_element_type=F32)


def _split2(x):
    hi = x.astype(BF16)
    lo = (x - hi.astype(F32)).astype(BF16)
    return hi, lo


def _dot3(a, b, dims=NN):
    ah, al = _split2(a)
    bh, bl = _split2(b)
    d = lambda x, y: lax.dot_general(x, y, dims, preferred_element_type=F32)
    return d(ah, bh) + (d(ah, bl) + d(al, bh))


def _dot_exact_lhs(a_bf16, b, dims=NN):
    b1 = b.astype(BF16)
    r1 = b - b1.astype(F32)
    b2 = r1.astype(BF16)
    b3 = (r1 - b2.astype(F32)).astype(BF16)
    d = lambda y: lax.dot_general(a_bf16, y, dims, preferred_element_type=F32)
    return d(b1) + (d(b2) + d(b3))


def _dot_exact_rhs(a, b_bf16, dims=NN):
    a1 = a.astype(BF16)
    r1 = a - a1.astype(F32)
    a2 = r1.astype(BF16)
    d = lambda x: lax.dot_general(x, b_bf16, dims, preferred_element_type=F32)
    return d(a1) + d(a2)


def _sigmoid(x):
    return 1.0 / (1.0 + jnp.exp(-x))


def _silu(x):
    return x * _sigmoid(x)


def _softplus(x):
    return jnp.maximum(x, 0.0) + jnp.log(1.0 + jnp.exp(-jnp.abs(x)))


def _iota(shape, axis):
    return lax.broadcasted_iota(I32, shape, axis)


def _layer_norm(u, g, b):
    mu = jnp.mean(u, axis=-1, keepdims=True)
    d = u - mu
    var = jnp.mean(d * d, axis=-1, keepdims=True)
    return d * lax.rsqrt(var + LN_EPS) * g + b


def _ada_body(c_ref, w_ref, b_ref, o_ref):
    o_ref[...] = _dot(_silu(c_ref[...]), w_ref[...]) + b_ref[...]


def ada_mod(c_all, ada_w, ada_b, tn=1024):
    nl, d, n = ada_w.shape
    r = c_all.shape[0]
    return pl.pallas_call(
        _ada_body,
        grid=(nl, n // tn),
        in_specs=[pl.BlockSpec((r, d), lambda l, j: (0, 0)),
                  pl.BlockSpec((None, d, tn), lambda l, j: (l, 0, j)),
                  pl.BlockSpec((None, 1, tn), lambda l, j: (l, 0, j))],
        out_specs=pl.BlockSpec((None, r, tn), lambda l, j: (l, 0, j)),
        out_shape=jax.ShapeDtypeStruct((nl, r, n), F32),
        compiler_params=_cp(("parallel", "parallel")),
        name="ada_mod",
    )(c_all, ada_w, ada_b.reshape(nl, 1, n))


class _Group:
    def __init__(self, n, t, tm, modarr, per_row):
        self.n, self.t, self.tm, self.modarr, self.per_row = n, t, tm, modarr, per_row

    def mod(self, l, c):
        d = self.modarr.shape[-1]
        if self.per_row:
            return pl.BlockSpec((None, None, self.tm, d), lambda i: (l, c, i, 0))
        t, tm = self.t, self.tm
        return pl.BlockSpec((None, None, None, 1, d), lambda i: (l, c, (i * tm) // t, 0, 0))


def _modulate_body(x_ref, sc_ref, sh_ref, h_ref):
    h_ref[...] = (x_ref[...] * (1.0 + sc_ref[...]) + sh_ref[...]).astype(h_ref.dtype)


def modulate(x, grp, l, c_sc, c_sh):
    n, d = x.shape
    tm = grp.tm
    row = pl.BlockSpec((tm, d), lambda i: (i, 0))
    return pl.pallas_call(
        _modulate_body, grid=(n // tm,),
        in_specs=[row, grp.mod(l, c_sc), grp.mod(l, c_sh)],
        out_specs=row, out_shape=jax.ShapeDtypeStruct((n, d), BF16),
        compiler_params=_cp(("parallel",)), name="modulate",
    )(x, grp.modarr, grp.modarr)


def _mm_body(a_ref, w_ref, o_ref):
    o_ref[...] = jnp.dot(a_ref[...], w_ref[...], preferred_element_type=F32)


def matmul(a, w, tm, tn):
    n, k = a.shape
    nn = w.shape[1]
    return pl.pallas_call(
        _mm_body, grid=(n // tm, nn // tn),
        in_specs=[pl.BlockSpec((tm, k), lambda i, j: (i, 0)),
                  pl.BlockSpec((k, tn), lambda i, j: (0, j))],
        out_specs=pl.BlockSpec((tm, tn), lambda i, j: (i, j)),
        out_shape=jax.ShapeDtypeStruct((n, nn), F32),
        compiler_params=_cp(("parallel", "parallel")), name="in_proj",
    )(a, w)


def _outln_body(a_ref, b_ref, wa_ref, wb_ref, x_ref, gt_ref, g_ref, be_ref, sc_ref, sh_ref, xo_ref, ho_ref):
    y = (jnp.dot(a_ref[...], wa_ref[...], preferred_element_type=F32)
         + jnp.dot(b_ref[...], wb_ref[...], preferred_element_type=F32))
    xn = _layer_norm(ALPHA * x_ref[...] + (1.0 + gt_ref[...]) * y, g_ref[...], be_ref[...])
    xo_ref[...] = xn
    ho_ref[...] = xn * (1.0 + sc_ref[...]) + sh_ref[...]


def out_proj_ln(a, b, wa, wb, x, grp, l, ln_g, ln_b):
    n, d = x.shape
    ka, kb = a.shape[1], b.shape[1]
    tm = min(grp.tm, 256)
    g2 = _Group(grp.n, grp.t, tm, grp.modarr, grp.per_row)
    row = pl.BlockSpec((tm, d), lambda i: (i, 0))
    vec = pl.BlockSpec((1, d), lambda i: (0, 0))
    return pl.pallas_call(
        _outln_body, grid=(n // tm,),
        in_specs=[pl.BlockSpec((tm, ka), lambda i: (i, 0)), pl.BlockSpec((tm, kb), lambda i: (i, 0)),
                  pl.BlockSpec((ka, d), lambda i: (0, 0)), pl.BlockSpec((kb, d), lambda i: (0, 0)),
                  row, g2.mod(l, 2), vec, vec, g2.mod(l, 4), g2.mod(l, 3)],
        out_specs=[row, row],
        out_shape=[jax.ShapeDtypeStruct((n, d), F32), jax.ShapeDtypeStruct((n, d), F32)],
        compiler_params=_cp(("parallel",)), name="out_proj_ln",
    )(a, b, wa, wb, x, grp.modarr, ln_g.reshape(1, d), ln_b.reshape(1, d), grp.modarr, grp.modarr)


def _comb_body(*refs, has_next):
    if has_next:
        y0_ref, y1_ref, gate_ref, x_ref, gt_ref, g_ref, be_ref, sc_ref, sh_ref, xo_ref, ho_ref = refs
    else:
        y0_ref, y1_ref, gate_ref, x_ref, gt_ref, g_ref, be_ref, xo_ref = refs
    gate = gate_ref[...]
    y = y0_ref[...] * gate[:, 0:1] + y1_ref[...] * gate[:, 1:2]
    xn = _layer_norm(ALPHA * x_ref[...] + (1.0 + gt_ref[...]) * y, g_ref[...], be_ref[...])
    xo_ref[...] = xn
    if has_next:
        ho_ref[...] = (xn * (1.0 + sc_ref[...]) + sh_ref[...]).astype(ho_ref.dtype)


def moe_combine_ln(ys, gate, x, grp, l, ln_g, ln_b, has_next):
    n, d = x.shape
    tm = min(grp.tm, 256)
    g2 = _Group(grp.n, grp.t, tm, grp.modarr, grp.per_row)
    nt = n // tm
    row = pl.BlockSpec((tm, d), lambda i: (i, 0))
    vec = pl.BlockSpec((1, d), lambda i: (0, 0))
    in_specs = [row, pl.BlockSpec((tm, d), lambda i: (i + nt, 0)), pl.BlockSpec((tm, 2), lambda i: (i, 0)),
                row, g2.mod(l, 5), vec, vec]
    args = [ys, ys, gate, x, grp.modarr, ln_g.reshape(1, d), ln_b.reshape(1, d)]
    out_specs = [row]
    out_shape = [jax.ShapeDtypeStruct((n, d), F32)]
    if has_next:
        in_specs += [g2.mod(l + 1, 1), g2.mod(l + 1, 0)]
        args += [grp.modarr, grp.modarr]
        out_specs.append(row)
        out_shape.append(jax.ShapeDtypeStruct((n, d), BF16))
    res = pl.pallas_call(
        functools.partial(_comb_body, has_next=has_next), grid=(nt,),
        in_specs=in_specs, out_specs=out_specs, out_shape=out_shape,
        compiler_params=_cp(("parallel",)), name="moe_combine_ln",
    )(*args)
    return (res[0], res[1]) if has_next else (res[0], None)


def _router_body(h_ref, w_ref, b_ref, eid_ref, gate_ref, rank_ref, cnt_ref, run_ref):
    i = pl.program_id(0)

    @pl.when(i == 0)
    def _():
        run_ref[...] = jnp.zeros_like(run_ref)

    tm = h_ref.shape[0]
    logits = _dot3(h_ref[...], w_ref[...]) + b_ref[...]
    lg = logits[:, :LANES]
    le = logits[:, LANES:]
    lane = _iota((tm, LANES), 1).astype(F32)
    neg = jnp.float32(-jnp.inf)
    lgm = jnp.where(lane < N_GROUPS, lg, neg)
    mg = jnp.max(lgm, axis=-1, keepdims=True)
    gidx = jnp.min(jnp.where(lgm == mg, lane, float(LANES)), axis=-1, keepdims=True)
    p_g = 1.0 / jnp.sum(jnp.exp(lgm - mg), axis=-1, keepdims=True)
    lo = gidx * EXPERTS_PER_GROUP
    in_grp = (lane >= lo) & (lane < lo + EXPERTS_PER_GROUP)
    lem = jnp.where(in_grp, le, neg)
    v1 = jnp.max(lem, axis=-1, keepdims=True)
    i1 = jnp.min(jnp.where(lem == v1, lane, float(LANES)), axis=-1, keepdims=True)
    lem2 = jnp.where(lane == i1, neg, lem)
    v2 = jnp.max(lem2, axis=-1, keepdims=True)
    i2 = jnp.min(jnp.where(lem2 == v2, lane, float(LANES)), axis=-1, keepdims=True)
    e2 = jnp.exp(v2 - v1)
    g1 = p_g / (1.0 + e2)
    g2 = p_g * e2 / (1.0 + e2)
    oh1 = jnp.where(lane == i1, 1.0, 0.0)
    oh2 = jnp.where(lane == i2, 1.0, 0.0)
    comb = oh1 + oh2
    tri = jnp.where(_iota((tm, tm), 0) > _iota((tm, tm), 1), 1.0, 0.0).astype(BF16)
    before = jnp.dot(tri, comb.astype(BF16), preferred_element_type=F32) + run_ref[...]
    r1 = jnp.sum(before * oh1, axis=-1, keepdims=True)
    r2 = jnp.sum(before * oh2, axis=-1, keepdims=True)
    run_ref[...] = run_ref[...] + jnp.sum(comb, axis=0, keepdims=True)
    eid_ref[:, 0:1] = i1.astype(I32)
    eid_ref[:, 1:2] = i2.astype(I32)
    gate_ref[:, 0:1] = g1
    gate_ref[:, 1:2] = g2
    rank_ref[:, 0:1] = r1.astype(I32)
    rank_ref[:, 1:2] = r2.astype(I32)
    cnt_ref[...] = run_ref[...]


def router(h, wr, br, tm):
    n, d = h.shape
    two = pl.BlockSpec((tm, 2), lambda i: (i, 0))
    return pl.pallas_call(
        _router_body, grid=(n // tm,),
        in_specs=[pl.BlockSpec((tm, d), lambda i: (i, 0)),
                  pl.BlockSpec((d, 2 * LANES), lambda i: (0, 0)),
                  pl.BlockSpec((1, 2 * LANES), lambda i: (0, 0))],
        out_specs=[two, two, two, pl.BlockSpec((1, LANES), lambda i: (0, 0))],
        out_shape=[jax.ShapeDtypeStruct((n, 2), I32), jax.ShapeDtypeStruct((n, 2), F32),
                   jax.ShapeDtypeStruct((n, 2), I32), jax.ShapeDtypeStruct((1, LANES), F32)],
        scratch_shapes=[pltpu.VMEM((1, LANES), F32)],
        compiler_params=_cp(("arbitrary",)), name="router",
    )(h, wr, br)


def moe_plan(eid, rank, cnt, n):
    nk = 2 * n
    n_blocks = -(-nk // MOE_ROWS) + N_EXPERTS
    n_rows = n_blocks * MOE_ROWS
    counts = cnt[0, :N_EXPERTS].astype(I32)
    padded = (counts + MOE_ROWS - 1) // MOE_ROWS * MOE_ROWS
    pad_end = jnp.cumsum(padded)
    pad_start = pad_end - padded
    dest = (pad_start[eid] + rank).reshape(-1)
    src = jnp.full((n_rows,), -1, I32).at[dest].set(jnp.arange(nk, dtype=I32))
    valid = src >= 0
    tok = jnp.where(valid, src >> 1, -1)
    dst = jnp.where(valid, (src & 1) * n + (src >> 1), -1)
    n_used = pad_end[-1] // MOE_ROWS
    blk = jnp.arange(n_blocks, dtype=I32)
    be = jnp.minimum(jnp.sum((blk * MOE_ROWS)[:, None] >= pad_end[None, :], axis=-1), N_EXPERTS - 1).astype(I32)
    be = jnp.where(blk < n_used, be, be[jnp.maximum(n_used - 1, 0)])
    return tok, dst, be, n_used.reshape(1).astype(I32), n_blocks


def _moe_body(tok_ref, dst_ref, be_ref, nu_ref, h_hbm, w1_ref, w3_ref, w2_ref, ys_hbm, xbuf, ybuf, gsem, ssem):
    b = pl.program_id(0)
    n_used = nu_ref[0]

    def gather_copy(row_src, slot, i):
        return pltpu.make_async_copy(h_hbm.at[pl.ds(row_src, 1)], xbuf.at[slot, pl.ds(i, 1)], gsem.at[slot])

    def scatter_copy(row_dst, i):
        return pltpu.make_async_copy(ybuf.at[pl.ds(i, 1)], ys_hbm.at[pl.ds(row_dst, 1)], ssem.at[0])

    def for_live_rows(idx_ref, blk, fn):
        def body(i, c):
            idx = idx_ref[blk * MOE_ROWS + i]

            @pl.when(idx >= 0)
            def _():
                fn(idx, i)
            return c
        lax.fori_loop(0, MOE_ROWS, body, 0)

    def start_gather(blk, slot):
        for_live_rows(tok_ref, blk, lambda idx, i: gather_copy(idx, slot, i).start())

    def wait_gather(blk, slot):
        for_live_rows(tok_ref, blk, lambda idx, i: gather_copy(idx, slot, i).wait())

    def start_scatter(blk):
        for_live_rows(dst_ref, blk, lambda idx, i: scatter_copy(idx, i).start())

    def wait_scatter(blk):
        for_live_rows(dst_ref, blk, lambda idx, i: scatter_copy(idx, i).wait())

    slot = b % 2

    @pl.when(b == 0)
    def _():
        xbuf[...] = jnp.zeros_like(xbuf)
        start_gather(0, 0)

    @pl.when(b < n_used)
    def _():
        wait_gather(b, slot)

        @pl.when(b + 1 < n_used)
        def _():
            start_gather(b + 1, 1 - slot)

        x = xbuf[slot].astype(BF16)
        h1 = jnp.dot(x, w1_ref[...].astype(BF16), preferred_element_type=F32)
        h3 = jnp.dot(x, w3_ref[...].astype(BF16), preferred_element_type=F32)
        mid = (_silu(h1) * h3).astype(BF16)
        y = jnp.dot(mid, w2_ref[...].astype(BF16), preferred_element_type=F32)

        @pl.when(b > 0)
        def _():
            wait_scatter(b - 1)

        ybuf[...] = y
        start_scatter(b)

        @pl.when(b == n_used - 1)
        def _():
            wait_scatter(b)


def moe_ffn(h, tok, dst, be, n_used, n_blocks, w1, w3, w2):
    n, d = h.shape
    de = w1.shape[-1]
    wspec_in = pl.BlockSpec((None, d, de), lambda b, tok, dst, be, nu: (be[b], 0, 0))
    wspec_out = pl.BlockSpec((None, de, d), lambda b, tok, dst, be, nu: (be[b], 0, 0))
    return pl.pallas_call(
        _moe_body,
        grid_spec=pltpu.PrefetchScalarGridSpec(
            num_scalar_prefetch=4, grid=(n_blocks,),
            in_specs=[pl.BlockSpec(memory_space=pl.ANY), wspec_in, wspec_in, wspec_out],
            out_specs=pl.BlockSpec(memory_space=pl.ANY),
            scratch_shapes=[pltpu.VMEM((2, MOE_ROWS, d), F32), pltpu.VMEM((MOE_ROWS, d), F32),
                            pltpu.SemaphoreType.DMA((2,)), pltpu.SemaphoreType.DMA((1,))]),
        out_shape=jax.ShapeDtypeStruct((2 * n, d), F32),
        compiler_params=_cp(("arbitrary",)), name="moe_ffn",
    )(tok, dst, be, n_used, h, w1, w3, w2)


def _conv_body(va_ref, bg_ref, cg_ref, w_ref, buf_ref, o_ref, nb_ref):
    u = cg_ref[...] * va_ref[...]
    t = u.shape[0]
    row = _iota(u.shape, 0)
    b0 = buf_ref[0:1, :]
    b1 = buf_ref[1:2, :]
    u1 = jnp.where(row == 0, b1, pltpu.roll(u, 1, axis=0))
    u2 = jnp.where(row == 0, b0, jnp.where(row == 1, b1, pltpu.roll(u, 2, axis=0)))
    w = w_ref[...]
    y = w[0:1] * u2 + w[1:2] * u1 + w[2:3] * u
    o_ref[...] = (bg_ref[...] * y).astype(o_ref.dtype)
    nb_ref[...] = u[t - 2:t, :]


def conv_seq(z3, conv_w, buf, tc=256):
    bsz, t, _ = z3.shape
    nc = D_CONV // tc
    col = lambda off: pl.BlockSpec((None, t, tc), lambda b, j: (b, 0, off + j))
    return pl.pallas_call(
        _conv_body, grid=(bsz, nc),
        in_specs=[col(0), col(nc), col(2 * nc),
                  pl.BlockSpec((CONV_WIDTH, tc), lambda b, j: (0, j)),
                  pl.BlockSpec((None, 2, tc), lambda b, j: (b, 0, j))],
        out_specs=[pl.BlockSpec((None, t, tc), lambda b, j: (b, 0, j)),
                   pl.BlockSpec((None, 2, tc), lambda b, j: (b, 0, j))],
        out_shape=[jax.ShapeDtypeStruct((bsz, t, D_CONV), BF16), jax.ShapeDtypeStruct((bsz, 2, D_CONV), F32)],
        compiler_params=_cp(("parallel", "parallel")), name="conv_seq",
    )(z3, z3, z3, conv_w, buf)


def _conv_step_body(va_ref, bg_ref, cg_ref, w_ref, b0_ref, b1_ref, o_ref, u_ref):
    u = cg_ref[...] * va_ref[...]
    w = w_ref[...]
    y = w[0:1] * b0_ref[...] + w[1:2] * b1_ref[...] + w[2:3] * u
    o_ref[...] = (bg_ref[...] * y).astype(o_ref.dtype)
    u_ref[...] = u


def conv_step(z2, conv_w, buf):
    bsz = z2.shape[0]
    col = lambda j: pl.BlockSpec((bsz, D_CONV), lambda i: (0, j))
    full = pl.BlockSpec((bsz, D_CONV), lambda i: (0, 0))
    a, u = pl.pallas_call(
        _conv_step_body, grid=(1,),
        in_specs=[col(0), col(1), col(2), pl.BlockSpec((CONV_WIDTH, D_CONV), lambda i: (0, 0)), full, full],
        out_specs=[full, full],
        out_shape=[jax.ShapeDtypeStruct((bsz, D_CONV), BF16), jax.ShapeDtypeStruct((bsz, D_CONV), F32)],
        compiler_params=_cp(("arbitrary",)), name="conv_step",
    )(z2, z2, z2, conv_w, buf[:, 0], buf[:, 1])
    return a, jnp.stack([buf[:, 1], u], axis=1)


def _hgrn_body(q_ref, f_ref, i_ref, g_ref, lb_ref, nw_ref, s0_ref, o_ref, so_ref, st_ref,
               *, chunk, sub, nchunk, t_real, layer):
    tstep = pl.program_id(2)
    tb = chunk * nchunk

    @pl.when(tstep == 0)
    def _():
        st_ref[...] = s0_ref[...].T

    lbl = lb_ref[...]
    e = jnp.exp(lbl - jnp.max(lbl, axis=0, keepdims=True))
    sm = e / jnp.sum(e, axis=0, keepdims=True)
    lb = jnp.zeros((1, lbl.shape[1]), F32)
    for r in range(1, layer + 1):
        lb = lb + sm[r:r + 1]
    tri = jnp.where(_iota((chunk, chunk), 0) >= _iota((chunk, chunk), 1), 1.0, 0.0).astype(BF16)
    nw = nw_ref[...]
    neg = jnp.float32(-jnp.inf)

    def do_chunk(c, carry):
        r0 = pl.multiple_of(c * chunk, chunk)
        q = q_ref[pl.ds(r0, chunk), :]
        fr = f_ref[pl.ds(r0, chunk), :]
        v = i_ref[pl.ds(r0, chunk), :]
        g = g_ref[pl.ds(r0, chunk), :]
        logf = jnp.log(lb + (1.0 - lb) * _sigmoid(fr))
        kin = (1.0 - lb) * _sigmoid(-fr)
        if t_real is not None:
            live = (tstep * tb + r0 + _iota(fr.shape, 0)) < t_real
            logf = jnp.where(live, logf, 0.0)
            kin = jnp.where(live, kin, 0.0)
        qs = _silu(q)
        bb = _dot_exact_lhs(tri, logf)
        st = st_ref[...]
        o = _dot(qs * jnp.exp(bb), st, NT)
        parts = []
        for blk in range(chunk // sub):
            lo = blk * sub
            qi, bi, ki, vi = qs[lo:lo + sub], bb[lo:lo + sub], kin[lo:lo + sub], v[lo:lo + sub]
            d = bi[:, None, :] - bi[None, :, :]
            causal = _iota((sub, sub, 1), 1) <= _iota((sub, sub, 1), 0)
            sc = jnp.sum(qi[:, None, :] * ki[None, :, :] * jnp.exp(jnp.where(causal, d, neg)), axis=-1)
            oi = _dot(sc, vi)
            if blk > 0:
                anchor = bb[lo - 1:lo]
                qt = qi * jnp.exp(bi - anchor)
                kt = kin[:lo] * jnp.exp(anchor - bb[:lo])
                oi = oi + _dot(_dot(qt, kt, NT), v[:lo])
            parts.append(oi)
        o = o + (parts[0] if len(parts) == 1 else jnp.concatenate(parts, axis=0))
        b_last = bb[chunk - 1:chunk]
        st_ref[...] = st * jnp.exp(b_last) + _dot(v, kin * jnp.exp(b_last - bb), TN)
        o = o * lax.rsqrt(jnp.mean(o * o, axis=-1, keepdims=True) + RMS_EPS) * nw
        o_ref[pl.ds(r0, chunk), :] = (o * _silu(g)).astype(o_ref.dtype)
        return carry

    lax.fori_loop(0, nchunk, do_chunk, 0)

    @pl.when(tstep == pl.num_programs(2) - 1)
    def _():
        so_ref[...] = st_ref[...].T


def hgrn_seq(z3, hgrn_lb, norm_w, s0, layer, chunk, nchunk, t_real=None):
    bsz, t, _ = z3.shape
    tb = chunk * nchunk
    qoff = 3 * D_CONV // LANES
    col = lambda off: pl.BlockSpec((None, tb, LANES), lambda b, h, s: (b, s, off + h))
    st = pl.BlockSpec((None, None, HGRN_DK, HGRN_DV), lambda b, h, s: (b, h, 0, 0))
    body = functools.partial(_hgrn_body, chunk=chunk, sub=min(16, chunk), nchunk=nchunk, t_real=t_real, layer=layer)
    return pl.pallas_call(
        body, grid=(bsz, HGRN_HEADS, t // tb),
        in_specs=[col(qoff), col(qoff + HGRN_HEADS), col(qoff + 2 * HGRN_HEADS), col(qoff + 3 * HGRN_HEADS),
                  pl.BlockSpec((N_EVEN, LANES), lambda b, h, s: (0, h)),
                  pl.BlockSpec((1, HGRN_DV), lambda b, h, s: (0, 0)), st],
        out_specs=[pl.BlockSpec((None, tb, LANES), lambda b, h, s: (b, s, h)), st],
        out_shape=[jax.ShapeDtypeStruct((bsz, t, D_HV), BF16),
                   jax.ShapeDtypeStruct((bsz, HGRN_HEADS, HGRN_DK, HGRN_DV), F32)],
        scratch_shapes=[pltpu.VMEM((HGRN_DV, HGRN_DK), F32)],
        compiler_params=_cp(("parallel", "parallel", "arbitrary")), name="hgrn_seq",
    )(z3, z3, z3, z3, hgrn_lb, norm_w.reshape(1, HGRN_DV), s0)


def _attn_body(sink_ref, zq_ref, zkv_ref, cos_ref, sin_ref, ck_ref, cv_ref, o_ref, nk_ref, nv_ref, kp_ref, vp_ref,
               *, prev_valid, t_real):
    i = pl.program_id(1)
    w = WINDOW

    @pl.when(i == 0)
    def _():
        kp_ref[...] = ck_ref[...]
        vp_ref[...] = cv_ref[...]

    cos = cos_ref[...]
    sin = sin_ref[...]

    def rope(x):
        width = x.shape[1]
        reps = width // LANES
        first = (_iota(x.shape, 1) & (HEAD_DIM - 1)) < (HEAD_DIM // 2)
        rot = jnp.where(first, pltpu.roll(x, width - HEAD_DIM // 2, axis=1), pltpu.roll(x, HEAD_DIM // 2, axis=1))
        return x * jnp.tile(cos, (1, reps)) + rot * jnp.tile(sin, (1, reps))

    kv = zkv_ref[...]
    qr = rope(zq_ref[...]) * (HEAD_DIM ** -0.5)
    kr = rope(kv[:, :D_KV])
    v = kv[:, D_KV:]
    kprev = kp_ref[...]
    vprev = vp_ref[...]

    grp = ATTN_HEADS // ATTN_KV_HEADS
    rows = _iota((grp * w, 2 * w), 0)
    cols = _iota((grp * w, 2 * w), 1)
    delta = (rows & (w - 1)) + w - cols
    valid = (delta >= 0) & (delta <= w)
    if not prev_valid:
        valid = valid & (cols >= jnp.where(i > 0, 0, w))
    head_of_row = _iota((grp * w, 1), 0) >> int(math.log2(w))
    neg = jnp.float32(-jnp.inf)
    outs = []
    for g in range(ATTN_KV_HEADS):
        ls = slice(g * HEAD_DIM, (g + 1) * HEAD_DIM)
        kg = jnp.concatenate([kprev[:, ls], kr[:, ls]], axis=0)
        vg = jnp.concatenate([vprev[:, ls], v[:, ls]], axis=0)
        qg = jnp.concatenate([qr[:, (grp * g + hh) * HEAD_DIM:(grp * g + hh + 1) * HEAD_DIM] for hh in range(grp)],
                             axis=0)
        s = jnp.where(valid, _dot(qg, kg, NT), neg)
        sink = jnp.zeros((grp * w, 1), F32)
        for hh in range(grp):
            sink = jnp.where(head_of_row == hh, sink_ref[grp * g + hh], sink)
        m = jnp.maximum(jnp.max(s, axis=-1, keepdims=True), sink)
        p = jnp.exp(s - m)
        p = p / (jnp.sum(p, axis=-1, keepdims=True) + jnp.exp(sink - m))
        og = _dot(p, vg)
        outs += [og[hh * w:(hh + 1) * w] for hh in range(grp)]
    o_ref[...] = jnp.concatenate(outs, axis=1).astype(o_ref.dtype)
    kp_ref[...] = kr
    vp_ref[...] = v

    @pl.when(i == pl.num_programs(1) - 1)
    def _():
        if t_real == w:
            nk_ref[...] = kr
            nv_ref[...] = v
        else:
            last = _iota(kr.shape, 0) == w - 1
            nk_ref[...] = jnp.where(last, kr[0:1], pltpu.roll(kprev, w - 1, axis=0))
            nv_ref[...] = jnp.where(last, v[0:1], pltpu.roll(vprev, w - 1, axis=0))


def attn_seq(z3, sinks, cos, sin, cache_k, cache_v, prev_valid, t_real):
    bsz, t, _ = z3.shape
    nb = t // WINDOW
    cache = pl.BlockSpec((None, WINDOW, D_KV), lambda b, i: (b, 0, 0))
    tab = pl.BlockSpec((WINDOW, LANES), lambda b, i: (i, 0))
    body = functools.partial(_attn_body, prev_valid=prev_valid, t_real=t_real)
    return pl.pallas_call(
        body, grid=(bsz, nb),
        in_specs=[pl.BlockSpec(memory_space=pltpu.SMEM),
                  pl.BlockSpec((None, WINDOW, D_Q), lambda b, i: (b, i, 0)),
                  pl.BlockSpec((None, WINDOW, 2 * D_KV), lambda b, i: (b, i, D_Q // (2 * D_KV))),
                  tab, tab, cache, cache],
        out_specs=[pl.BlockSpec((None, WINDOW, D_Q), lambda b, i: (b, i, 0)), cache, cache],
        out_shape=[jax.ShapeDtypeStruct((bsz, t, D_Q), BF16),
                   jax.ShapeDtypeStruct((bsz, WINDOW, D_KV), F32), jax.ShapeDtypeStruct((bsz, WINDOW, D_KV), F32)],
        scratch_shapes=[pltpu.VMEM((WINDOW, D_KV), F32), pltpu.VMEM((WINDOW, D_KV), F32)],
        compiler_params=_cp(("parallel", "arbitrary")), name="attn_seq",
    )(sinks, z3, z3, cos, sin, cache_k, cache_v)


def rope_tables(pos):
    half = HEAD_DIM // 2
    inv = jnp.exp(-math.log(ROPE_THETA) * jnp.arange(half, dtype=F32) / half)
    ang = pos.astype(F32)[:, None] * inv[None, :]
    c, s = jnp.cos(ang), jnp.sin(ang)
    cos = jnp.concatenate([c, c, c, c], axis=1)
    sin = jnp.concatenate([-s, s, -s, s], axis=1)
    return cos, sin


def _rwkv_body(r_ref, k_ref, v_ref, wa_ref, gd_ref, sr_ref, sk_ref, sv_ref, swa_ref, sgd_ref,
               mr_ref, mk_ref, mv_ref, mwa_ref, mgd_ref, w0_ref, w2_ref, a0_ref, a2_ref, g2_ref,
               kkp_ref, ka_ref, rk_ref, lg_ref, lbias_ref, s0_ref, o_ref, so_ref, st_ref, prev_ref,
               *, chunk, t_real):
    tstep = pl.program_id(2)
    n = RWKV_N
    ln = chunk

    @pl.when(tstep == 0)
    def _():
        st_ref[...] = s0_ref[...]
        prev_ref[0:1, :] = sr_ref[...]
        prev_ref[1:2, :] = sk_ref[...]
        prev_ref[2:3, :] = sv_ref[...]
        prev_ref[3:4, :] = swa_ref[...]
        prev_ref[4:5, :] = sgd_ref[...]

    row = _iota((ln, LANES), 0)

    def mix(x_ref, mu_ref, idx):
        x = x_ref[...]
        shifted = jnp.where(row == 0, prev_ref[idx:idx + 1, :], pltpu.roll(x, 1, axis=0))
        prev_ref[idx:idx + 1, :] = x[ln - 1:ln, :]
        return x + mu_ref[...] * (shifted - x)

    r = mix(r_ref, mr_ref, 0)
    kr = mix(k_ref, mk_ref, 1)
    vr = mix(v_ref, mv_ref, 2)
    wa = mix(wa_ref, mwa_ref, 3)
    gd = mix(gd_ref, mgd_ref, 4)[:, :RWKV_RANK]
    wd = wa[:, :RWKV_RANK]
    ad = wa[:, RWKV_RANK:]

    w_log = -_softplus(-(w0_ref[...] + _dot3(jnp.tanh(wd), w2_ref[...]))) - 0.5
    logw = -jnp.exp(w_log)
    a = _sigmoid(a0_ref[...] + _dot3(ad, a2_ref[...]))
    gate = _dot3(_sigmoid(gd), g2_ref[...])

    sh = int(math.log2(n))
    seg = jnp.where((_iota((LANES, LANES), 0) >> sh) == (_iota((LANES, LANES), 1) >> sh), 1.0, 0.0).astype(BF16)
    segsum = lambda x: _dot_exact_rhs(x, seg)

    kkv = kr * kkp_ref[...]
    kk = kkv / jnp.maximum(jnp.sqrt(segsum(kkv * kkv)), 1e-12)
    kf = kr * (1.0 + (a - 1.0) * ka_ref[...])
    bonus = segsum(r * kf * rk_ref[...]) * vr
    if t_real is not None:
        live = (tstep * ln + row) < t_real
        zero = jnp.zeros_like(logw)
        logw = jnp.where(live, logw, zero)
        kk = jnp.where(live, kk, zero)
        kf = jnp.where(live, kf, zero)
        vr_s = jnp.where(live, vr, zero)
    else:
        vr_s = vr

    tri = jnp.where(_iota((ln, ln), 0) >= _iota((ln, ln), 1), 1.0, 0.0).astype(BF16)
    c = _dot_exact_lhs(tri, logw)
    gam = jnp.exp(c)
    am = -kk * jnp.exp(c - logw)
    rm = r * gam
    ginv = jnp.exp(-c)
    bp = kk * a * ginv
    kp = kf * ginv
    g_last = gam[ln - 1:ln, :]
    bp_l = bp * g_last
    kp_l = kp * g_last

    rr = _iota((ln, ln), 0)
    cc = _iota((ln, ln), 1)
    strict = rr > cc
    incl = rr >= cc
    eye = jnp.where(rr == cc, 1.0, 0.0)
    outs = []
    for hh in range(2):
        ls = slice(hh * n, (hh + 1) * n)
        s = st_ref[hh]
        vh = vr_s[:, ls]
        pw = _dot3(jnp.concatenate([am[:, ls], rm[:, ls]], axis=0),
                   jnp.concatenate([bp[:, ls], kp[:, ls]], axis=0), NT)
        m = jnp.where(strict, pw[:ln, :ln], 0.0)
        nm = jnp.where(strict, pw[:ln, ln:], 0.0)
        qb = jnp.where(incl, pw[ln:, :ln], 0.0)
        qk = jnp.where(incl, pw[ln:, ln:], 0.0)
        tinv = eye + m
        p = m
        for _ in range(int(math.log2(ln)) - 1):
            p = _dot3(p, p)
            tinv = tinv + _dot3(tinv, p)
        u = _dot3(_dot3(tinv, am[:, ls]), s, NT) + _dot3(tinv, _dot3(nm, vh))
        o = _dot3(rm[:, ls], s, NT) + _dot3(qb, u) + _dot3(qk, vh)
        st_ref[hh] = s * g_last[:, ls] + _dot3(u, bp_l[:, ls], TN) + _dot3(vh, kp_l[:, ls], TN)
        outs.append(o)
    o = jnp.concatenate(outs, axis=1)
    mu_o = segsum(o) * (1.0 / n)
    dlt = o - mu_o
    var_o = segsum(dlt * dlt) * (1.0 / n)
    o = dlt * lax.rsqrt(var_o + RWKV_GN_EPS) * lg_ref[...] + lbias_ref[...]
    o_ref[...] = ((o + bonus) * gate).astype(o_ref.dtype)

    @pl.when(tstep == pl.num_programs(2) - 1)
    def _():
        so_ref[...] = st_ref[...]


def rwkv_seq(z3, shift, P, j, s0, chunk, t_real=None):
    bsz, t, _ = z3.shape
    pairs = RWKV_HEADS // 2
    zoff = (D_Q + 2 * D_KV) // LANES
    nb = D_RWKV // LANES
    wa_blk = 3 * nb
    gd_blk = 3 * nb + 1
    zc = lambda off, per_pair=True: pl.BlockSpec(
        (None, chunk, LANES), lambda b, p, s: (b, s, zoff + off + (p if per_pair else 0)))
    sc = lambda off, per_pair=True: pl.BlockSpec(
        (None, 1, LANES), lambda b, p, s: (b, 0, off + (p if per_pair else 0)))
    mc = lambda off, per_pair=True: pl.BlockSpec((1, LANES), lambda b, p, s: (0, off + (p if per_pair else 0)))
    vec = pl.BlockSpec((1, LANES), lambda b, p, s: (0, p))
    lora = pl.BlockSpec((RWKV_RANK, LANES), lambda b, p, s: (0, p))
    st = pl.BlockSpec((None, 2, RWKV_N, RWKV_N), lambda b, p, s: (b, p, 0, 0))
    mu = jnp.pad(P['rwkv_mu'][j], (0, D_SHIFT_PAD - D_SHIFT)).reshape(1, D_SHIFT_PAD)
    row = lambda x: x.reshape(1, D_RWKV)
    body = functools.partial(_rwkv_body, chunk=chunk, t_real=t_real)
    return pl.pallas_call(
        body, grid=(bsz, pairs, t // chunk),
        in_specs=[zc(0), zc(nb), zc(2 * nb), zc(wa_blk, False), zc(gd_blk, False),
                  sc(0), sc(nb), sc(2 * nb), sc(wa_blk, False), sc(gd_blk, False),
                  mc(0), mc(nb), mc(2 * nb), mc(wa_blk, False), mc(gd_blk, False),
                  vec, lora, vec, lora, lora, vec, vec, vec, vec, vec, st],
        out_specs=[pl.BlockSpec((None, chunk, LANES), lambda b, p, s: (b, s, p)), st],
        out_shape=[jax.ShapeDtypeStruct((bsz, t, D_RWKV), BF16),
                   jax.ShapeDtypeStruct((bsz, RWKV_HEADS, RWKV_N, RWKV_N), F32)],
        scratch_shapes=[pltpu.VMEM((2, RWKV_N, RWKV_N), F32), pltpu.VMEM((8, LANES), F32)],
        compiler_params=_cp(("parallel", "parallel", "arbitrary")), name="rwkv_seq",
    )(z3, z3, z3, z3, z3, shift, shift, shift, shift, shift, mu, mu, mu, mu, mu,
      row(P['rwkv_w0'][j]), P['rwkv_w2'][j], row(P['rwkv_a0'][j]), P['rwkv_a2'][j], P['rwkv_g2'][j],
      row(P['rwkv_kk'][j]), row(P['rwkv_ka'][j]), row(P['rwkv_rk'][j]), row(P['rwkv_lnx_g'][j]),
      row(P['rwkv_lnx_b'][j]), s0)


def _pad_time(z2, tp):
    return jnp.pad(z2[:, None, :], ((0, 0), (0, tp - 1), (0, 0)))


def _trunk(x, grp, pos0, W, P, conv_st, hgrn_st, k_cache, v_cache, rwkv_st, shift_st, single):
    n, d = x.shape
    bsz = n // grp.t
    t = grp.t
    tm = grp.tm
    step_pad = 8
    conv_new, hgrn_new, k_new, v_new, rwkv_new, shift_new = [], [], [], [], [], []
    h = modulate(x, grp, 0, 1, 0)
    for l in range(DEPTH):
        j = l // 2
        if l % 2 == 0:
            z = matmul(h, W['w_in_even'][j], tm, D_IN_EVEN // 4)
            if single:
                a_out, cb = conv_step(z, P['conv_w'][j], conv_st[j])
                b3, hs = hgrn_seq(_pad_time(z, step_pad), P['hgrn_lb'], P['hgrn_norm'][j], hgrn_st[j], j,
                                  chunk=step_pad, nchunk=1, t_real=1)
                b_out = b3[:, 0]
            else:
                z3 = z.reshape(bsz, t, D_IN_EVEN)
                a3, cb = conv_seq(z3, P['conv_w'][j], conv_st[j])
                b3, hs = hgrn_seq(z3, P['hgrn_lb'], P['hgrn_norm'][j], hgrn_st[j], j, chunk=64, nchunk=8)
                a_out, b_out = a3.reshape(n, D_CONV), b3.reshape(n, D_HV)
            conv_new.append(cb)
            hgrn_new.append(hs)
            wa, wb = W['w_out_even'][j][:D_CONV], W['w_out_even'][j][D_CONV:]
        else:
            z = matmul(h, W['w_in_odd'][j], tm, D_IN_ODD_PAD // 2)
            shift_in = jnp.pad(shift_st[j], ((0, 0), (0, D_SHIFT_PAD - D_SHIFT)))[:, None, :]
            kc = k_cache[j].reshape(bsz, WINDOW, D_KV)
            vc = v_cache[j].reshape(bsz, WINDOW, D_KV)
            if single:
                cos, sin = rope_tables(pos0 + jnp.arange(WINDOW, dtype=I32))
                a3, nk, nv = attn_seq(_pad_time(z[:, :D_Q + 2 * D_KV], WINDOW), P['attn_sinks'][j], cos, sin,
                                      kc, vc, prev_valid=True, t_real=1)
                b3, rs = rwkv_seq(_pad_time(z, step_pad), shift_in, P, j, rwkv_st[j], chunk=step_pad, t_real=1)
                a_out, b_out = a3[:, 0], b3[:, 0]
                shift_new.append(z[:, D_Q + 2 * D_KV:D_IN_ODD])
            else:
                z3 = z.reshape(bsz, t, D_IN_ODD_PAD)
                cos, sin = rope_tables(pos0 + jnp.arange(t, dtype=I32))
                a3, nk, nv = attn_seq(z3, P['attn_sinks'][j], cos, sin, kc, vc, prev_valid=False, t_real=WINDOW)
                b3, rs = rwkv_seq(z3, shift_in, P, j, rwkv_st[j], chunk=64)
                a_out, b_out = a3.reshape(n, D_Q), b3.reshape(n, D_RWKV)
                shift_new.append(z3[:, t - 1, D_Q + 2 * D_KV:D_IN_ODD])
            k_new.append(nk.reshape(bsz, WINDOW, ATTN_KV_HEADS, HEAD_DIM))
            v_new.append(nv.reshape(bsz, WINDOW, ATTN_KV_HEADS, HEAD_DIM))
            rwkv_new.append(rs)
            wa, wb = W['w_out_odd'][j][:D_Q], W['w_out_odd'][j][D_Q:]
        x, h2 = out_proj_ln(a_out, b_out, wa, wb, x, grp, l, P['ln_g'][l, 0], P['ln_b'][l, 0])
        eid, gate, rank, cnt = router(h2, W['router_w'][l], W['router_b'][l], min(tm, 256))
        tok, dst, be, n_used, n_blocks = moe_plan(eid, rank, cnt, n)
        ys = moe_ffn(h2, tok, dst, be, n_used, n_blocks, P['moe_w1'][l], P['moe_w3'][l], P['moe_w2'][l])
        x, h = moe_combine_ln(ys, gate, x, grp, l, P['ln_g'][l, 1], P['ln_b'][l, 1], has_next=l + 1 < DEPTH)
    return (x, jnp.stack(conv_new), jnp.stack(hgrn_new), jnp.stack(k_new), jnp.stack(v_new),
            jnp.stack(rwkv_new), jnp.stack(shift_new))


def kernel(x_prompt, x_sample, c_prompt, c_sample, state_conv, state_hgrn, cache_swa_k, cache_swa_v,
           state_rwkv, state_shift, ada_w, ada_b, ln_g, ln_b, w_in_even, w_out_even, conv_w, hgrn_lb,
           hgrn_norm, w_in_odd, w_out_odd, attn_sinks, rwkv_mu, rwkv_w0, rwkv_w2, rwkv_a0, rwkv_a2,
           rwkv_g2, rwkv_kk, rwkv_ka, rwkv_rk, rwkv_lnx_g, rwkv_lnx_b, moe_w_grp, moe_b_grp, moe_w_exp,
           moe_b_exp, moe_w1, moe_w3, moe_w2):
    P = dict(ln_g=ln_g, ln_b=ln_b, conv_w=conv_w, hgrn_lb=hgrn_lb, hgrn_norm=hgrn_norm, attn_sinks=attn_sinks,
             rwkv_mu=rwkv_mu, rwkv_w0=rwkv_w0, rwkv_w2=rwkv_w2, rwkv_a0=rwkv_a0, rwkv_a2=rwkv_a2,
             rwkv_g2=rwkv_g2, rwkv_kk=rwkv_kk, rwkv_ka=rwkv_ka, rwkv_rk=rwkv_rk.reshape(N_ODD, D_RWKV),
             rwkv_lnx_g=rwkv_lnx_g, rwkv_lnx_b=rwkv_lnx_b, moe_w1=moe_w1, moe_w3=moe_w3, moe_w2=moe_w2)
    bp, tp, d = x_prompt.shape
    bs, ts, _ = x_sample.shape
    router_w = jnp.zeros((DEPTH, d, 2 * LANES), F32)
    router_w = router_w.at[:, :, :N_GROUPS].set(moe_w_grp).at[:, :, LANES:LANES + N_EXPERTS].set(moe_w_exp)
    router_b = jnp.zeros((DEPTH, 1, 2 * LANES), F32)
    router_b = router_b.at[:, 0, :N_GROUPS].set(moe_b_grp).at[:, 0, LANES:LANES + N_EXPERTS].set(moe_b_exp)
    W = dict(w_in_even=w_in_even.astype(BF16), w_out_even=w_out_even.astype(BF16),
             w_in_odd=jnp.pad(w_in_odd.astype(BF16), ((0, 0), (0, 0), (0, D_IN_ODD_PAD - D_IN_ODD))),
             w_out_odd=w_out_odd.astype(BF16), router_w=router_w, router_b=router_b)

    mod = ada_mod(jnp.concatenate([c_prompt, c_sample], axis=0), ada_w, ada_b)
    mod = mod.reshape(DEPTH, bp + bs, 6, d).transpose(0, 2, 1, 3)
    grp_p = _Group(bp * tp, tp, 512, mod[:, :, :bp, None, :], per_row=False)
    grp_s = _Group(bs * ts, ts, bs * ts, mod[:, :, bp:], per_row=True)

    zeros = lambda *s: jnp.zeros(s, F32)
    outs_p = _trunk(x_prompt.reshape(bp * tp, d), grp_p, 0, W, P,
                    zeros(N_EVEN, bp, CONV_WIDTH - 1, D_CONV), zeros(N_EVEN, bp, HGRN_HEADS, HGRN_DK, HGRN_DV),
                    zeros(N_ODD, bp, WINDOW, ATTN_KV_HEADS, HEAD_DIM), zeros(N_ODD, bp, WINDOW, ATTN_KV_HEADS, HEAD_DIM),
                    zeros(N_ODD, bp, RWKV_HEADS, RWKV_N, RWKV_N), zeros(N_ODD, bp, D_SHIFT), single=False)
    outs_s = _trunk(x_sample.reshape(bs * ts, d), grp_s, PAST_LEN, W, P,
                    state_conv, state_hgrn, cache_swa_k, cache_swa_v, state_rwkv, state_shift, single=True)
    y_p = outs_p[0].reshape(bp, tp, d)
    y_s = outs_s[0].reshape(bs, ts, d)
    return (y_p, y_s) + tuple(outs_p[1:]) + tuple(outs_s[1:])
```

```python
import functools
import math

import jax
import jax.numpy as jnp
from jax import lax
from jax.experimental import pallas as pl
from jax.experimental.pallas import tpu as pltpu

F32 = jnp.float32
BF16 = jnp.bfloat16
I32 = jnp.int32

D_MODEL = 2048
DEPTH = 4
PAST_LEN = 16384
N_EVEN = (DEPTH + 1) // 2
N_ODD = DEPTH // 2
D_CONV = 1024
CONV_WIDTH = 3
HGRN_HEADS = 8
HGRN_DK = 128
HGRN_DV = 128
D_HK = HGRN_HEADS * HGRN_DK
D_HV = HGRN_HEADS * HGRN_DV
ATTN_HEADS = 16
ATTN_KV_HEADS = 4
HEAD_DIM = 64
WINDOW = 128
ROPE_THETA = 10000.0
D_Q = ATTN_HEADS * HEAD_DIM
D_KV = ATTN_KV_HEADS * HEAD_DIM
RWKV_HEADS = 16
RWKV_N = 64
D_RWKV = RWKV_HEADS * RWKV_N
RWKV_RANK = 64
RWKV_GN_EPS = 64e-5
D_SHIFT = 3 * D_RWKV + 3 * RWKV_RANK
D_IN_EVEN = 3 * D_CONV + 2 * D_HK + 2 * D_HV
D_IN_ODD = D_Q + 2 * D_KV + D_SHIFT
N_GROUPS = 4
EXPERTS_PER_GROUP = 8
N_EXPERTS = N_GROUPS * EXPERTS_PER_GROUP
D_EXPERT = 512
ALPHA = (2 * DEPTH) ** 0.25
LN_EPS = 1e-5
RMS_EPS = 1e-6

LANES = 128
MOE_ROWS = 128
VMEM_LIMIT = 48 * 1024 * 1024

D_IN_ODD_PAD = -(-D_IN_ODD // LANES) * LANES
D_SHIFT_PAD = D_IN_ODD_PAD - (D_Q + 2 * D_KV)

NN = (((1,), (0,)), ((), ()))
NT = (((1,), (1,)), ((), ()))
TN = (((0,), (0,)), ((), ()))


def _cp(sem, vmem=VMEM_LIMIT):
    return pltpu.CompilerParams(dimension_semantics=sem, vmem_limit_bytes=vmem)


def _dot(a, b, dims=NN):
    return lax.dot_general(a.astype(BF16), b.astype(BF16), dims, preferred_element_type=F32)


def _split2(x):
    hi = x.astype(BF16)
    lo = (x - hi.astype(F32)).astype(BF16)
    return hi, lo


def _dot3(a, b, dims=NN):
    ah, al = _split2(a)
    bh, bl = _split2(b)
    d = lambda x, y: lax.dot_general(x, y, dims, preferred_element_type=F32)
    return d(ah, bh) + (d(ah, bl) + d(al, bh))


def _dot_exact_lhs(a_bf16, b, dims=NN):
    b1 = b.astype(BF16)
    r1 = b - b1.astype(F32)
    b2 = r1.astype(BF16)
    b3 = (r1 - b2.astype(F32)).astype(BF16)
    d = lambda y: lax.dot_general(a_bf16, y, dims, preferred_element_type=F32)
    return d(b1) + (d(b2) + d(b3))


def _dot_exact_rhs(a, b_bf16, dims=NN):
    a1 = a.astype(BF16)
    r1 = a - a1.astype(F32)
    a2 = r1.astype(BF16)
    d = lambda x: lax.dot_general(x, b_bf16, dims, preferred_element_type=F32)
    return d(a1) + d(a2)


def _sigmoid(x):
    return 1.0 / (1.0 + jnp.exp(-x))


def _silu(x):
    return x * _sigmoid(x)


def _softplus(x):
    return jnp.maximum(x, 0.0) + jnp.log(1.0 + jnp.exp(-jnp.abs(x)))


def _iota(shape, axis):
    return lax.broadcasted_iota(I32, shape, axis)


def _layer_norm(u, g, b):
    mu = jnp.mean(u, axis=-1, keepdims=True)
    d = u - mu
    var = jnp.mean(d * d, axis=-1, keepdims=True)
    return d * lax.rsqrt(var + LN_EPS) * g + b


def _ada_body(c_ref, w_ref, b_ref, o_ref):
    o_ref[...] = _dot(_silu(c_ref[...]), w_ref[...]) + b_ref[...]


def ada_mod(c_all, ada_w, ada_b, tn=1024):
    nl, d, n = ada_w.shape
    r = c_all.shape[0]
    return pl.pallas_call(
        _ada_body,
        grid=(nl, n // tn),
        in_specs=[pl.BlockSpec((r, d), lambda l, j: (0, 0)),
                  pl.BlockSpec((None, d, tn), lambda l, j: (l, 0, j)),
                  pl.BlockSpec((None, 1, tn), lambda l, j: (l, 0, j))],
        out_specs=pl.BlockSpec((None, r, tn), lambda l, j: (l, 0, j)),
        out_shape=jax.ShapeDtypeStruct((nl, r, n), F32),
        compiler_params=_cp(("parallel", "parallel")),
        name="ada_mod",
    )(c_all, ada_w, ada_b.reshape(nl, 1, n))


class _Group:
    def __init__(self, n, t, tm, modarr, per_row):
        self.n, self.t, self.tm, self.modarr, self.per_row = n, t, tm, modarr, per_row

    def mod(self, l, c):
        d = self.modarr.shape[-1]
        if self.per_row:
            return pl.BlockSpec((None, None, self.tm, d), lambda i: (l, c, i, 0))
        t, tm = self.t, self.tm
        return pl.BlockSpec((None, None, None, 1, d), lambda i: (l, c, (i * tm) // t, 0, 0))


def _modulate_body(x_ref, sc_ref, sh_ref, h_ref):
    h_ref[...] = (x_ref[...] * (1.0 + sc_ref[...]) + sh_ref[...]).astype(h_ref.dtype)


def modulate(x, grp, l, c_sc, c_sh):
    n, d = x.shape
    tm = grp.tm
    row = pl.BlockSpec((tm, d), lambda i: (i, 0))
    return pl.pallas_call(
        _modulate_body, grid=(n // tm,),
        in_specs=[row, grp.mod(l, c_sc), grp.mod(l, c_sh)],
        out_specs=row, out_shape=jax.ShapeDtypeStruct((n, d), BF16),
        compiler_params=_cp(("parallel",)), name="modulate",
    )(x, grp.modarr, grp.modarr)


def _mm_body(a_ref, w_ref, o_ref):
    o_ref[...] = jnp.dot(a_ref[...], w_ref[...], preferred_element_type=F32)


def matmul(a, w, tm, tn):
    n, k = a.shape
    nn = w.shape[1]
    return pl.pallas_call(
        _mm_body, grid=(n // tm, nn // tn),
        in_specs=[pl.BlockSpec((tm, k), lambda i, j: (i, 0)),
                  pl.BlockSpec((k, tn), lambda i, j: (0, j))],
        out_specs=pl.BlockSpec((tm, tn), lambda i, j: (i, j)),
        out_shape=jax.ShapeDtypeStruct((n, nn), F32),
        compiler_params=_cp(("parallel", "parallel")), name="in_proj",
    )(a, w)


def _outln_body(a_ref, b_ref, wa_ref, wb_ref, x_ref, gt_ref, g_ref, be_ref, sc_ref, sh_ref, *rest):
    xo_ref, ho_ref = rest[-2:]
    y = (jnp.dot(a_ref[...], wa_ref[...], preferred_element_type=F32)
         + jnp.dot(b_ref[...], wb_ref[...], preferred_element_type=F32))
    xn = _layer_norm(ALPHA * x_ref[...] + (1.0 + gt_ref[...]) * y, g_ref[...], be_ref[...])
    xo_ref[...] = xn
    ho_ref[...] = xn * (1.0 + sc_ref[...]) + sh_ref[...]


def out_proj_ln(a, b, wa, wb, x, grp, l, ln_g, ln_b, h_rows, h_row0, h_all=None):
    n, d = x.shape
    ka, kb = a.shape[1], b.shape[1]
    tm = min(grp.tm, 256)
    g2 = _Group(grp.n, grp.t, tm, grp.modarr, grp.per_row)
    row = pl.BlockSpec((tm, d), lambda i: (i, 0))
    vec = pl.BlockSpec((1, d), lambda i: (0, 0))
    in_specs = [pl.BlockSpec((tm, ka), lambda i: (i, 0)), pl.BlockSpec((tm, kb), lambda i: (i, 0)),
                pl.BlockSpec((ka, d), lambda i: (0, 0)), pl.BlockSpec((kb, d), lambda i: (0, 0)),
                row, g2.mod(l, 2), vec, vec, g2.mod(l, 4), g2.mod(l, 3)]
    args = [a, b, wa, wb, x, grp.modarr, ln_g.reshape(1, d), ln_b.reshape(1, d), grp.modarr, grp.modarr]
    aliases = {}
    if h_all is not None:
        in_specs.append(pl.BlockSpec(memory_space=pl.ANY))
        args.append(h_all)
        aliases = {len(args) - 1: 1}
    return pl.pallas_call(
        _outln_body, grid=(n // tm,),
        in_specs=in_specs,
        out_specs=[row, pl.BlockSpec((tm, d), lambda i: (i + h_row0 // tm, 0))],
        out_shape=[jax.ShapeDtypeStruct((n, d), F32), jax.ShapeDtypeStruct((h_rows, d), F32)],
        input_output_aliases=aliases,
        compiler_params=_cp(("parallel",)), name="out_proj_ln",
    )(*args)


def _comb_body(*refs, has_next):
    if has_next:
        y0_ref, y1_ref, gate_ref, x_ref, gt_ref, g_ref, be_ref, sc_ref, sh_ref, xo_ref, ho_ref = refs
    else:
        y0_ref, y1_ref, gate_ref, x_ref, gt_ref, g_ref, be_ref, xo_ref = refs
    gate = gate_ref[...]
    y = y0_ref[...] * gate[:, 0:1] + y1_ref[...] * gate[:, 1:2]
    xn = _layer_norm(ALPHA * x_ref[...] + (1.0 + gt_ref[...]) * y, g_ref[...], be_ref[...])
    xo_ref[...] = xn
    if has_next:
        ho_ref[...] = (xn * (1.0 + sc_ref[...]) + sh_ref[...]).astype(ho_ref.dtype)


def moe_combine_ln(ys, row0, row1, gate, x, grp, l, ln_g, ln_b, has_next):
    n, d = x.shape
    tm = min(grp.tm, 256)
    g2 = _Group(grp.n, grp.t, tm, grp.modarr, grp.per_row)
    nt = n // tm
    row = pl.BlockSpec((tm, d), lambda i: (i, 0))
    vec = pl.BlockSpec((1, d), lambda i: (0, 0))
    in_specs = [pl.BlockSpec((tm, d), lambda i: (i + row0 // tm, 0)),
                pl.BlockSpec((tm, d), lambda i: (i + row1 // tm, 0)), pl.BlockSpec((tm, 2), lambda i: (i, 0)),
                row, g2.mod(l, 5), vec, vec]
    args = [ys, ys, gate, x, grp.modarr, ln_g.reshape(1, d), ln_b.reshape(1, d)]
    out_specs = [row]
    out_shape = [jax.ShapeDtypeStruct((n, d), F32)]
    if has_next:
        in_specs += [g2.mod(l + 1, 1), g2.mod(l + 1, 0)]
        args += [grp.modarr, grp.modarr]
        out_specs.append(row)
        out_shape.append(jax.ShapeDtypeStruct((n, d), BF16))
    res = pl.pallas_call(
        functools.partial(_comb_body, has_next=has_next), grid=(nt,),
        in_specs=in_specs, out_specs=out_specs, out_shape=out_shape,
        compiler_params=_cp(("parallel",)), name="moe_combine_ln",
    )(*args)
    return (res[0], res[1]) if has_next else (res[0], None)


def _router_body(h_ref, w_ref, b_ref, cnt0_ref, eid_ref, gate_ref, rank_ref, cnt_ref, run_ref):
    i = pl.program_id(0)

    @pl.when(i == 0)
    def _():
        run_ref[...] = cnt0_ref[...]

    tm = h_ref.shape[0]
    logits = _dot3(h_ref[...], w_ref[...]) + b_ref[...]
    lg = logits[:, :LANES]
    le = logits[:, LANES:]
    lane = _iota((tm, LANES), 1).astype(F32)
    neg = jnp.float32(-jnp.inf)
    lgm = jnp.where(lane < N_GROUPS, lg, neg)
    mg = jnp.max(lgm, axis=-1, keepdims=True)
    gidx = jnp.min(jnp.where(lgm == mg, lane, float(LANES)), axis=-1, keepdims=True)
    p_g = 1.0 / jnp.sum(jnp.exp(lgm - mg), axis=-1, keepdims=True)
    lo = gidx * EXPERTS_PER_GROUP
    in_grp = (lane >= lo) & (lane < lo + EXPERTS_PER_GROUP)
    lem = jnp.where(in_grp, le, neg)
    v1 = jnp.max(lem, axis=-1, keepdims=True)
    i1 = jnp.min(jnp.where(lem == v1, lane, float(LANES)), axis=-1, keepdims=True)
    lem2 = jnp.where(lane == i1, neg, lem)
    v2 = jnp.max(lem2, axis=-1, keepdims=True)
    i2 = jnp.min(jnp.where(lem2 == v2, lane, float(LANES)), axis=-1, keepdims=True)
    e2 = jnp.exp(v2 - v1)
    g1 = p_g / (1.0 + e2)
    g2 = p_g * e2 / (1.0 + e2)
    oh1 = jnp.where(lane == i1, 1.0, 0.0)
    oh2 = jnp.where(lane == i2, 1.0, 0.0)
    comb = oh1 + oh2
    tri = jnp.where(_iota((tm, tm), 0) > _iota((tm, tm), 1), 1.0, 0.0).astype(BF16)
    before = jnp.dot(tri, comb.astype(BF16), preferred_element_type=F32) + run_ref[...]
    r1 = jnp.sum(before * oh1, axis=-1, keepdims=True)
    r2 = jnp.sum(before * oh2, axis=-1, keepdims=True)
    run_ref[...] = run_ref[...] + jnp.sum(comb, axis=0, keepdims=True)
    eid_ref[:, 0:1] = i1.astype(I32)
    eid_ref[:, 1:2] = i2.astype(I32)
    gate_ref[:, 0:1] = g1
    gate_ref[:, 1:2] = g2
    rank_ref[:, 0:1] = r1.astype(I32)
    rank_ref[:, 1:2] = r2.astype(I32)
    cnt_ref[...] = run_ref[...]


def router(h, row0, n, wr, br, tm, cnt0):
    d = h.shape[1]
    two = pl.BlockSpec((tm, 2), lambda i: (i, 0))
    one = pl.BlockSpec((1, LANES), lambda i: (0, 0))
    return pl.pallas_call(
        _router_body, grid=(n // tm,),
        in_specs=[pl.BlockSpec((tm, d), lambda i: (i + row0 // tm, 0)),
                  pl.BlockSpec((d, 2 * LANES), lambda i: (0, 0)),
                  pl.BlockSpec((1, 2 * LANES), lambda i: (0, 0)), one],
        out_specs=[two, two, two, one],
        out_shape=[jax.ShapeDtypeStruct((n, 2), I32), jax.ShapeDtypeStruct((n, 2), F32),
                   jax.ShapeDtypeStruct((n, 2), I32), jax.ShapeDtypeStruct((1, LANES), F32)],
        scratch_shapes=[pltpu.VMEM((1, LANES), F32)],
        compiler_params=_cp(("arbitrary",)), name="router",
    )(h, wr, br, cnt0)


def moe_plan(eid, rank, cnt, dst_of_entry, spare_row0):
    nk = 2 * eid.shape[0]
    n_blocks = -(-nk // MOE_ROWS) + N_EXPERTS
    n_rows = n_blocks * MOE_ROWS
    counts = cnt[0, :N_EXPERTS].astype(I32)
    padded = (counts + MOE_ROWS - 1) // MOE_ROWS * MOE_ROWS
    pad_end = jnp.cumsum(padded)
    pad_start = pad_end - padded
    dest = (pad_start[eid] + rank).reshape(-1)
    src = jnp.full((n_rows + MOE_ROWS,), -1, I32).at[dest].set(jnp.arange(nk, dtype=I32))
    valid = src >= 0
    pos = jnp.arange(n_rows + MOE_ROWS, dtype=I32)
    safe = jnp.maximum(src, 0)
    tok = jnp.where(valid, safe >> 1, 0)
    dst = jnp.where(valid, dst_of_entry[safe], spare_row0 + pos % MOE_ROWS)
    n_used = pad_end[-1] // MOE_ROWS
    blk = jnp.arange(n_blocks, dtype=I32)
    be = jnp.minimum(jnp.sum((blk * MOE_ROWS)[:, None] >= pad_end[None, :], axis=-1), N_EXPERTS - 1).astype(I32)
    be = jnp.where(blk < n_used, be, be[jnp.maximum(n_used - 1, 0)])
    return tok, dst, be, n_used.reshape(1).astype(I32), n_blocks


def _moe_body(tok_ref, dst_ref, be_ref, nu_ref, h_hbm, w1_ref, w3_ref, w2_ref, ys_hbm,
              xbuf, ybuf, w1b, w3b, w2b, gsem, ssem, *, spare_row0):
    b = pl.program_id(0)
    n_used = nu_ref[0]
    slot = b % 2

    def gather_rows(blk, sl):
        for i in range(MOE_ROWS):
            pltpu.make_async_copy(h_hbm.at[pl.ds(tok_ref[blk * MOE_ROWS + i], 1)], xbuf.at[sl, pl.ds(i, 1)],
                                  gsem.at[sl]).start()

    def wait_gather(sl):
        pltpu.make_async_copy(xbuf.at[sl], xbuf.at[sl], gsem.at[sl]).wait()

    def scatter_rows(blk):
        for i in range(MOE_ROWS):
            pltpu.make_async_copy(ybuf.at[pl.ds(i, 1)], ys_hbm.at[pl.ds(dst_ref[blk * MOE_ROWS + i], 1)],
                                  ssem.at[0]).start()

    def wait_scatter():
        pltpu.make_async_copy(ybuf, ybuf, ssem.at[0]).wait()

    @pl.when(b == 0)
    def _():
        ybuf[...] = jnp.zeros_like(ybuf)
        pltpu.make_async_copy(ybuf, ys_hbm.at[pl.ds(spare_row0, MOE_ROWS)], ssem.at[0]).start()
        gather_rows(0, 0)

    @pl.when(b < n_used)
    def _():
        @pl.when(jnp.logical_or(b == 0, be_ref[b] != be_ref[jnp.maximum(b - 1, 0)]))
        def _():
            w1b[...] = w1_ref[...].astype(BF16)
            w3b[...] = w3_ref[...].astype(BF16)
            w2b[...] = w2_ref[...].astype(BF16)

        wait_gather(slot)
        gather_rows(b + 1, 1 - slot)
        x = xbuf[slot].astype(BF16)
        h1 = jnp.dot(x, w1b[...], preferred_element_type=F32)
        h3 = jnp.dot(x, w3b[...], preferred_element_type=F32)
        mid = (_silu(h1) * h3).astype(BF16)
        y = jnp.dot(mid, w2b[...], preferred_element_type=F32)
        wait_scatter()
        ybuf[...] = y
        scatter_rows(b)

        @pl.when(b == n_used - 1)
        def _():
            wait_gather(1 - slot)
            wait_scatter()


def moe_ffn(h, tok, dst, be, n_used, n_blocks, w1, w3, w2, layer):
    n, d = h.shape
    de = w1.shape[-1]
    wspec_in = pl.BlockSpec((None, None, d, de), lambda b, tok, dst, be, nu: (layer, be[b], 0, 0))
    wspec_out = pl.BlockSpec((None, None, de, d), lambda b, tok, dst, be, nu: (layer, be[b], 0, 0))
    return pl.pallas_call(
        functools.partial(_moe_body, spare_row0=2 * n),
        grid_spec=pltpu.PrefetchScalarGridSpec(
            num_scalar_prefetch=4, grid=(n_blocks,),
            in_specs=[pl.BlockSpec(memory_space=pl.ANY), wspec_in, wspec_in, wspec_out],
            out_specs=pl.BlockSpec(memory_space=pl.ANY),
            scratch_shapes=[pltpu.VMEM((2, MOE_ROWS, d), F32), pltpu.VMEM((MOE_ROWS, d), F32),
                            pltpu.VMEM((d, de), BF16), pltpu.VMEM((d, de), BF16), pltpu.VMEM((de, d), BF16),
                            pltpu.SemaphoreType.DMA((2,)), pltpu.SemaphoreType.DMA((1,))]),
        out_shape=jax.ShapeDtypeStruct((2 * n + MOE_ROWS, d), F32),
        compiler_params=_cp(("arbitrary",)), name="moe_ffn",
    )(tok, dst, be, n_used, h, w1, w3, w2)


def _conv_body(va_ref, bg_ref, cg_ref, w_ref, buf_ref, o_ref, nb_ref):
    u = cg_ref[...] * va_ref[...]
    t = u.shape[0]
    row = _iota(u.shape, 0)
    b0 = buf_ref[0:1, :]
    b1 = buf_ref[1:2, :]
    u1 = jnp.where(row == 0, b1, pltpu.roll(u, 1, axis=0))
    u2 = jnp.where(row == 0, b0, jnp.where(row == 1, b1, pltpu.roll(u, 2, axis=0)))
    w = w_ref[...]
    y = w[0:1] * u2 + w[1:2] * u1 + w[2:3] * u
    o_ref[...] = (bg_ref[...] * y).astype(o_ref.dtype)
    nb_ref[...] = u[t - 2:t, :]


def conv_seq(z3, conv_w, buf, tc=256):
    bsz, t, _ = z3.shape
    nc = D_CONV // tc
    col = lambda off: pl.BlockSpec((None, t, tc), lambda b, j: (b, 0, off + j))
    return pl.pallas_call(
        _conv_body, grid=(bsz, nc),
        in_specs=[col(0), col(nc), col(2 * nc),
                  pl.BlockSpec((CONV_WIDTH, tc), lambda b, j: (0, j)),
                  pl.BlockSpec((None, 2, tc), lambda b, j: (b, 0, j))],
        out_specs=[pl.BlockSpec((None, t, tc), lambda b, j: (b, 0, j)),
                   pl.BlockSpec((None, 2, tc), lambda b, j: (b, 0, j))],
        out_shape=[jax.ShapeDtypeStruct((bsz, t, D_CONV), BF16), jax.ShapeDtypeStruct((bsz, 2, D_CONV), F32)],
        compiler_params=_cp(("parallel", "parallel")), name="conv_seq",
    )(z3, z3, z3, conv_w, buf)


def _conv_step_body(va_ref, bg_ref, cg_ref, w_ref, b0_ref, b1_ref, o_ref, u_ref):
    u = cg_ref[...] * va_ref[...]
    w = w_ref[...]
    y = w[0:1] * b0_ref[...] + w[1:2] * b1_ref[...] + w[2:3] * u
    o_ref[...] = (bg_ref[...] * y).astype(o_ref.dtype)
    u_ref[...] = u


def conv_step(z2, conv_w, buf):
    bsz = z2.shape[0]
    col = lambda j: pl.BlockSpec((bsz, D_CONV), lambda i: (0, j))
    full = pl.BlockSpec((bsz, D_CONV), lambda i: (0, 0))
    a, u = pl.pallas_call(
        _conv_step_body, grid=(1,),
        in_specs=[col(0), col(1), col(2), pl.BlockSpec((CONV_WIDTH, D_CONV), lambda i: (0, 0)), full, full],
        out_specs=[full, full],
        out_shape=[jax.ShapeDtypeStruct((bsz, D_CONV), BF16), jax.ShapeDtypeStruct((bsz, D_CONV), F32)],
        compiler_params=_cp(("arbitrary",)), name="conv_step",
    )(z2, z2, z2, conv_w, buf[:, 0], buf[:, 1])
    return a, jnp.stack([buf[:, 1], u], axis=1)


def _hgrn_body(q_ref, f_ref, i_ref, g_ref, lb_ref, nw_ref, s0_ref, o_ref, so_ref, st_ref,
               *, chunk, sub, nchunk, t_real, layer):
    tstep = pl.program_id(2)
    tb = chunk * nchunk

    @pl.when(tstep == 0)
    def _():
        st_ref[...] = s0_ref[...].T

    lbl = lb_ref[...]
    e = jnp.exp(lbl - jnp.max(lbl, axis=0, keepdims=True))
    sm = e / jnp.sum(e, axis=0, keepdims=True)
    lb = jnp.zeros((1, lbl.shape[1]), F32)
    for r in range(1, layer + 1):
        lb = lb + sm[r:r + 1]
    tri = jnp.where(_iota((chunk, chunk), 0) >= _iota((chunk, chunk), 1), 1.0, 0.0).astype(BF16)
    nw = nw_ref[...]
    neg = jnp.float32(-jnp.inf)

    fr = f_ref[...]
    v = i_ref[...]
    logf = jnp.log(lb + (1.0 - lb) * _sigmoid(fr))
    kin = (1.0 - lb) * _sigmoid(-fr)
    if t_real is not None:
        live = (tstep * tb + _iota(fr.shape, 0)) < t_real
        logf = jnp.where(live, logf, 0.0)
        kin = jnp.where(live, kin, 0.0)
    qs = _silu(q_ref[...])
    causal = _iota((sub, sub, 1), 1) <= _iota((sub, sub, 1), 0)
    chunks = [slice(c * chunk, (c + 1) * chunk) for c in range(nchunk)]
    bb_l = [_dot_exact_lhs(tri, logf[rows]) for rows in chunks]
    qs_l = [qs[rows] for rows in chunks]
    kin_l = [kin[rows] for rows in chunks]
    v_l = [v[rows] for rows in chunks]
    bl_l = [bb[chunk - 1:chunk] for bb in bb_l]
    kv_l = [_dot(vc, kc * jnp.exp(bl - bb), TN) for vc, kc, bl, bb in zip(v_l, kin_l, bl_l, bb_l)]
    intra_l = []
    for qc, kc, vc, bb in zip(qs_l, kin_l, v_l, bb_l):
        parts = []
        for blk in range(chunk // sub):
            lo = blk * sub
            qi, bi, ki, vi = qc[lo:lo + sub], bb[lo:lo + sub], kc[lo:lo + sub], vc[lo:lo + sub]
            d = bi[:, None, :] - bi[None, :, :]
            sc = jnp.sum(qi[:, None, :] * ki[None, :, :] * jnp.exp(jnp.where(causal, d, neg)), axis=-1)
            oi = _dot(sc, vi)
            if blk > 0:
                anchor = bb[lo - 1:lo]
                qt = qi * jnp.exp(bi - anchor)
                kt = kc[:lo] * jnp.exp(anchor - bb[:lo])
                oi = oi + _dot(_dot(qt, kt, NT), vc[:lo])
            parts.append(oi)
        intra_l.append(parts[0] if len(parts) == 1 else jnp.concatenate(parts, axis=0))
    st = st_ref[...]
    st_l = []
    for bl, kv in zip(bl_l, kv_l):
        st_l.append(st)
        st = st * jnp.exp(bl) + kv
    st_ref[...] = st
    o_l = [_dot(qc * jnp.exp(bb), sc, NT) + oi for qc, bb, sc, oi in zip(qs_l, bb_l, st_l, intra_l)]
    o = o_l[0] if nchunk == 1 else jnp.concatenate(o_l, axis=0)
    o = o * lax.rsqrt(jnp.mean(o * o, axis=-1, keepdims=True) + RMS_EPS) * nw
    o_ref[...] = (o * _silu(g_ref[...])).astype(o_ref.dtype)

    @pl.when(tstep == pl.num_programs(2) - 1)
    def _():
        so_ref[...] = st_ref[...].T


def hgrn_seq(z3, hgrn_lb, norm_w, s0, layer, chunk, nchunk, t_real=None):
    bsz, t, _ = z3.shape
    tb = chunk * nchunk
    qoff = 3 * D_CONV // LANES
    col = lambda off: pl.BlockSpec((None, tb, LANES), lambda b, h, s: (b, s, off + h))
    st = pl.BlockSpec((None, None, HGRN_DK, HGRN_DV), lambda b, h, s: (b, h, 0, 0))
    body = functools.partial(_hgrn_body, chunk=chunk, sub=min(16, chunk), nchunk=nchunk, t_real=t_real, layer=layer)
    return pl.pallas_call(
        body, grid=(bsz, HGRN_HEADS, t // tb),
        in_specs=[col(qoff), col(qoff + HGRN_HEADS), col(qoff + 2 * HGRN_HEADS), col(qoff + 3 * HGRN_HEADS),
                  pl.BlockSpec((N_EVEN, LANES), lambda b, h, s: (0, h)),
                  pl.BlockSpec((1, HGRN_DV), lambda b, h, s: (0, 0)), st],
        out_specs=[pl.BlockSpec((None, tb, LANES), lambda b, h, s: (b, s, h)), st],
        out_shape=[jax.ShapeDtypeStruct((bsz, t, D_HV), BF16),
                   jax.ShapeDtypeStruct((bsz, HGRN_HEADS, HGRN_DK, HGRN_DV), F32)],
        scratch_shapes=[pltpu.VMEM((HGRN_DV, HGRN_DK), F32)],
        compiler_params=_cp(("parallel", "parallel", "arbitrary")), name="hgrn_seq",
    )(z3, z3, z3, z3, hgrn_lb, norm_w.reshape(1, HGRN_DV), s0)


def _attn_body(sink_ref, zq_ref, zkv_ref, cos_ref, sin_ref, ck_ref, cv_ref, o_ref, nk_ref, nv_ref, kp_ref, vp_ref,
               *, prev_valid, t_real):
    i = pl.program_id(1)
    w = WINDOW

    @pl.when(i == 0)
    def _():
        kp_ref[...] = ck_ref[...]
        vp_ref[...] = cv_ref[...]

    cos = cos_ref[...]
    sin = sin_ref[...]

    def rope(x):
        width = x.shape[1]
        reps = width // LANES
        first = (_iota(x.shape, 1) & (HEAD_DIM - 1)) < (HEAD_DIM // 2)
        rot = jnp.where(first, pltpu.roll(x, width - HEAD_DIM // 2, axis=1), pltpu.roll(x, HEAD_DIM // 2, axis=1))
        return x * jnp.tile(cos, (1, reps)) + rot * jnp.tile(sin, (1, reps))

    kv = zkv_ref[...]
    qr = rope(zq_ref[...]) * (HEAD_DIM ** -0.5)
    kr = rope(kv[:, :D_KV])
    v = kv[:, D_KV:]
    kprev = kp_ref[...]
    vprev = vp_ref[...]

    grp = ATTN_HEADS // ATTN_KV_HEADS
    rows = _iota((grp * w, 2 * w), 0)
    cols = _iota((grp * w, 2 * w), 1)
    delta = (rows & (w - 1)) + w - cols
    valid = (delta >= 0) & (delta <= w)
    if not prev_valid:
        valid = valid & (cols >= jnp.where(i > 0, 0, w))
    head_of_row = _iota((grp * w, 1), 0) >> int(math.log2(w))
    neg = jnp.float32(-jnp.inf)
    outs = []
    for g in range(ATTN_KV_HEADS):
        ls = slice(g * HEAD_DIM, (g + 1) * HEAD_DIM)
        kg = jnp.concatenate([kprev[:, ls], kr[:, ls]], axis=0)
        vg = jnp.concatenate([vprev[:, ls], v[:, ls]], axis=0)
        qg = jnp.concatenate([qr[:, (grp * g + hh) * HEAD_DIM:(grp * g + hh + 1) * HEAD_DIM] for hh in range(grp)],
                             axis=0)
        s = jnp.where(valid, _dot(qg, kg, NT), neg)
        sink = jnp.zeros((grp * w, 1), F32)
        for hh in range(grp):
            sink = jnp.where(head_of_row == hh, sink_ref[grp * g + hh], sink)
        m = jnp.maximum(jnp.max(s, axis=-1, keepdims=True), sink)
        p = jnp.exp(s - m)
        p = p / (jnp.sum(p, axis=-1, keepdims=True) + jnp.exp(sink - m))
        og = _dot(p, vg)
        outs += [og[hh * w:(hh + 1) * w] for hh in range(grp)]
    o_ref[...] = jnp.concatenate(outs, axis=1).astype(o_ref.dtype)
    kp_ref[...] = kr
    vp_ref[...] = v

    @pl.when(i == pl.num_programs(1) - 1)
    def _():
        if t_real == w:
            nk_ref[...] = kr
            nv_ref[...] = v
        else:
            last = _iota(kr.shape, 0) == w - 1
            nk_ref[...] = jnp.where(last, kr[0:1], pltpu.roll(kprev, w - 1, axis=0))
            nv_ref[...] = jnp.where(last, v[0:1], pltpu.roll(vprev, w - 1, axis=0))


def attn_seq(z3, sinks, cos, sin, cache_k, cache_v, prev_valid, t_real):
    bsz, t, _ = z3.shape
    nb = t // WINDOW
    cache = pl.BlockSpec((None, WINDOW, D_KV), lambda b, i: (b, 0, 0))
    tab = pl.BlockSpec((WINDOW, LANES), lambda b, i: (i, 0))
    body = functools.partial(_attn_body, prev_valid=prev_valid, t_real=t_real)
    return pl.pallas_call(
        body, grid=(bsz, nb),
        in_specs=[pl.BlockSpec(memory_space=pltpu.SMEM),
                  pl.BlockSpec((None, WINDOW, D_Q), lambda b, i: (b, i, 0)),
                  pl.BlockSpec((None, WINDOW, 2 * D_KV), lambda b, i: (b, i, D_Q // (2 * D_KV))),
                  tab, tab, cache, cache],
        out_specs=[pl.BlockSpec((None, WINDOW, D_Q), lambda b, i: (b, i, 0)), cache, cache],
        out_shape=[jax.ShapeDtypeStruct((bsz, t, D_Q), BF16),
                   jax.ShapeDtypeStruct((bsz, WINDOW, D_KV), F32), jax.ShapeDtypeStruct((bsz, WINDOW, D_KV), F32)],
        scratch_shapes=[pltpu.VMEM((WINDOW, D_KV), F32), pltpu.VMEM((WINDOW, D_KV), F32)],
        compiler_params=_cp(("parallel", "arbitrary")), name="attn_seq",
    )(sinks, z3, z3, cos, sin, cache_k, cache_v)


def rope_tables(pos):
    half = HEAD_DIM // 2
    inv = jnp.exp(-math.log(ROPE_THETA) * jnp.arange(half, dtype=F32) / half)
    ang = pos.astype(F32)[:, None] * inv[None, :]
    c, s = jnp.cos(ang), jnp.sin(ang)
    cos = jnp.concatenate([c, c, c, c], axis=1)
    sin = jnp.concatenate([-s, s, -s, s], axis=1)
    return cos, sin


def _rwkv_body(r_ref, k_ref, v_ref, wa_ref, gd_ref, sr_ref, sk_ref, sv_ref, swa_ref, sgd_ref,
               mr_ref, mk_ref, mv_ref, mwa_ref, mgd_ref, w0_ref, w2_ref, a0_ref, a2_ref, g2_ref,
               kkp_ref, ka_ref, rk_ref, lg_ref, lbias_ref, s0_ref, o_ref, so_ref, st_ref, prev_ref, prevw_ref,
               *, chunk, nchunk, npair, t_real):
    tstep = pl.program_id(2)
    n = RWKV_N
    ln = chunk
    tb = chunk * nchunk

    @pl.when(tstep == 0)
    def _():
        st_ref[...] = s0_ref[...]
        prev_ref[0:1, :] = sr_ref[...]
        prev_ref[1:2, :] = sk_ref[...]
        prev_ref[2:3, :] = sv_ref[...]
        prevw_ref[0:1, :] = swa_ref[...]
        prevw_ref[1:2, :] = sgd_ref[...]

    def mix(x_ref, mu_ref, p_ref, idx):
        x = x_ref[...]
        shifted = jnp.where(_iota(x.shape, 0) == 0, p_ref[idx:idx + 1, :], pltpu.roll(x, 1, axis=0))
        p_ref[idx:idx + 1, :] = x[tb - 1:tb, :]
        return x + mu_ref[...] * (shifted - x)

    r = mix(r_ref, mr_ref, prev_ref, 0)
    kr = mix(k_ref, mk_ref, prev_ref, 1)
    vr = mix(v_ref, mv_ref, prev_ref, 2)
    wa = mix(wa_ref, mwa_ref, prevw_ref, 0)
    gd = mix(gd_ref, mgd_ref, prevw_ref, 1)[:, :RWKV_RANK]
    wd = wa[:, :RWKV_RANK]
    ad = wa[:, RWKV_RANK:]

    w_log = -_softplus(-(w0_ref[...] + _dot3(jnp.tanh(wd), w2_ref[...]))) - 0.5
    logw = -jnp.exp(w_log)
    a = _sigmoid(a0_ref[...] + _dot3(ad, a2_ref[...]))
    gate = _dot3(_sigmoid(gd), g2_ref[...])

    sh = int(math.log2(n))
    seg = jnp.where((_iota((LANES, LANES), 0) >> sh) == (_iota((LANES, LANES), 1) >> sh), 1.0, 0.0).astype(BF16)

    def segsum(x):
        tiles = [_dot_exact_rhs(x[:, i * LANES:(i + 1) * LANES], seg) for i in range(npair)]
        return tiles[0] if npair == 1 else jnp.concatenate(tiles, axis=1)

    kkv = kr * kkp_ref[...]
    kk = kkv / jnp.maximum(jnp.sqrt(segsum(kkv * kkv)), 1e-12)
    kf = kr * (1.0 + (a - 1.0) * ka_ref[...])
    bonus = segsum(r * kf * rk_ref[...]) * vr
    if t_real is not None:
        live = (tstep * tb + _iota(logw.shape, 0)) < t_real
        zero = jnp.zeros_like(logw)
        logw = jnp.where(live, logw, zero)
        kk = jnp.where(live, kk, zero)
        kf = jnp.where(live, kf, zero)
        vr_s = jnp.where(live, vr, zero)
    else:
        vr_s = vr

    tri = jnp.where(_iota((ln, ln), 0) >= _iota((ln, ln), 1), 1.0, 0.0).astype(BF16)
    rr = _iota((ln, ln), 0)
    cc = _iota((ln, ln), 1)
    strict = rr > cc
    incl = rr >= cc
    eye = jnp.where(rr == cc, 1.0, 0.0)
    eye_n = jnp.where(_iota((n, n), 0) == _iota((n, n), 1), 1.0, 0.0)

    nhead = 2 * npair
    am_l, rm_l, bp_l, kp_l, bl_l, kl_l, v_l, gl_l = [], [], [], [], [], [], [], []
    for ci in range(nchunk):
        rows = slice(ci * ln, (ci + 1) * ln)
        lw = logw[rows]
        c = _dot_exact_lhs(tri, lw)
        gam = jnp.exp(c)
        ginv = jnp.exp(-c)
        am = -kk[rows] * jnp.exp(c - lw)
        rm = r[rows] * gam
        bp = kk[rows] * a[rows] * ginv
        kp = kf[rows] * ginv
        g_last = gam[ln - 1:ln, :]
        bpl = bp * g_last
        kpl = kp * g_last
        vv = vr_s[rows]
        for hd in range(nhead):
            ls = slice(hd * n, (hd + 1) * n)
            am_l.append(am[:, ls]); rm_l.append(rm[:, ls]); bp_l.append(bp[:, ls]); kp_l.append(kp[:, ls])
            bl_l.append(bpl[:, ls]); kl_l.append(kpl[:, ls]); v_l.append(vv[:, ls]); gl_l.append(g_last[:, ls])
    each = lambda fn, *ls: [fn(*xs) for xs in zip(*ls)]
    pw_l = each(lambda am, rm, bp, kp: _dot3(jnp.concatenate([am, rm], axis=0),
                                             jnp.concatenate([bp, kp], axis=0), NT), am_l, rm_l, bp_l, kp_l)
    m_l = each(lambda pw: jnp.where(strict, pw[:ln, :ln], 0.0), pw_l)
    nm_l = each(lambda pw: jnp.where(strict, pw[:ln, ln:], 0.0), pw_l)
    qb_l = each(lambda pw: jnp.where(incl, pw[ln:, :ln], 0.0), pw_l)
    qk_l = each(lambda pw: jnp.where(incl, pw[ln:, ln:], 0.0), pw_l)
    nv_l = each(_dot3, nm_l, v_l)
    qkv_l = each(_dot3, qk_l, v_l)
    kv_l = each(lambda v, kl: _dot3(v, kl, TN), v_l, kl_l)
    tinv_l = each(lambda m: eye + m, m_l)
    p_l = m_l
    for _ in range(int(math.log2(ln)) - 1):
        p_l = each(lambda p: _dot3(p, p), p_l)
        tinv_l = each(lambda t, p: t + _dot3(t, p), tinv_l, p_l)
    wt_l = each(_dot3, tinv_l, am_l)
    ut_l = each(_dot3, tinv_l, nv_l)
    a_l = each(lambda wt, bl, gl: eye_n * gl + _dot3(wt, bl, TN), wt_l, bl_l, gl_l)
    c_l = each(lambda ut, bl, kv: _dot3(ut, bl, TN) + kv, ut_l, bl_l, kv_l)
    rt_l = each(lambda rm, qb, wt: rm + _dot3(qb, wt), rm_l, qb_l, wt_l)
    o0_l = each(lambda qb, ut, qkv: _dot3(qb, ut) + qkv, qb_l, ut_l, qkv_l)

    s_l = [st_ref[hd] for hd in range(nhead)]
    o_rows = []
    for ci in range(nchunk):
        sl = slice(ci * nhead, (ci + 1) * nhead)
        o_rows.append(each(lambda rt, s, o0: _dot3(rt, s, NT) + o0, rt_l[sl], s_l, o0_l[sl]))
        s_l = each(lambda s, am, cm: _dot3(s, am) + cm, s_l, a_l[sl], c_l[sl])
    for hd in range(nhead):
        st_ref[hd] = s_l[hd]
    cols = [o_rows[0][hd] if nchunk == 1 else jnp.concatenate([o_rows[ci][hd] for ci in range(nchunk)], axis=0)
            for hd in range(nhead)]
    o = jnp.concatenate(cols, axis=1)
    mu_o = segsum(o) * (1.0 / n)
    dlt = o - mu_o
    var_o = segsum(dlt * dlt) * (1.0 / n)
    o = dlt * lax.rsqrt(var_o + RWKV_GN_EPS) * lg_ref[...] + lbias_ref[...]
    o_ref[...] = ((o + bonus) * gate).astype(o_ref.dtype)

    @pl.when(tstep == pl.num_programs(2) - 1)
    def _():
        so_ref[...] = st_ref[...]


def rwkv_seq(z3, shift, P, j, s0, chunk, nchunk=1, npair=1, t_real=None):
    bsz, t, _ = z3.shape
    tb = chunk * nchunk
    wl = npair * LANES
    groups = D_RWKV // wl
    zoff = (D_Q + 2 * D_KV) // LANES
    nb = D_RWKV // LANES
    wa_blk = 3 * nb
    gd_blk = 3 * nb + 1
    zc = lambda off: pl.BlockSpec((None, tb, wl), lambda b, p, s: (b, s, (zoff + off) // npair + p))
    zw = lambda blk: pl.BlockSpec((None, tb, LANES), lambda b, p, s: (b, s, zoff + blk))
    sc = lambda off: pl.BlockSpec((None, 1, wl), lambda b, p, s: (b, 0, off // npair + p))
    sw = lambda blk: pl.BlockSpec((None, 1, LANES), lambda b, p, s: (b, 0, blk))
    mc = lambda off: pl.BlockSpec((1, wl), lambda b, p, s: (0, off // npair + p))
    mw = lambda blk: pl.BlockSpec((1, LANES), lambda b, p, s: (0, blk))
    vec = pl.BlockSpec((1, wl), lambda b, p, s: (0, p))
    lora = pl.BlockSpec((RWKV_RANK, wl), lambda b, p, s: (0, p))
    st = pl.BlockSpec((None, 2 * npair, RWKV_N, RWKV_N), lambda b, p, s: (b, p, 0, 0))
    mu = jnp.pad(P['rwkv_mu'][j], (0, D_SHIFT_PAD - D_SHIFT)).reshape(1, D_SHIFT_PAD)
    row = lambda x: x.reshape(1, D_RWKV)
    body = functools.partial(_rwkv_body, chunk=chunk, nchunk=nchunk, npair=npair, t_real=t_real)
    return pl.pallas_call(
        body, grid=(bsz, groups, t // tb),
        in_specs=[zc(0), zc(nb), zc(2 * nb), zw(wa_blk), zw(gd_blk),
                  sc(0), sc(nb), sc(2 * nb), sw(wa_blk), sw(gd_blk),
                  mc(0), mc(nb), mc(2 * nb), mw(wa_blk), mw(gd_blk),
                  vec, lora, vec, lora, lora, vec, vec, vec, vec, vec, st],
        out_specs=[pl.BlockSpec((None, tb, wl), lambda b, p, s: (b, s, p)), st],
        out_shape=[jax.ShapeDtypeStruct((bsz, t, D_RWKV), BF16),
                   jax.ShapeDtypeStruct((bsz, RWKV_HEADS, RWKV_N, RWKV_N), F32)],
        scratch_shapes=[pltpu.VMEM((2 * npair, RWKV_N, RWKV_N), F32), pltpu.VMEM((8, wl), F32),
                        pltpu.VMEM((8, LANES), F32)],
        compiler_params=_cp(("parallel", "parallel", "arbitrary")), name="rwkv_seq",
    )(z3, z3, z3, z3, z3, shift, shift, shift, shift, shift, mu, mu, mu, mu, mu,
      row(P['rwkv_w0'][j]), P['rwkv_w2'][j], row(P['rwkv_a0'][j]), P['rwkv_a2'][j], P['rwkv_g2'][j],
      row(P['rwkv_kk'][j]), row(P['rwkv_ka'][j]), row(P['rwkv_rk'][j]), row(P['rwkv_lnx_g'][j]),
      row(P['rwkv_lnx_b'][j]), s0)


def _pad_time(z2, tp):
    return jnp.pad(z2[:, None, :], ((0, 0), (0, tp - 1), (0, 0)))


def _mix_layer(l, h, x, grp, pos0, W, P, st, single):
    n = x.shape[0]
    t = grp.t
    bsz = n // t
    tm = grp.tm
    j = l // 2
    step_pad = 8
    new = {}
    if l % 2 == 0:
        z = matmul(h, W['w_in_even'][j], tm, D_IN_EVEN // 4)
        if single:
            a_out, new['conv'] = conv_step(z, P['conv_w'][j], st['conv'][j])
            b3, new['hgrn'] = hgrn_seq(_pad_time(z, step_pad), P['hgrn_lb'], P['hgrn_norm'][j], st['hgrn'][j], j,
                                       chunk=step_pad, nchunk=1, t_real=1)
            b_out = b3[:, 0]
        else:
            z3 = z.reshape(bsz, t, D_IN_EVEN)
            a3, new['conv'] = conv_seq(z3, P['conv_w'][j], st['conv'][j])
            b3, new['hgrn'] = hgrn_seq(z3, P['hgrn_lb'], P['hgrn_norm'][j], st['hgrn'][j], j, chunk=64, nchunk=8)
            a_out, b_out = a3.reshape(n, D_CONV), b3.reshape(n, D_HV)
        wa, wb = W['w_out_even'][j][:D_CONV], W['w_out_even'][j][D_CONV:]
    else:
        z = matmul(h, W['w_in_odd'][j], tm, D_IN_ODD_PAD // 2)
        shift_in = jnp.pad(st['shift'][j], ((0, 0), (0, D_SHIFT_PAD - D_SHIFT)))[:, None, :]
        kc = st['k'][j].reshape(bsz, WINDOW, D_KV)
        vc = st['v'][j].reshape(bsz, WINDOW, D_KV)
        if single:
            cos, sin = rope_tables(pos0 + jnp.arange(WINDOW, dtype=I32))
            a3, nk, nv = attn_seq(_pad_time(z[:, :D_Q + 2 * D_KV], WINDOW), P['attn_sinks'][j], cos, sin,
                                  kc, vc, prev_valid=True, t_real=1)
            b3, new['rwkv'] = rwkv_seq(_pad_time(z, step_pad), shift_in, P, j, st['rwkv'][j], chunk=step_pad,
                                       npair=4, t_real=1)
            a_out, b_out = a3[:, 0], b3[:, 0]
            new['shift'] = z[:, D_Q + 2 * D_KV:D_IN_ODD]
        else:
            z3 = z.reshape(bsz, t, D_IN_ODD_PAD)
            cos, sin = rope_tables(pos0 + jnp.arange(t, dtype=I32))
            a3, nk, nv = attn_seq(z3, P['attn_sinks'][j], cos, sin, kc, vc, prev_valid=False, t_real=WINDOW)
            b3, new['rwkv'] = rwkv_seq(z3, shift_in, P, j, st['rwkv'][j], chunk=64, nchunk=4)
            a_out, b_out = a3.reshape(n, D_Q), b3.reshape(n, D_RWKV)
            new['shift'] = z3[:, t - 1, D_Q + 2 * D_KV:D_IN_ODD]
        new['k'] = nk.reshape(bsz, WINDOW, ATTN_KV_HEADS, HEAD_DIM)
        new['v'] = nv.reshape(bsz, WINDOW, ATTN_KV_HEADS, HEAD_DIM)
        wa, wb = W['w_out_odd'][j][:D_Q], W['w_out_odd'][j][D_Q:]
    return a_out, b_out, wa, wb, new


def kernel(x_prompt, x_sample, c_prompt, c_sample, state_conv, state_hgrn, cache_swa_k, cache_swa_v,
           state_rwkv, state_shift, ada_w, ada_b, ln_g, ln_b, w_in_even, w_out_even, conv_w, hgrn_lb,
           hgrn_norm, w_in_odd, w_out_odd, attn_sinks, rwkv_mu, rwkv_w0, rwkv_w2, rwkv_a0, rwkv_a2,
           rwkv_g2, rwkv_kk, rwkv_ka, rwkv_rk, rwkv_lnx_g, rwkv_lnx_b, moe_w_grp, moe_b_grp, moe_w_exp,
           moe_b_exp, moe_w1, moe_w3, moe_w2):
    P = dict(ln_g=ln_g, ln_b=ln_b, conv_w=conv_w, hgrn_lb=hgrn_lb, hgrn_norm=hgrn_norm, attn_sinks=attn_sinks,
             rwkv_mu=rwkv_mu, rwkv_w0=rwkv_w0, rwkv_w2=rwkv_w2, rwkv_a0=rwkv_a0, rwkv_a2=rwkv_a2,
             rwkv_g2=rwkv_g2, rwkv_kk=rwkv_kk, rwkv_ka=rwkv_ka, rwkv_rk=rwkv_rk.reshape(N_ODD, D_RWKV),
             rwkv_lnx_g=rwkv_lnx_g, rwkv_lnx_b=rwkv_lnx_b)
    bp, tp, d = x_prompt.shape
    bs, ts, _ = x_sample.shape
    n_p, n_s = bp * tp, bs * ts
    router_w = jnp.zeros((DEPTH, d, 2 * LANES), F32)
    router_w = router_w.at[:, :, :N_GROUPS].set(moe_w_grp).at[:, :, LANES:LANES + N_EXPERTS].set(moe_w_exp)
    router_b = jnp.zeros((DEPTH, 1, 2 * LANES), F32)
    router_b = router_b.at[:, 0, :N_GROUPS].set(moe_b_grp).at[:, 0, LANES:LANES + N_EXPERTS].set(moe_b_exp)
    W = dict(w_in_even=w_in_even.astype(BF16), w_out_even=w_out_even.astype(BF16),
             w_in_odd=jnp.pad(w_in_odd.astype(BF16), ((0, 0), (0, 0), (0, D_IN_ODD_PAD - D_IN_ODD))),
             w_out_odd=w_out_odd.astype(BF16))

    mod = ada_mod(jnp.concatenate([c_prompt, c_sample], axis=0), ada_w, ada_b)
    mod = mod.reshape(DEPTH, bp + bs, 6, d).transpose(0, 2, 1, 3)
    grp_p = _Group(n_p, tp, 512, mod[:, :, :bp, None, :], per_row=False)
    grp_s = _Group(n_s, ts, n_s, mod[:, :, bp:], per_row=True)

    zeros = lambda *s: jnp.zeros(s, F32)
    st_p = dict(conv=zeros(N_EVEN, bp, CONV_WIDTH - 1, D_CONV), hgrn=zeros(N_EVEN, bp, HGRN_HEADS, HGRN_DK, HGRN_DV),
                k=zeros(N_ODD, bp, WINDOW, ATTN_KV_HEADS, HEAD_DIM), v=zeros(N_ODD, bp, WINDOW, ATTN_KV_HEADS, HEAD_DIM),
                rwkv=zeros(N_ODD, bp, RWKV_HEADS, RWKV_N, RWKV_N), shift=zeros(N_ODD, bp, D_SHIFT))
    st_s = dict(conv=state_conv, hgrn=state_hgrn, k=cache_swa_k, v=cache_swa_v, rwkv=state_rwkv, shift=state_shift)
    new_p = {k: [] for k in st_p}
    new_s = {k: [] for k in st_s}

    n_all = n_p + n_s
    t_all = jnp.arange(n_all, dtype=I32)
    row_of = lambda k: jnp.where(t_all < n_p, k * n_p + t_all, 2 * n_p + k * n_s + (t_all - n_p))
    dst_of_entry = jnp.stack([row_of(0), row_of(1)], axis=1).reshape(-1)

    x_p, x_s = x_prompt.reshape(n_p, d), x_sample.reshape(n_s, d)
    h_all = zeros(n_all, d)
    h_p, h_s = modulate(x_p, grp_p, 0, 1, 0), modulate(x_s, grp_s, 0, 1, 0)
    for l in range(DEPTH):
        a_p, b_p, wa, wb, np_l = _mix_layer(l, h_p, x_p, grp_p, 0, W, P, st_p, single=False)
        a_s, b_s, _, _, ns_l = _mix_layer(l, h_s, x_s, grp_s, PAST_LEN, W, P, st_s, single=True)
        for k, v in np_l.items():
            new_p[k].append(v)
        for k, v in ns_l.items():
            new_s[k].append(v)
        x_p, h_all = out_proj_ln(a_p, b_p, wa, wb, x_p, grp_p, l, ln_g[l, 0], ln_b[l, 0], n_all, 0, h_all)
        x_s, h_all = out_proj_ln(a_s, b_s, wa, wb, x_s, grp_s, l, ln_g[l, 0], ln_b[l, 0], n_all, n_p, h_all)
        eid_p, gate_p, rank_p, cnt = router(h_all, 0, n_p, router_w[l], router_b[l], 256, zeros(1, LANES))
        eid_s, gate_s, rank_s, cnt = router(h_all, n_p, n_s, router_w[l], router_b[l], n_s, cnt)
        tok, dst, be, n_used, n_blocks = moe_plan(jnp.concatenate([eid_p, eid_s]), jnp.concatenate([rank_p, rank_s]),
                                                  cnt, dst_of_entry, 2 * n_all)
        ys = moe_ffn(h_all, tok, dst, be, n_used, n_blocks, moe_w1, moe_w3, moe_w2, l)
        last = l + 1 == DEPTH
        x_p, h_p = moe_combine_ln(ys, 0, n_p, gate_p, x_p, grp_p, l, ln_g[l, 1], ln_b[l, 1], has_next=not last)
        x_s, h_s = moe_combine_ln(ys, 2 * n_p, 2 * n_p + n_s, gate_s, x_s, grp_s, l, ln_g[l, 1], ln_b[l, 1],
                                  has_next=not last)
    order = ('conv', 'hgrn', 'k', 'v', 'rwkv', 'shift')
    return ((x_p.reshape(bp, tp, d), x_s.reshape(bs, ts, d))
            + tuple(jnp.stack(new_p[k]) for k in order) + tuple(jnp.stack(new_s[k]) for k in order))
```

```python
import functools
import math

import jax
import jax.numpy as jnp
from jax import lax
from jax.experimental import pallas as pl
from jax.experimental.pallas import tpu as pltpu

F32 = jnp.float32
BF16 = jnp.bfloat16
I32 = jnp.int32

D_MODEL = 2048
DEPTH = 4
PAST_LEN = 16384
N_EVEN = (DEPTH + 1) // 2
N_ODD = DEPTH // 2
D_CONV = 1024
CONV_WIDTH = 3
HGRN_HEADS = 8
HGRN_DK = 128
HGRN_DV = 128
D_HK = HGRN_HEADS * HGRN_DK
D_HV = HGRN_HEADS * HGRN_DV
ATTN_HEADS = 16
ATTN_KV_HEADS = 4
HEAD_DIM = 64
WINDOW = 128
ROPE_THETA = 10000.0
D_Q = ATTN_HEADS * HEAD_DIM
D_KV = ATTN_KV_HEADS * HEAD_DIM
RWKV_HEADS = 16
RWKV_N = 64
D_RWKV = RWKV_HEADS * RWKV_N
RWKV_RANK = 64
RWKV_GN_EPS = 64e-5
D_SHIFT = 3 * D_RWKV + 3 * RWKV_RANK
D_IN_EVEN = 3 * D_CONV + 2 * D_HK + 2 * D_HV
D_IN_ODD = D_Q + 2 * D_KV + D_SHIFT
N_GROUPS = 4
EXPERTS_PER_GROUP = 8
N_EXPERTS = N_GROUPS * EXPERTS_PER_GROUP
D_EXPERT = 512
ALPHA = (2 * DEPTH) ** 0.25
LN_EPS = 1e-5
RMS_EPS = 1e-6

LANES = 128
HGRN_SUB = 16
MOE_ROWS = 128
VMEM_LIMIT = 48 * 1024 * 1024

D_IN_ODD_PAD = -(-D_IN_ODD // LANES) * LANES
D_SHIFT_PAD = D_IN_ODD_PAD - (D_Q + 2 * D_KV)

NN = (((1,), (0,)), ((), ()))
NT = (((1,), (1,)), ((), ()))
TN = (((0,), (0,)), ((), ()))


def _cp(sem, vmem=VMEM_LIMIT):
    return pltpu.CompilerParams(dimension_semantics=sem, vmem_limit_bytes=vmem)


def _dot(a, b, dims=NN):
    return lax.dot_general(a.astype(BF16), b.astype(BF16), dims, preferred_element_type=F32)


def _split2(x):
    hi = x.astype(BF16)
    lo = (x - hi.astype(F32)).astype(BF16)
    return hi, lo


def _dot3(a, b, dims=NN):
    ah, al = _split2(a)
    bh, bl = _split2(b)
    d = lambda x, y: lax.dot_general(x, y, dims, preferred_element_type=F32)
    return d(ah, bh) + (d(ah, bl) + d(al, bh))


def _dot_exact_lhs(a_bf16, b, dims=NN):
    b1 = b.astype(BF16)
    r1 = b - b1.astype(F32)
    b2 = r1.astype(BF16)
    b3 = (r1 - b2.astype(F32)).astype(BF16)
    d = lambda y: lax.dot_general(a_bf16, y, dims, preferred_element_type=F32)
    return d(b1) + (d(b2) + d(b3))


def _dot_exact_rhs(a, b_bf16, dims=NN):
    a1 = a.astype(BF16)
    r1 = a - a1.astype(F32)
    a2 = r1.astype(BF16)
    d = lambda x: lax.dot_general(x, b_bf16, dims, preferred_element_type=F32)
    return d(a1) + d(a2)


def _sigmoid(x):
    return 1.0 / (1.0 + jnp.exp(-x))


def _silu(x):
    return x * _sigmoid(x)


def _softplus(x):
    return jnp.maximum(x, 0.0) + jnp.log(1.0 + jnp.exp(-jnp.abs(x)))


def _iota(shape, axis):
    return lax.broadcasted_iota(I32, shape, axis)


def _layer_norm(u, g, b):
    mu = jnp.mean(u, axis=-1, keepdims=True)
    d = u - mu
    var = jnp.mean(d * d, axis=-1, keepdims=True)
    return d * lax.rsqrt(var + LN_EPS) * g + b


def _ada_body(c_ref, w_ref, b_ref, o_ref):
    o_ref[...] = _dot(_silu(c_ref[...]), w_ref[...]) + b_ref[...]


def ada_mod(c_all, ada_w, ada_b, tn=1024):
    nl, d, n = ada_w.shape
    r = c_all.shape[0]
    return pl.pallas_call(
        _ada_body,
        grid=(nl, n // tn),
        in_specs=[pl.BlockSpec((r, d), lambda l, j: (0, 0)),
                  pl.BlockSpec((None, d, tn), lambda l, j: (l, 0, j)),
                  pl.BlockSpec((None, 1, tn), lambda l, j: (l, 0, j))],
        out_specs=pl.BlockSpec((None, r, tn), lambda l, j: (l, 0, j)),
        out_shape=jax.ShapeDtypeStruct((nl, r, n), F32),
        compiler_params=_cp(("parallel", "parallel")),
        name="ada_mod",
    )(c_all, ada_w, ada_b.reshape(nl, 1, n))


class _Group:
    def __init__(self, n, t, tm, modarr, per_row):
        self.n, self.t, self.tm, self.modarr, self.per_row = n, t, tm, modarr, per_row

    def mod(self, l, c):
        d = self.modarr.shape[-1]
        if self.per_row:
            return pl.BlockSpec((None, None, self.tm, d), lambda i: (l, c, i, 0))
        t, tm = self.t, self.tm
        return pl.BlockSpec((None, None, None, 1, d), lambda i: (l, c, (i * tm) // t, 0, 0))


def _modulate_body(x_ref, sc_ref, sh_ref, h_ref):
    h_ref[...] = (x_ref[...] * (1.0 + sc_ref[...]) + sh_ref[...]).astype(h_ref.dtype)


def modulate(x, grp, l, c_sc, c_sh):
    n, d = x.shape
    tm = grp.tm
    row = pl.BlockSpec((tm, d), lambda i: (i, 0))
    return pl.pallas_call(
        _modulate_body, grid=(n // tm,),
        in_specs=[row, grp.mod(l, c_sc), grp.mod(l, c_sh)],
        out_specs=row, out_shape=jax.ShapeDtypeStruct((n, d), BF16),
        compiler_params=_cp(("parallel",)), name="modulate",
    )(x, grp.modarr, grp.modarr)


def _mm_body(a_ref, w_ref, o_ref):
    o_ref[...] = jnp.dot(a_ref[...], w_ref[...], preferred_element_type=F32)


def matmul(a, w, tm, tn):
    n, k = a.shape
    nn = w.shape[1]
    return pl.pallas_call(
        _mm_body, grid=(n // tm, nn // tn),
        in_specs=[pl.BlockSpec((tm, k), lambda i, j: (i, 0)),
                  pl.BlockSpec((k, tn), lambda i, j: (0, j))],
        out_specs=pl.BlockSpec((tm, tn), lambda i, j: (i, j)),
        out_shape=jax.ShapeDtypeStruct((n, nn), F32),
        compiler_params=_cp(("parallel", "parallel")), name="in_proj",
    )(a, w)


def _outln_body(a_ref, b_ref, wa_ref, wb_ref, x_ref, gt_ref, g_ref, be_ref, sc_ref, sh_ref, *rest):
    xo_ref, ho_ref = rest[-2:]
    y = (jnp.dot(a_ref[...], wa_ref[...], preferred_element_type=F32)
         + jnp.dot(b_ref[...], wb_ref[...], preferred_element_type=F32))
    xn = _layer_norm(ALPHA * x_ref[...] + (1.0 + gt_ref[...]) * y, g_ref[...], be_ref[...])
    xo_ref[...] = xn
    ho_ref[...] = xn * (1.0 + sc_ref[...]) + sh_ref[...]


def out_proj_ln(a, b, wa, wb, x, grp, l, ln_g, ln_b, h_rows, h_row0, h_all=None):
    n, d = x.shape
    ka, kb = a.shape[1], b.shape[1]
    tm = min(grp.tm, 256)
    g2 = _Group(grp.n, grp.t, tm, grp.modarr, grp.per_row)
    row = pl.BlockSpec((tm, d), lambda i: (i, 0))
    vec = pl.BlockSpec((1, d), lambda i: (0, 0))
    in_specs = [pl.BlockSpec((tm, ka), lambda i: (i, 0)), pl.BlockSpec((tm, kb), lambda i: (i, 0)),
                pl.BlockSpec((ka, d), lambda i: (0, 0)), pl.BlockSpec((kb, d), lambda i: (0, 0)),
                row, g2.mod(l, 2), vec, vec, g2.mod(l, 4), g2.mod(l, 3)]
    args = [a, b, wa, wb, x, grp.modarr, ln_g.reshape(1, d), ln_b.reshape(1, d), grp.modarr, grp.modarr]
    aliases = {}
    if h_all is not None:
        in_specs.append(pl.BlockSpec(memory_space=pl.ANY))
        args.append(h_all)
        aliases = {len(args) - 1: 1}
    return pl.pallas_call(
        _outln_body, grid=(n // tm,),
        in_specs=in_specs,
        out_specs=[row, pl.BlockSpec((tm, d), lambda i: (i + h_row0 // tm, 0))],
        out_shape=[jax.ShapeDtypeStruct((n, d), F32), jax.ShapeDtypeStruct((h_rows, d), F32)],
        input_output_aliases=aliases,
        compiler_params=_cp(("parallel",)), name="out_proj_ln",
    )(*args)


def _comb_body(*refs, has_next):
    if has_next:
        y0_ref, y1_ref, gate_ref, x_ref, gt_ref, g_ref, be_ref, sc_ref, sh_ref, xo_ref, ho_ref = refs
    else:
        y0_ref, y1_ref, gate_ref, x_ref, gt_ref, g_ref, be_ref, xo_ref = refs
    gate = gate_ref[...]
    y = y0_ref[...] * gate[:, 0:1] + y1_ref[...] * gate[:, 1:2]
    xn = _layer_norm(ALPHA * x_ref[...] + (1.0 + gt_ref[...]) * y, g_ref[...], be_ref[...])
    xo_ref[...] = xn
    if has_next:
        ho_ref[...] = (xn * (1.0 + sc_ref[...]) + sh_ref[...]).astype(ho_ref.dtype)


def moe_combine_ln(ys, row0, row1, gate, x, grp, l, ln_g, ln_b, has_next):
    n, d = x.shape
    tm = min(grp.tm, 256)
    g2 = _Group(grp.n, grp.t, tm, grp.modarr, grp.per_row)
    nt = n // tm
    row = pl.BlockSpec((tm, d), lambda i: (i, 0))
    vec = pl.BlockSpec((1, d), lambda i: (0, 0))
    in_specs = [pl.BlockSpec((tm, d), lambda i: (i + row0 // tm, 0)),
                pl.BlockSpec((tm, d), lambda i: (i + row1 // tm, 0)), pl.BlockSpec((tm, 2), lambda i: (i, 0)),
                row, g2.mod(l, 5), vec, vec]
    args = [ys, ys, gate, x, grp.modarr, ln_g.reshape(1, d), ln_b.reshape(1, d)]
    out_specs = [row]
    out_shape = [jax.ShapeDtypeStruct((n, d), F32)]
    if has_next:
        in_specs += [g2.mod(l + 1, 1), g2.mod(l + 1, 0)]
        args += [grp.modarr, grp.modarr]
        out_specs.append(row)
        out_shape.append(jax.ShapeDtypeStruct((n, d), BF16))
    res = pl.pallas_call(
        functools.partial(_comb_body, has_next=has_next), grid=(nt,),
        in_specs=in_specs, out_specs=out_specs, out_shape=out_shape,
        compiler_params=_cp(("parallel",)), name="moe_combine_ln",
    )(*args)
    return (res[0], res[1]) if has_next else (res[0], None)


def _router_body(h_ref, w_ref, b_ref, cnt0_ref, eid_ref, gate_ref, rank_ref, cnt_ref, run_ref):
    i = pl.program_id(0)

    @pl.when(i == 0)
    def _():
        run_ref[...] = cnt0_ref[...]

    tm = h_ref.shape[0]
    logits = _dot3(h_ref[...], w_ref[...]) + b_ref[...]
    lg = logits[:, :LANES]
    le = logits[:, LANES:]
    lane = _iota((tm, LANES), 1).astype(F32)
    neg = jnp.float32(-jnp.inf)
    lgm = jnp.where(lane < N_GROUPS, lg, neg)
    mg = jnp.max(lgm, axis=-1, keepdims=True)
    gidx = jnp.min(jnp.where(lgm == mg, lane, float(LANES)), axis=-1, keepdims=True)
    p_g = 1.0 / jnp.sum(jnp.exp(lgm - mg), axis=-1, keepdims=True)
    lo = gidx * EXPERTS_PER_GROUP
    in_grp = (lane >= lo) & (lane < lo + EXPERTS_PER_GROUP)
    lem = jnp.where(in_grp, le, neg)
    v1 = jnp.max(lem, axis=-1, keepdims=True)
    i1 = jnp.min(jnp.where(lem == v1, lane, float(LANES)), axis=-1, keepdims=True)
    lem2 = jnp.where(lane == i1, neg, lem)
    v2 = jnp.max(lem2, axis=-1, keepdims=True)
    i2 = jnp.min(jnp.where(lem2 == v2, lane, float(LANES)), axis=-1, keepdims=True)
    e2 = jnp.exp(v2 - v1)
    g1 = p_g / (1.0 + e2)
    g2 = p_g * e2 / (1.0 + e2)
    oh1 = jnp.where(lane == i1, 1.0, 0.0)
    oh2 = jnp.where(lane == i2, 1.0, 0.0)
    comb = oh1 + oh2
    tri = jnp.where(_iota((tm, tm), 0) > _iota((tm, tm), 1), 1.0, 0.0).astype(BF16)
    before = jnp.dot(tri, comb.astype(BF16), preferred_element_type=F32) + run_ref[...]
    r1 = jnp.sum(before * oh1, axis=-1, keepdims=True)
    r2 = jnp.sum(before * oh2, axis=-1, keepdims=True)
    run_ref[...] = run_ref[...] + jnp.sum(comb, axis=0, keepdims=True)
    eid_ref[:, 0:1] = i1.astype(I32)
    eid_ref[:, 1:2] = i2.astype(I32)
    gate_ref[:, 0:1] = g1
    gate_ref[:, 1:2] = g2
    rank_ref[:, 0:1] = r1.astype(I32)
    rank_ref[:, 1:2] = r2.astype(I32)
    cnt_ref[...] = run_ref[...]


def router(h, row0, n, wr, br, tm, cnt0):
    d = h.shape[1]
    two = pl.BlockSpec((tm, 2), lambda i: (i, 0))
    one = pl.BlockSpec((1, LANES), lambda i: (0, 0))
    return pl.pallas_call(
        _router_body, grid=(n // tm,),
        in_specs=[pl.BlockSpec((tm, d), lambda i: (i + row0 // tm, 0)),
                  pl.BlockSpec((d, 2 * LANES), lambda i: (0, 0)),
                  pl.BlockSpec((1, 2 * LANES), lambda i: (0, 0)), one],
        out_specs=[two, two, two, one],
        out_shape=[jax.ShapeDtypeStruct((n, 2), I32), jax.ShapeDtypeStruct((n, 2), F32),
                   jax.ShapeDtypeStruct((n, 2), I32), jax.ShapeDtypeStruct((1, LANES), F32)],
        scratch_shapes=[pltpu.VMEM((1, LANES), F32)],
        compiler_params=_cp(("arbitrary",)), name="router",
    )(h, wr, br, cnt0)


def moe_plan(eid, rank, cnt, dst_of_entry, spare_row0):
    nk = 2 * eid.shape[0]
    n_blocks = -(-nk // MOE_ROWS) + N_EXPERTS
    n_rows = n_blocks * MOE_ROWS
    counts = cnt[0, :N_EXPERTS].astype(I32)
    padded = (counts + MOE_ROWS - 1) // MOE_ROWS * MOE_ROWS
    pad_end = jnp.cumsum(padded)
    pad_start = pad_end - padded
    dest = (pad_start[eid] + rank).reshape(-1)
    src = jnp.full((n_rows + MOE_ROWS,), -1, I32).at[dest].set(jnp.arange(nk, dtype=I32))
    valid = src >= 0
    pos = jnp.arange(n_rows + MOE_ROWS, dtype=I32)
    safe = jnp.maximum(src, 0)
    tok = jnp.where(valid, safe >> 1, 0)
    dst = jnp.where(valid, dst_of_entry[safe], spare_row0 + pos % MOE_ROWS)
    dst = jnp.concatenate([spare_row0 + jnp.arange(MOE_ROWS, dtype=I32), dst])
    n_used = pad_end[-1] // MOE_ROWS
    blk = jnp.arange(n_blocks, dtype=I32)
    be = jnp.minimum(jnp.sum((blk * MOE_ROWS)[:, None] >= pad_end[None, :], axis=-1), N_EXPERTS - 1).astype(I32)
    be = jnp.where(blk < n_used, be, be[jnp.maximum(n_used - 1, 0)])
    return tok, dst, be, n_used.reshape(1).astype(I32), n_blocks


def _moe_body(tok_ref, dst_ref, be_ref, nu_ref, h_hbm, w1_ref, w3_ref, w2_ref, ys_hbm,
              xbuf, ybuf, w1b, w3b, w2b, gsem, ssem, *, spare_row0):
    b = pl.program_id(0)
    n_used = nu_ref[0]
    slot = b % 2

    def gather_rows(blk, sl):
        for i in range(MOE_ROWS):
            pltpu.make_async_copy(h_hbm.at[pl.ds(tok_ref[blk * MOE_ROWS + i], 1)], xbuf.at[sl, pl.ds(i, 1)],
                                  gsem.at[sl]).start()

    def wait_gather(sl):
        pltpu.make_async_copy(xbuf.at[sl], xbuf.at[sl], gsem.at[sl]).wait()

    def scatter_rows(blk, sl):
        for i in range(MOE_ROWS):
            pltpu.make_async_copy(ybuf.at[sl, pl.ds(i, 1)],
                                  ys_hbm.at[pl.ds(dst_ref[(blk + 1) * MOE_ROWS + i], 1)], ssem.at[sl]).start()

    def wait_scatter(sl):
        pltpu.make_async_copy(ybuf.at[sl], ybuf.at[sl], ssem.at[sl]).wait()

    @pl.when(b == 0)
    def _():
        ybuf[...] = jnp.zeros_like(ybuf)
        gather_rows(0, 0)

    @pl.when(b < n_used)
    def _():
        @pl.when(jnp.logical_or(b == 0, be_ref[b] != be_ref[jnp.maximum(b - 1, 0)]))
        def _():
            w1b[...] = w1_ref[...].astype(BF16)
            w3b[...] = w3_ref[...].astype(BF16)
            w2b[...] = w2_ref[...].astype(BF16)

        wait_gather(slot)
        x = xbuf[slot].astype(BF16)
        h1 = jnp.dot(x, w1b[...], preferred_element_type=F32)
        h3 = jnp.dot(x, w3b[...], preferred_element_type=F32)
        mid = (_silu(h1) * h3).astype(BF16)
        y = jnp.dot(mid, w2b[...], preferred_element_type=F32)
        gather_rows(b + 1, 1 - slot)
        scatter_rows(b - 1, 1 - slot)

        @pl.when(b >= 1)
        def _():
            wait_scatter(slot)

        ybuf[slot] = y

        @pl.when(b == n_used - 1)
        def _():
            scatter_rows(b, slot)
            wait_gather(1 - slot)
            wait_scatter(1 - slot)
            wait_scatter(slot)


def moe_ffn(h, tok, dst, be, n_used, n_blocks, w1, w3, w2, layer):
    n, d = h.shape
    de = w1.shape[-1]
    wspec_in = pl.BlockSpec((None, None, d, de), lambda b, tok, dst, be, nu: (layer, be[b], 0, 0))
    wspec_out = pl.BlockSpec((None, None, de, d), lambda b, tok, dst, be, nu: (layer, be[b], 0, 0))
    return pl.pallas_call(
        functools.partial(_moe_body, spare_row0=2 * n),
        grid_spec=pltpu.PrefetchScalarGridSpec(
            num_scalar_prefetch=4, grid=(n_blocks,),
            in_specs=[pl.BlockSpec(memory_space=pl.ANY), wspec_in, wspec_in, wspec_out],
            out_specs=pl.BlockSpec(memory_space=pl.ANY),
            scratch_shapes=[pltpu.VMEM((2, MOE_ROWS, d), F32), pltpu.VMEM((2, MOE_ROWS, d), F32),
                            pltpu.VMEM((d, de), BF16), pltpu.VMEM((d, de), BF16), pltpu.VMEM((de, d), BF16),
                            pltpu.SemaphoreType.DMA((2,)), pltpu.SemaphoreType.DMA((2,))]),
        out_shape=jax.ShapeDtypeStruct((2 * n + MOE_ROWS, d), F32),
        compiler_params=_cp(("arbitrary",)), name="moe_ffn",
    )(tok, dst, be, n_used, h, w1, w3, w2)


def _conv_body(va_ref, bg_ref, cg_ref, w_ref, buf_ref, o_ref, nb_ref):
    u = cg_ref[...] * va_ref[...]
    t = u.shape[0]
    row = _iota(u.shape, 0)
    b0 = buf_ref[0:1, :]
    b1 = buf_ref[1:2, :]
    u1 = jnp.where(row == 0, b1, pltpu.roll(u, 1, axis=0))
    u2 = jnp.where(row == 0, b0, jnp.where(row == 1, b1, pltpu.roll(u, 2, axis=0)))
    w = w_ref[...]
    y = w[0:1] * u2 + w[1:2] * u1 + w[2:3] * u
    o_ref[...] = (bg_ref[...] * y).astype(o_ref.dtype)
    nb_ref[...] = u[t - 2:t, :]


def conv_seq(z3, conv_w, buf, tc=256):
    bsz, t, _ = z3.shape
    nc = D_CONV // tc
    col = lambda off: pl.BlockSpec((None, t, tc), lambda b, j: (b, 0, off + j))
    return pl.pallas_call(
        _conv_body, grid=(bsz, nc),
        in_specs=[col(0), col(nc), col(2 * nc),
                  pl.BlockSpec((CONV_WIDTH, tc), lambda b, j: (0, j)),
                  pl.BlockSpec((None, 2, tc), lambda b, j: (b, 0, j))],
        out_specs=[pl.BlockSpec((None, t, tc), lambda b, j: (b, 0, j)),
                   pl.BlockSpec((None, 2, tc), lambda b, j: (b, 0, j))],
        out_shape=[jax.ShapeDtypeStruct((bsz, t, D_CONV), BF16), jax.ShapeDtypeStruct((bsz, 2, D_CONV), F32)],
        compiler_params=_cp(("parallel", "parallel")), name="conv_seq",
    )(z3, z3, z3, conv_w, buf)


def _conv_step_body(va_ref, bg_ref, cg_ref, w_ref, b0_ref, b1_ref, o_ref, u_ref):
    u = cg_ref[...] * va_ref[...]
    w = w_ref[...]
    y = w[0:1] * b0_ref[...] + w[1:2] * b1_ref[...] + w[2:3] * u
    o_ref[...] = (bg_ref[...] * y).astype(o_ref.dtype)
    u_ref[...] = u


def conv_step(z2, conv_w, buf):
    bsz = z2.shape[0]
    col = lambda j: pl.BlockSpec((bsz, D_CONV), lambda i: (0, j))
    full = pl.BlockSpec((bsz, D_CONV), lambda i: (0, 0))
    a, u = pl.pallas_call(
        _conv_step_body, grid=(1,),
        in_specs=[col(0), col(1), col(2), pl.BlockSpec((CONV_WIDTH, D_CONV), lambda i: (0, 0)), full, full],
        out_specs=[full, full],
        out_shape=[jax.ShapeDtypeStruct((bsz, D_CONV), BF16), jax.ShapeDtypeStruct((bsz, D_CONV), F32)],
        compiler_params=_cp(("arbitrary",)), name="conv_step",
    )(z2, z2, z2, conv_w, buf[:, 0], buf[:, 1])
    return a, jnp.stack([buf[:, 1], u], axis=1)


def _hgrn_body(q_ref, f_ref, i_ref, g_ref, lb_ref, nw_ref, s0_ref, o_ref, so_ref, st_ref,
               *, chunk, sub, nchunk, nhead, t_real, layer, indep):
    tstep = pl.program_id(2)
    tb = chunk * nchunk
    dk = HGRN_DK

    if not indep:
        @pl.when(tstep == 0)
        def _():
            for hd in range(nhead):
                st_ref[hd] = s0_ref[hd].T

    lbl = lb_ref[...]
    e = jnp.exp(lbl - jnp.max(lbl, axis=0, keepdims=True))
    sm = e / jnp.sum(e, axis=0, keepdims=True)
    lb = jnp.zeros((1, lbl.shape[1]), F32)
    for r in range(1, layer + 1):
        lb = lb + sm[r:r + 1]
    tri = jnp.where(_iota((chunk, chunk), 0) >= _iota((chunk, chunk), 1), 1.0, 0.0).astype(BF16)
    nw = nw_ref[...]
    neg = jnp.float32(-jnp.inf)

    fr = f_ref[...]
    v = i_ref[...]
    logf = jnp.log(lb + (1.0 - lb) * _sigmoid(fr))
    kin = (1.0 - lb) * _sigmoid(-fr)
    if t_real is not None:
        row = _iota(fr.shape, 0)
        live = ((row & (chunk - 1)) if indep else (tstep * tb + row)) < t_real
        logf = jnp.where(live, logf, 0.0)
        kin = jnp.where(live, kin, 0.0)
    qs = _silu(q_ref[...])
    causal = _iota((sub, sub, 1), 1) <= _iota((sub, sub, 1), 0)
    cuts = [(slice(c * chunk, (c + 1) * chunk), slice(hd * dk, (hd + 1) * dk))
            for c in range(nchunk) for hd in range(nhead)]
    bb_all = [_dot_exact_lhs(tri, logf[c * chunk:(c + 1) * chunk]) for c in range(nchunk)]
    bb_l = [bb_all[c][:, hd * dk:(hd + 1) * dk] for c in range(nchunk) for hd in range(nhead)]
    qs_l = [qs[rows, ls] for rows, ls in cuts]
    kin_l = [kin[rows, ls] for rows, ls in cuts]
    v_l = [v[rows, ls] for rows, ls in cuts]
    bl_l = [bb[chunk - 1:chunk] for bb in bb_l]
    kv_l = [_dot(vc, kc * jnp.exp(bl - bb), TN) for vc, kc, bl, bb in zip(v_l, kin_l, bl_l, bb_l)]
    intra_l = []
    for qc, kc, vc, bb in zip(qs_l, kin_l, v_l, bb_l):
        parts = []
        for blk in range(chunk // sub):
            lo = blk * sub
            qi, bi, ki, vi = qc[lo:lo + sub], bb[lo:lo + sub], kc[lo:lo + sub], vc[lo:lo + sub]
            d = bi[:, None, :] - bi[None, :, :]
            sc = jnp.sum(qi[:, None, :] * ki[None, :, :] * jnp.exp(jnp.where(causal, d, neg)), axis=-1)
            oi = _dot(sc, vi)
            if blk > 0:
                anchor = bb[lo - 1:lo]
                qt = qi * jnp.exp(bi - anchor)
                kt = kc[:lo] * jnp.exp(anchor - bb[:lo])
                oi = oi + _dot(_dot(qt, kt, NT), vc[:lo])
            parts.append(oi)
        intra_l.append(parts[0] if len(parts) == 1 else jnp.concatenate(parts, axis=0))
    if indep:
        st_l = [s0_ref[c, hd].T for c in range(nchunk) for hd in range(nhead)]
        for idx, (st, bl, kv) in enumerate(zip(st_l, bl_l, kv_l)):
            so_ref[idx // nhead, idx % nhead] = (st * jnp.exp(bl) + kv).T
    else:
        st_l = []
        cur = [st_ref[hd] for hd in range(nhead)]
        for c in range(nchunk):
            for hd in range(nhead):
                idx = c * nhead + hd
                st_l.append(cur[hd])
                cur[hd] = cur[hd] * jnp.exp(bl_l[idx]) + kv_l[idx]
        for hd in range(nhead):
            st_ref[hd] = cur[hd]
    o_l = [_dot(qc * jnp.exp(bb), sc, NT) + oi for qc, bb, sc, oi in zip(qs_l, bb_l, st_l, intra_l)]
    o_l = [o * lax.rsqrt(jnp.mean(o * o, axis=-1, keepdims=True) + RMS_EPS) * nw for o in o_l]
    rows_l = [o_l[c * nhead] if nhead == 1 else jnp.concatenate(o_l[c * nhead:(c + 1) * nhead], axis=1)
              for c in range(nchunk)]
    o = rows_l[0] if nchunk == 1 else jnp.concatenate(rows_l, axis=0)
    o_ref[...] = (o * _silu(g_ref[...])).astype(o_ref.dtype)

    if not indep:
        @pl.when(tstep == pl.num_programs(2) - 1)
        def _():
            for hd in range(nhead):
                so_ref[hd] = st_ref[hd].T


def hgrn_seq(z3, hgrn_lb, norm_w, s0, layer, chunk, nchunk, nhead=1, t_real=None, indep=False):
    bsz, t, _ = z3.shape
    tb = chunk * nchunk
    sb = None
    if indep:
        sb = nchunk
        z3 = z3.reshape(bsz // nchunk, tb, z3.shape[2])
        bsz, t = bsz // nchunk, tb
    wl = nhead * LANES
    groups = HGRN_HEADS // nhead
    qoff = 3 * D_CONV // wl
    col = lambda k: pl.BlockSpec((None, tb, wl), lambda b, h, s: (b, s, qoff + k * groups + h))
    st = pl.BlockSpec((sb, nhead, HGRN_DK, HGRN_DV), lambda b, h, s: (b, h, 0, 0))
    body = functools.partial(_hgrn_body, chunk=chunk, sub=min(HGRN_SUB, chunk), nchunk=nchunk, nhead=nhead, t_real=t_real,
                             layer=layer, indep=indep)
    nseq = bsz * (sb or 1)
    o, so = pl.pallas_call(
        body, grid=(bsz, groups, t // tb),
        in_specs=[col(0), col(1), col(2), col(3),
                  pl.BlockSpec((N_EVEN, wl), lambda b, h, s: (0, h)),
                  pl.BlockSpec((1, HGRN_DV), lambda b, h, s: (0, 0)), st],
        out_specs=[pl.BlockSpec((None, tb, wl), lambda b, h, s: (b, s, h)), st],
        out_shape=[jax.ShapeDtypeStruct((bsz, t, D_HV), BF16),
                   jax.ShapeDtypeStruct((nseq, HGRN_HEADS, HGRN_DK, HGRN_DV), F32)],
        scratch_shapes=[pltpu.VMEM((nhead, HGRN_DV, HGRN_DK), F32)],
        compiler_params=_cp(("parallel", "parallel", "arbitrary")), name="hgrn_seq",
    )(z3, z3, z3, z3, hgrn_lb, norm_w.reshape(1, HGRN_DV), s0)
    return o.reshape(nseq, chunk if indep else t, D_HV), so


def _attn_body(sink_ref, zq_ref, zkv_ref, cos_ref, sin_ref, ck_ref, cv_ref, o_ref, nk_ref, nv_ref, kp_ref, vp_ref,
               *, prev_valid, t_real, tq):
    i = pl.program_id(1)
    w = WINDOW

    @pl.when(i == 0)
    def _():
        kp_ref[...] = ck_ref[...]
        vp_ref[...] = cv_ref[...]

    cos = cos_ref[...]
    sin = sin_ref[...]

    def rope(x):
        width = x.shape[1]
        reps = width // LANES
        first = (_iota(x.shape, 1) & (HEAD_DIM - 1)) < (HEAD_DIM // 2)
        rot = jnp.where(first, pltpu.roll(x, width - HEAD_DIM // 2, axis=1), pltpu.roll(x, HEAD_DIM // 2, axis=1))
        return x * jnp.tile(cos, (1, reps)) + rot * jnp.tile(sin, (1, reps))

    kv = zkv_ref[...]
    qr = rope(zq_ref[...]) * (HEAD_DIM ** -0.5)
    kr = rope(kv[:, :D_KV])
    v = kv[:, D_KV:]
    kprev = kp_ref[...]
    vprev = vp_ref[...]

    grp = ATTN_HEADS // ATTN_KV_HEADS
    rows = _iota((grp * tq, w + tq), 0)
    cols = _iota((grp * tq, w + tq), 1)
    delta = (rows & (tq - 1)) + w - cols
    valid = (delta >= 0) & (delta <= w)
    if not prev_valid:
        valid = valid & (cols >= jnp.where(i > 0, 0, w))
    head_of_row = _iota((grp * tq, 1), 0) >> int(math.log2(tq))
    neg = jnp.float32(-jnp.inf)
    outs = []
    for g in range(ATTN_KV_HEADS):
        ls = slice(g * HEAD_DIM, (g + 1) * HEAD_DIM)
        kg = jnp.concatenate([kprev[:, ls], kr[:, ls]], axis=0)
        vg = jnp.concatenate([vprev[:, ls], v[:, ls]], axis=0)
        qg = jnp.concatenate([qr[:, (grp * g + hh) * HEAD_DIM:(grp * g + hh + 1) * HEAD_DIM] for hh in range(grp)],
                             axis=0)
        s = jnp.where(valid, _dot(qg, kg, NT), neg)
        sink = jnp.zeros((grp * tq, 1), F32)
        for hh in range(grp):
            sink = jnp.where(head_of_row == hh, sink_ref[grp * g + hh], sink)
        m = jnp.maximum(jnp.max(s, axis=-1, keepdims=True), sink)
        p = jnp.exp(s - m)
        p = p / (jnp.sum(p, axis=-1, keepdims=True) + jnp.exp(sink - m))
        og = _dot(p, vg)
        outs += [og[hh * tq:(hh + 1) * tq] for hh in range(grp)]
    o_ref[...] = jnp.concatenate(outs, axis=1).astype(o_ref.dtype)
    if t_real == w:
        kp_ref[...] = kr
        vp_ref[...] = v

    @pl.when(i == pl.num_programs(1) - 1)
    def _():
        if t_real == w:
            nk_ref[...] = kr
            nv_ref[...] = v
        else:
            last = _iota(kprev.shape, 0) == w - 1
            nk_ref[...] = jnp.where(last, kr[0:1], pltpu.roll(kprev, w - 1, axis=0))
            nv_ref[...] = jnp.where(last, v[0:1], pltpu.roll(vprev, w - 1, axis=0))


def attn_seq(z3, sinks, cos, sin, cache_k, cache_v, prev_valid, t_real, tq=WINDOW):
    bsz, t, _ = z3.shape
    nb = t // tq
    assert tq == WINDOW or nb == 1
    cache = pl.BlockSpec((None, WINDOW, D_KV), lambda b, i: (b, 0, 0))
    tab = pl.BlockSpec((tq, LANES), lambda b, i: (i, 0))
    body = functools.partial(_attn_body, prev_valid=prev_valid, t_real=t_real, tq=tq)
    return pl.pallas_call(
        body, grid=(bsz, nb),
        in_specs=[pl.BlockSpec(memory_space=pltpu.SMEM),
                  pl.BlockSpec((None, tq, D_Q), lambda b, i: (b, i, 0)),
                  pl.BlockSpec((None, tq, 2 * D_KV), lambda b, i: (b, i, D_Q // (2 * D_KV))),
                  tab, tab, cache, cache],
        out_specs=[pl.BlockSpec((None, tq, D_Q), lambda b, i: (b, i, 0)), cache, cache],
        out_shape=[jax.ShapeDtypeStruct((bsz, t, D_Q), BF16),
                   jax.ShapeDtypeStruct((bsz, WINDOW, D_KV), F32), jax.ShapeDtypeStruct((bsz, WINDOW, D_KV), F32)],
        scratch_shapes=[pltpu.VMEM((WINDOW, D_KV), F32), pltpu.VMEM((WINDOW, D_KV), F32)],
        compiler_params=_cp(("parallel", "arbitrary")), name="attn_seq",
    )(sinks, z3, z3, cos, sin, cache_k, cache_v)


def rope_tables(pos):
    half = HEAD_DIM // 2
    inv = jnp.exp(-math.log(ROPE_THETA) * jnp.arange(half, dtype=F32) / half)
    ang = pos.astype(F32)[:, None] * inv[None, :]
    c, s = jnp.cos(ang), jnp.sin(ang)
    cos = jnp.concatenate([c, c, c, c], axis=1)
    sin = jnp.concatenate([-s, s, -s, s], axis=1)
    return cos, sin


def _rwkv_body(r_ref, k_ref, v_ref, wa_ref, gd_ref, sr_ref, sk_ref, sv_ref, swa_ref, sgd_ref,
               mr_ref, mk_ref, mv_ref, mwa_ref, mgd_ref, w0_ref, w2_ref, a0_ref, a2_ref, g2_ref,
               kkp_ref, ka_ref, rk_ref, lg_ref, lbias_ref, s0_ref, o_ref, so_ref, st_ref, prev_ref, prevw_ref,
               *, chunk, nchunk, npair, t_real, indep):
    tstep = pl.program_id(2)
    n = RWKV_N
    ln = chunk
    tb = chunk * nchunk

    if not indep:
        @pl.when(tstep == 0)
        def _():
            st_ref[...] = s0_ref[...]
            prev_ref[0:1, :] = sr_ref[...]
            prev_ref[1:2, :] = sk_ref[...]
            prev_ref[2:3, :] = sv_ref[...]
            prevw_ref[0:1, :] = swa_ref[...]
            prevw_ref[1:2, :] = sgd_ref[...]

    def mix(x_ref, mu_ref, s_ref, p_ref, idx):
        x = x_ref[...]
        row = _iota(x.shape, 0)
        if indep:
            before = jnp.broadcast_to(s_ref[...], (nchunk, ln, x.shape[1])).reshape(tb, x.shape[1])
            shifted = jnp.where((row & (ln - 1)) == 0, before, pltpu.roll(x, 1, axis=0))
        else:
            shifted = jnp.where(row == 0, p_ref[idx:idx + 1, :], pltpu.roll(x, 1, axis=0))
            p_ref[idx:idx + 1, :] = x[tb - 1:tb, :]
        return x + mu_ref[...] * (shifted - x)

    r = mix(r_ref, mr_ref, sr_ref, prev_ref, 0)
    kr = mix(k_ref, mk_ref, sk_ref, prev_ref, 1)
    vr = mix(v_ref, mv_ref, sv_ref, prev_ref, 2)
    wa = mix(wa_ref, mwa_ref, swa_ref, prevw_ref, 0)
    gd = mix(gd_ref, mgd_ref, sgd_ref, prevw_ref, 1)[:, :RWKV_RANK]
    wd = wa[:, :RWKV_RANK]
    ad = wa[:, RWKV_RANK:]

    w_log = -_softplus(-(w0_ref[...] + _dot3(jnp.tanh(wd), w2_ref[...]))) - 0.5
    logw = -jnp.exp(w_log)
    a = _sigmoid(a0_ref[...] + _dot3(ad, a2_ref[...]))
    gate = _dot3(_sigmoid(gd), g2_ref[...])

    sh = int(math.log2(n))
    seg = jnp.where((_iota((LANES, LANES), 0) >> sh) == (_iota((LANES, LANES), 1) >> sh), 1.0, 0.0).astype(BF16)

    def segsum(x):
        tiles = [_dot_exact_rhs(x[:, i * LANES:(i + 1) * LANES], seg) for i in range(npair)]
        return tiles[0] if npair == 1 else jnp.concatenate(tiles, axis=1)

    kkv = kr * kkp_ref[...]
    kk = kkv / jnp.maximum(jnp.sqrt(segsum(kkv * kkv)), 1e-12)
    kf = kr * (1.0 + (a - 1.0) * ka_ref[...])
    bonus = segsum(r * kf * rk_ref[...]) * vr
    if t_real is not None:
        row = _iota(logw.shape, 0)
        live = ((row & (ln - 1)) if indep else (tstep * tb + row)) < t_real
        zero = jnp.zeros_like(logw)
        logw = jnp.where(live, logw, zero)
        kk = jnp.where(live, kk, zero)
        kf = jnp.where(live, kf, zero)
        vr_s = jnp.where(live, vr, zero)
    else:
        vr_s = vr

    tri = jnp.where(_iota((ln, ln), 0) >= _iota((ln, ln), 1), 1.0, 0.0).astype(BF16)
    rr = _iota((ln, ln), 0)
    cc = _iota((ln, ln), 1)
    strict = rr > cc
    incl = rr >= cc
    eye = jnp.where(rr == cc, 1.0, 0.0)
    eye_n = jnp.where(_iota((n, n), 0) == _iota((n, n), 1), 1.0, 0.0)

    nhead = 2 * npair
    am_l, rm_l, bp_l, kp_l, bl_l, kl_l, v_l, gl_l = [], [], [], [], [], [], [], []
    for ci in range(nchunk):
        rows = slice(ci * ln, (ci + 1) * ln)
        lw = logw[rows]
        c = _dot_exact_lhs(tri, lw)
        gam = jnp.exp(c)
        ginv = jnp.exp(-c)
        am = -kk[rows] * jnp.exp(c - lw)
        rm = r[rows] * gam
        bp = kk[rows] * a[rows] * ginv
        kp = kf[rows] * ginv
        g_last = gam[ln - 1:ln, :]
        bpl = bp * g_last
        kpl = kp * g_last
        vv = vr_s[rows]
        for hd in range(nhead):
            ls = slice(hd * n, (hd + 1) * n)
            am_l.append(am[:, ls]); rm_l.append(rm[:, ls]); bp_l.append(bp[:, ls]); kp_l.append(kp[:, ls])
            bl_l.append(bpl[:, ls]); kl_l.append(kpl[:, ls]); v_l.append(vv[:, ls]); gl_l.append(g_last[:, ls])
    each = lambda fn, *ls: [fn(*xs) for xs in zip(*ls)]
    pw_l = each(lambda am, rm, bp, kp: _dot3(jnp.concatenate([am, rm], axis=0),
                                             jnp.concatenate([bp, kp], axis=0), NT), am_l, rm_l, bp_l, kp_l)
    m_l = each(lambda pw: jnp.where(strict, pw[:ln, :ln], 0.0), pw_l)
    nm_l = each(lambda pw: jnp.where(strict, pw[:ln, ln:], 0.0), pw_l)
    qb_l = each(lambda pw: jnp.where(incl, pw[ln:, :ln], 0.0), pw_l)
    qk_l = each(lambda pw: jnp.where(incl, pw[ln:, ln:], 0.0), pw_l)
    nq_l = each(lambda nm, qk, v: _dot3(jnp.concatenate([nm, qk], axis=0), v), nm_l, qk_l, v_l)
    nv_l = [x[:ln] for x in nq_l]
    qkv_l = [x[ln:] for x in nq_l]
    kv_l = each(lambda v, kl: _dot3(v, kl, TN), v_l, kl_l)
    tinv_l = each(lambda m: eye + m, m_l)
    p_l = m_l
    for _ in range(int(math.log2(ln)) - 1):
        p_l = each(lambda p: _dot3(p, p), p_l)
        tinv_l = each(lambda t, p: t + _dot3(t, p), tinv_l, p_l)
    wu_l = each(lambda t, am, nv: _dot3(t, jnp.concatenate([am, nv], axis=1)), tinv_l, am_l, nv_l)
    ac_l = each(lambda wu, bl: _dot3(wu, bl, TN), wu_l, bl_l)
    a_l = each(lambda ac, gl: eye_n * gl + ac[:n], ac_l, gl_l)
    c_l = each(lambda ac, kv: ac[n:] + kv, ac_l, kv_l)
    ro_l = each(_dot, qb_l, wu_l)
    rt_l = each(lambda rm, ro: rm + ro[:, :n], rm_l, ro_l)
    o0_l = each(lambda ro, qkv: ro[:, n:] + qkv, ro_l, qkv_l)

    o_rows = []
    if indep:
        s_l = [s0_ref[ci, hd] for ci in range(nchunk) for hd in range(nhead)]
        o_all = each(lambda rt, s, o0: _dot(rt, s, NT) + o0, rt_l, s_l, o0_l)
        s_l = each(lambda s, am, cm: _dot3(s, am) + cm, s_l, a_l, c_l)
        for ci in range(nchunk):
            o_rows.append(o_all[ci * nhead:(ci + 1) * nhead])
            for hd in range(nhead):
                so_ref[ci, hd] = s_l[ci * nhead + hd]
    else:
        s_l = [st_ref[hd] for hd in range(nhead)]
        for ci in range(nchunk):
            sl = slice(ci * nhead, (ci + 1) * nhead)
            o_rows.append(each(lambda rt, s, o0: _dot(rt, s, NT) + o0, rt_l[sl], s_l, o0_l[sl]))
            s_l = each(lambda s, am, cm: _dot3(s, am) + cm, s_l, a_l[sl], c_l[sl])
        for hd in range(nhead):
            st_ref[hd] = s_l[hd]
    cols = [o_rows[0][hd] if nchunk == 1 else jnp.concatenate([o_rows[ci][hd] for ci in range(nchunk)], axis=0)
            for hd in range(nhead)]
    o = jnp.concatenate(cols, axis=1)
    mu_o = segsum(o) * (1.0 / n)
    dlt = o - mu_o
    var_o = segsum(dlt * dlt) * (1.0 / n)
    o = dlt * lax.rsqrt(var_o + RWKV_GN_EPS) * lg_ref[...] + lbias_ref[...]
    o_ref[...] = ((o + bonus) * gate).astype(o_ref.dtype)

    if not indep:
        @pl.when(tstep == pl.num_programs(2) - 1)
        def _():
            so_ref[...] = st_ref[...]


def rwkv_seq(z3, shift, P, j, s0, chunk, nchunk=1, npair=1, t_real=None, indep=False):
    bsz, t, _ = z3.shape
    tb = chunk * nchunk
    sb = None
    if indep:
        sb = nchunk
        z3 = z3.reshape(bsz // nchunk, tb, z3.shape[2])
        bsz, t = bsz // nchunk, tb
    wl = npair * LANES
    groups = D_RWKV // wl
    zoff = (D_Q + 2 * D_KV) // LANES
    nb = D_RWKV // LANES
    wa_blk = 3 * nb
    gd_blk = 3 * nb + 1
    zc = lambda off: pl.BlockSpec((None, tb, wl), lambda b, p, s: (b, s, (zoff + off) // npair + p))
    zw = lambda blk: pl.BlockSpec((None, tb, LANES), lambda b, p, s: (b, s, zoff + blk))
    sc = lambda off: pl.BlockSpec((sb, 1, wl), lambda b, p, s: (b, 0, off // npair + p))
    sw = lambda blk: pl.BlockSpec((sb, 1, LANES), lambda b, p, s: (b, 0, blk))
    mc = lambda off: pl.BlockSpec((1, wl), lambda b, p, s: (0, off // npair + p))
    mw = lambda blk: pl.BlockSpec((1, LANES), lambda b, p, s: (0, blk))
    vec = pl.BlockSpec((1, wl), lambda b, p, s: (0, p))
    lora = pl.BlockSpec((RWKV_RANK, wl), lambda b, p, s: (0, p))
    st = pl.BlockSpec((sb, 2 * npair, RWKV_N, RWKV_N), lambda b, p, s: (b, p, 0, 0))
    mu = jnp.pad(P['rwkv_mu'][j], (0, D_SHIFT_PAD - D_SHIFT)).reshape(1, D_SHIFT_PAD)
    row = lambda x: x.reshape(1, D_RWKV)
    body = functools.partial(_rwkv_body, chunk=chunk, nchunk=nchunk, npair=npair, t_real=t_real, indep=indep)
    nseq = bsz * (sb or 1)
    o, so = pl.pallas_call(
        body, grid=(bsz, groups, t // tb),
        in_specs=[zc(0), zc(nb), zc(2 * nb), zw(wa_blk), zw(gd_blk),
                  sc(0), sc(nb), sc(2 * nb), sw(wa_blk), sw(gd_blk),
                  mc(0), mc(nb), mc(2 * nb), mw(wa_blk), mw(gd_blk),
                  vec, lora, vec, lora, lora, vec, vec, vec, vec, vec, st],
        out_specs=[pl.BlockSpec((None, tb, wl), lambda b, p, s: (b, s, p)), st],
        out_shape=[jax.ShapeDtypeStruct((bsz, t, D_RWKV), BF16),
                   jax.ShapeDtypeStruct((nseq, RWKV_HEADS, RWKV_N, RWKV_N), F32)],
        scratch_shapes=[pltpu.VMEM((2 * npair, RWKV_N, RWKV_N), F32), pltpu.VMEM((8, wl), F32),
                        pltpu.VMEM((8, LANES), F32)],
        compiler_params=_cp(("parallel", "parallel", "arbitrary")), name="rwkv_seq",
    )(z3, z3, z3, z3, z3, shift, shift, shift, shift, shift, mu, mu, mu, mu, mu,
      row(P['rwkv_w0'][j]), P['rwkv_w2'][j], row(P['rwkv_a0'][j]), P['rwkv_a2'][j], P['rwkv_g2'][j],
      row(P['rwkv_kk'][j]), row(P['rwkv_ka'][j]), row(P['rwkv_rk'][j]), row(P['rwkv_lnx_g'][j]),
      row(P['rwkv_lnx_b'][j]), s0)
    return o.reshape(nseq, chunk if indep else t, D_RWKV), so


def _pad_time(z2, tp):
    return jnp.pad(z2[:, None, :], ((0, 0), (0, tp - 1), (0, 0)))


def _mix_layer(l, h, x, grp, pos0, W, P, st, single):
    n = x.shape[0]
    t = grp.t
    bsz = n // t
    tm = grp.tm
    j = l // 2
    step_pad = 8
    new = {}
    if l % 2 == 0:
        z = matmul(h, W['w_in_even'][j], tm, D_IN_EVEN // 4)
        if single:
            a_out, new['conv'] = conv_step(z, P['conv_w'][j], st['conv'][j])
            b3, new['hgrn'] = hgrn_seq(_pad_time(z, step_pad), P['hgrn_lb'], P['hgrn_norm'][j], st['hgrn'][j], j,
                                       chunk=step_pad, nchunk=4, nhead=4, t_real=1, indep=True)
            b_out = b3[:, 0]
        else:
            z3 = z.reshape(bsz, t, D_IN_EVEN)
            a3, new['conv'] = conv_seq(z3, P['conv_w'][j], st['conv'][j])
            b3, new['hgrn'] = hgrn_seq(z3, P['hgrn_lb'], P['hgrn_norm'][j], st['hgrn'][j], j, chunk=64, nchunk=8)
            a_out, b_out = a3.reshape(n, D_CONV), b3.reshape(n, D_HV)
        wa, wb = W['w_out_even'][j][:D_CONV], W['w_out_even'][j][D_CONV:]
    else:
        z = matmul(h, W['w_in_odd'][j], tm, D_IN_ODD_PAD // 2)
        shift_in = jnp.pad(st['shift'][j], ((0, 0), (0, D_SHIFT_PAD - D_SHIFT)))[:, None, :]
        kc = st['k'][j].reshape(bsz, WINDOW, D_KV)
        vc = st['v'][j].reshape(bsz, WINDOW, D_KV)
        if single:
            cos, sin = rope_tables(pos0 + jnp.arange(step_pad, dtype=I32))
            a3, nk, nv = attn_seq(_pad_time(z[:, :D_Q + 2 * D_KV], step_pad), P['attn_sinks'][j], cos, sin,
                                  kc, vc, prev_valid=True, t_real=1, tq=step_pad)
            b3, new['rwkv'] = rwkv_seq(_pad_time(z, step_pad), shift_in, P, j, st['rwkv'][j], chunk=step_pad,
                                       nchunk=4, npair=4, t_real=1, indep=True)
            a_out, b_out = a3[:, 0], b3[:, 0]
            new['shift'] = z[:, D_Q + 2 * D_KV:D_IN_ODD]
        else:
            z3 = z.reshape(bsz, t, D_IN_ODD_PAD)
            cos, sin = rope_tables(pos0 + jnp.arange(t, dtype=I32))
            a3, nk, nv = attn_seq(z3, P['attn_sinks'][j], cos, sin, kc, vc, prev_valid=False, t_real=WINDOW)
            b3, new['rwkv'] = rwkv_seq(z3, shift_in, P, j, st['rwkv'][j], chunk=64, nchunk=4, npair=2)
            a_out, b_out = a3.reshape(n, D_Q), b3.reshape(n, D_RWKV)
            new['shift'] = z3[:, t - 1, D_Q + 2 * D_KV:D_IN_ODD]
        new['k'] = nk.reshape(bsz, WINDOW, ATTN_KV_HEADS, HEAD_DIM)
        new['v'] = nv.reshape(bsz, WINDOW, ATTN_KV_HEADS, HEAD_DIM)
        wa, wb = W['w_out_odd'][j][:D_Q], W['w_out_odd'][j][D_Q:]
    return a_out, b_out, wa, wb, new


def kernel(x_prompt, x_sample, c_prompt, c_sample, state_conv, state_hgrn, cache_swa_k, cache_swa_v,
           state_rwkv, state_shift, ada_w, ada_b, ln_g, ln_b, w_in_even, w_out_even, conv_w, hgrn_lb,
           hgrn_norm, w_in_odd, w_out_odd, attn_sinks, rwkv_mu, rwkv_w0, rwkv_w2, rwkv_a0, rwkv_a2,
           rwkv_g2, rwkv_kk, rwkv_ka, rwkv_rk, rwkv_lnx_g, rwkv_lnx_b, moe_w_grp, moe_b_grp, moe_w_exp,
           moe_b_exp, moe_w1, moe_w3, moe_w2):
    P = dict(ln_g=ln_g, ln_b=ln_b, conv_w=conv_w, hgrn_lb=hgrn_lb, hgrn_norm=hgrn_norm, attn_sinks=attn_sinks,
             rwkv_mu=rwkv_mu, rwkv_w0=rwkv_w0, rwkv_w2=rwkv_w2, rwkv_a0=rwkv_a0, rwkv_a2=rwkv_a2,
             rwkv_g2=rwkv_g2, rwkv_kk=rwkv_kk, rwkv_ka=rwkv_ka, rwkv_rk=rwkv_rk.reshape(N_ODD, D_RWKV),
             rwkv_lnx_g=rwkv_lnx_g, rwkv_lnx_b=rwkv_lnx_b)
    bp, tp, d = x_prompt.shape
    bs, ts, _ = x_sample.shape
    n_p, n_s = bp * tp, bs * ts
    router_w = jnp.zeros((DEPTH, d, 2 * LANES), F32)
    router_w = router_w.at[:, :, :N_GROUPS].set(moe_w_grp).at[:, :, LANES:LANES + N_EXPERTS].set(moe_w_exp)
    router_b = jnp.zeros((DEPTH, 1, 2 * LANES), F32)
    router_b = router_b.at[:, 0, :N_GROUPS].set(moe_b_grp).at[:, 0, LANES:LANES + N_EXPERTS].set(moe_b_exp)
    W = dict(w_in_even=w_in_even.astype(BF16), w_out_even=w_out_even.astype(BF16),
             w_in_odd=jnp.pad(w_in_odd.astype(BF16), ((0, 0), (0, 0), (0, D_IN_ODD_PAD - D_IN_ODD))),
             w_out_odd=w_out_odd.astype(BF16))

    mod = ada_mod(jnp.concatenate([c_prompt, c_sample], axis=0), ada_w, ada_b)
    mod = mod.reshape(DEPTH, bp + bs, 6, d).transpose(0, 2, 1, 3)
    grp_p = _Group(n_p, tp, 512, mod[:, :, :bp, None, :], per_row=False)
    grp_s = _Group(n_s, ts, n_s, mod[:, :, bp:], per_row=True)

    zeros = lambda *s: jnp.zeros(s, F32)
    st_p = dict(conv=zeros(N_EVEN, bp, CONV_WIDTH - 1, D_CONV), hgrn=zeros(N_EVEN, bp, HGRN_HEADS, HGRN_DK, HGRN_DV),
                k=zeros(N_ODD, bp, WINDOW, ATTN_KV_HEADS, HEAD_DIM), v=zeros(N_ODD, bp, WINDOW, ATTN_KV_HEADS, HEAD_DIM),
                rwkv=zeros(N_ODD, bp, RWKV_HEADS, RWKV_N, RWKV_N), shift=zeros(N_ODD, bp, D_SHIFT))
    st_s = dict(conv=state_conv, hgrn=state_hgrn, k=cache_swa_k, v=cache_swa_v, rwkv=state_rwkv, shift=state_shift)
    new_p = {k: [] for k in st_p}
    new_s = {k: [] for k in st_s}

    n_all = n_p + n_s
    t_all = jnp.arange(n_all, dtype=I32)
    row_of = lambda k: jnp.where(t_all < n_p, k * n_p + t_all, 2 * n_p + k * n_s + (t_all - n_p))
    dst_of_entry = jnp.stack([row_of(0), row_of(1)], axis=1).reshape(-1)

    x_p, x_s = x_prompt.reshape(n_p, d), x_sample.reshape(n_s, d)
    h_all = zeros(n_all, d)
    h_p, h_s = modulate(x_p, grp_p, 0, 1, 0), modulate(x_s, grp_s, 0, 1, 0)
    for l in range(DEPTH):
        a_p, b_p, wa, wb, np_l = _mix_layer(l, h_p, x_p, grp_p, 0, W, P, st_p, single=False)
        a_s, b_s, _, _, ns_l = _mix_layer(l, h_s, x_s, grp_s, PAST_LEN, W, P, st_s, single=True)
        for k, v in np_l.items():
            new_p[k].append(v)
        for k, v in ns_l.items():
            new_s[k].append(v)
        x_p, h_all = out_proj_ln(a_p, b_p, wa, wb, x_p, grp_p, l, ln_g[l, 0], ln_b[l, 0], n_all, 0, h_all)
        x_s, h_all = out_proj_ln(a_s, b_s, wa, wb, x_s, grp_s, l, ln_g[l, 0], ln_b[l, 0], n_all, n_p, h_all)
        eid_p, gate_p, rank_p, cnt = router(h_all, 0, n_p, router_w[l], router_b[l], 256, zeros(1, LANES))
        eid_s, gate_s, rank_s, cnt = router(h_all, n_p, n_s, router_w[l], router_b[l], n_s, cnt)
        tok, dst, be, n_used, n_blocks = moe_plan(jnp.concatenate([eid_p, eid_s]), jnp.concatenate([rank_p, rank_s]),
                                                  cnt, dst_of_entry, 2 * n_all)
        ys = moe_ffn(h_all, tok, dst, be, n_used, n_blocks, moe_w1, moe_w3, moe_w2, l)
        last = l + 1 == DEPTH
        x_p, h_p = moe_combine_ln(ys, 0, n_p, gate_p, x_p, grp_p, l, ln_g[l, 1], ln_b[l, 1], has_next=not last)
        x_s, h_s = moe_combine_ln(ys, 2 * n_p, 2 * n_p + n_s, gate_s, x_s, grp_s, l, ln_g[l, 1], ln_b[l, 1],
                                  has_next=not last)
    order = ('conv', 'hgrn', 'k', 'v', 'rwkv', 'shift')
    return ((x_p.reshape(bp, tp, d), x_s.reshape(bs, ts, d))
            + tuple(jnp.stack(new_p[k]) for k in order) + tuple(jnp.stack(new_s[k]) for k in order))
```

```python
import functools
import math

import jax
import jax.numpy as jnp
from jax import lax
from jax.experimental import pallas as pl
from jax.experimental.pallas import tpu as pltpu

F32 = jnp.float32
BF16 = jnp.bfloat16
I32 = jnp.int32

D_MODEL = 2048
DEPTH = 4
PAST_LEN = 16384
N_EVEN = (DEPTH + 1) // 2
N_ODD = DEPTH // 2
D_CONV = 1024
CONV_WIDTH = 3
HGRN_HEADS = 8
HGRN_DK = 128
HGRN_DV = 128
D_HK = HGRN_HEADS * HGRN_DK
D_HV = HGRN_HEADS * HGRN_DV
ATTN_HEADS = 16
ATTN_KV_HEADS = 4
HEAD_DIM = 64
WINDOW = 128
ROPE_THETA = 10000.0
D_Q = ATTN_HEADS * HEAD_DIM
D_KV = ATTN_KV_HEADS * HEAD_DIM
RWKV_HEADS = 16
RWKV_N = 64
D_RWKV = RWKV_HEADS * RWKV_N
RWKV_RANK = 64
RWKV_GN_EPS = 64e-5
D_SHIFT = 3 * D_RWKV + 3 * RWKV_RANK
D_IN_EVEN = 3 * D_CONV + 2 * D_HK + 2 * D_HV
D_IN_ODD = D_Q + 2 * D_KV + D_SHIFT
N_GROUPS = 4
EXPERTS_PER_GROUP = 8
N_EXPERTS = N_GROUPS * EXPERTS_PER_GROUP
D_EXPERT = 512
ALPHA = (2 * DEPTH) ** 0.25
LN_EPS = 1e-5
RMS_EPS = 1e-6

LANES = 128
HGRN_SUB = 16
MOE_ROWS = 128
VMEM_LIMIT = 48 * 1024 * 1024

D_IN_ODD_PAD = -(-D_IN_ODD // LANES) * LANES
D_SHIFT_PAD = D_IN_ODD_PAD - (D_Q + 2 * D_KV)

NN = (((1,), (0,)), ((), ()))
NT = (((1,), (1,)), ((), ()))
TN = (((0,), (0,)), ((), ()))


def _cp(sem, vmem=VMEM_LIMIT):
    return pltpu.CompilerParams(dimension_semantics=sem, vmem_limit_bytes=vmem)


def _dot(a, b, dims=NN):
    return lax.dot_general(a.astype(BF16), b.astype(BF16), dims, preferred_element_type=F32)


def _split2(x):
    hi = x.astype(BF16)
    lo = (x - hi.astype(F32)).astype(BF16)
    return hi, lo


def _dot3(a, b, dims=NN):
    ah, al = _split2(a)
    bh, bl = _split2(b)
    d = lambda x, y: lax.dot_general(x, y, dims, preferred_element_type=F32)
    return d(ah, bh) + (d(ah, bl) + d(al, bh))


def _dot_exact_lhs(a_bf16, b, dims=NN):
    b1 = b.astype(BF16)
    r1 = b - b1.astype(F32)
    b2 = r1.astype(BF16)
    b3 = (r1 - b2.astype(F32)).astype(BF16)
    d = lambda y: lax.dot_general(a_bf16, y, dims, preferred_element_type=F32)
    return d(b1) + (d(b2) + d(b3))


def _dot_exact_rhs(a, b_bf16, dims=NN):
    a1 = a.astype(BF16)
    r1 = a - a1.astype(F32)
    a2 = r1.astype(BF16)
    d = lambda x: lax.dot_general(x, b_bf16, dims, preferred_element_type=F32)
    return d(a1) + d(a2)


def _sigmoid(x):
    return 1.0 / (1.0 + jnp.exp(-x))


def _silu(x):
    return x * _sigmoid(x)


def _softplus(x):
    return jnp.maximum(x, 0.0) + jnp.log(1.0 + jnp.exp(-jnp.abs(x)))


def _iota(shape, axis):
    return lax.broadcasted_iota(I32, shape, axis)


def _layer_norm(u, g, b):
    mu = jnp.mean(u, axis=-1, keepdims=True)
    d = u - mu
    var = jnp.mean(d * d, axis=-1, keepdims=True)
    return d * lax.rsqrt(var + LN_EPS) * g + b


def _ada_body(c_ref, w_ref, b_ref, o_ref):
    o_ref[...] = _dot(_silu(c_ref[...]), w_ref[...]) + b_ref[...]


def ada_mod(c_all, ada_w, ada_b, tn=1024):
    nl, d, n = ada_w.shape
    r = c_all.shape[0]
    return pl.pallas_call(
        _ada_body,
        grid=(nl, n // tn),
        in_specs=[pl.BlockSpec((r, d), lambda l, j: (0, 0)),
                  pl.BlockSpec((None, d, tn), lambda l, j: (l, 0, j)),
                  pl.BlockSpec((None, 1, tn), lambda l, j: (l, 0, j))],
        out_specs=pl.BlockSpec((None, r, tn), lambda l, j: (l, 0, j)),
        out_shape=jax.ShapeDtypeStruct((nl, r, n), F32),
        compiler_params=_cp(("parallel", "parallel")),
        name="ada_mod",
    )(c_all, ada_w, ada_b.reshape(nl, 1, n))


class _Group:
    def __init__(self, n, t, tm, modarr, per_row):
        self.n, self.t, self.tm, self.modarr, self.per_row = n, t, tm, modarr, per_row

    def mod(self, l, c):
        d = self.modarr.shape[-1]
        if self.per_row:
            return pl.BlockSpec((None, None, self.tm, d), lambda i: (l, c, i, 0))
        t, tm = self.t, self.tm
        return pl.BlockSpec((None, None, None, 1, d), lambda i: (l, c, (i * tm) // t, 0, 0))


def _modulate_body(x_ref, sc_ref, sh_ref, h_ref):
    h_ref[...] = (x_ref[...] * (1.0 + sc_ref[...]) + sh_ref[...]).astype(h_ref.dtype)


def modulate(x, grp, l, c_sc, c_sh):
    n, d = x.shape
    tm = grp.tm
    row = pl.BlockSpec((tm, d), lambda i: (i, 0))
    return pl.pallas_call(
        _modulate_body, grid=(n // tm,),
        in_specs=[row, grp.mod(l, c_sc), grp.mod(l, c_sh)],
        out_specs=row, out_shape=jax.ShapeDtypeStruct((n, d), BF16),
        compiler_params=_cp(("parallel",)), name="modulate",
    )(x, grp.modarr, grp.modarr)


def _mm_body(a_ref, w_ref, o_ref):
    o_ref[...] = jnp.dot(a_ref[...], w_ref[...], preferred_element_type=F32)


def matmul(a, w, tm, tn):
    n, k = a.shape
    nn = w.shape[1]
    return pl.pallas_call(
        _mm_body, grid=(n // tm, nn // tn),
        in_specs=[pl.BlockSpec((tm, k), lambda i, j: (i, 0)),
                  pl.BlockSpec((k, tn), lambda i, j: (0, j))],
        out_specs=pl.BlockSpec((tm, tn), lambda i, j: (i, j)),
        out_shape=jax.ShapeDtypeStruct((n, nn), F32),
        compiler_params=_cp(("parallel", "parallel")), name="in_proj",
    )(a, w)


def _outln_body(a_ref, b_ref, wa_ref, wb_ref, x_ref, gt_ref, g_ref, be_ref, sc_ref, sh_ref, *rest):
    xo_ref, ho_ref = rest[-2:]
    y = (jnp.dot(a_ref[...], wa_ref[...], preferred_element_type=F32)
         + jnp.dot(b_ref[...], wb_ref[...], preferred_element_type=F32))
    xn = _layer_norm(ALPHA * x_ref[...] + (1.0 + gt_ref[...]) * y, g_ref[...], be_ref[...])
    xo_ref[...] = xn
    ho_ref[...] = xn * (1.0 + sc_ref[...]) + sh_ref[...]


def out_proj_ln(a, b, wa, wb, x, grp, l, ln_g, ln_b, h_rows, h_row0, h_all=None):
    n, d = x.shape
    ka, kb = a.shape[1], b.shape[1]
    tm = min(grp.tm, 256)
    g2 = _Group(grp.n, grp.t, tm, grp.modarr, grp.per_row)
    row = pl.BlockSpec((tm, d), lambda i: (i, 0))
    vec = pl.BlockSpec((1, d), lambda i: (0, 0))
    in_specs = [pl.BlockSpec((tm, ka), lambda i: (i, 0)), pl.BlockSpec((tm, kb), lambda i: (i, 0)),
                pl.BlockSpec((ka, d), lambda i: (0, 0)), pl.BlockSpec((kb, d), lambda i: (0, 0)),
                row, g2.mod(l, 2), vec, vec, g2.mod(l, 4), g2.mod(l, 3)]
    args = [a, b, wa, wb, x, grp.modarr, ln_g.reshape(1, d), ln_b.reshape(1, d), grp.modarr, grp.modarr]
    aliases = {}
    if h_all is not None:
        in_specs.append(pl.BlockSpec(memory_space=pl.ANY))
        args.append(h_all)
        aliases = {len(args) - 1: 1}
    return pl.pallas_call(
        _outln_body, grid=(n // tm,),
        in_specs=in_specs,
        out_specs=[row, pl.BlockSpec((tm, d), lambda i: (i + h_row0 // tm, 0))],
        out_shape=[jax.ShapeDtypeStruct((n, d), F32), jax.ShapeDtypeStruct((h_rows, d), F32)],
        input_output_aliases=aliases,
        compiler_params=_cp(("parallel",)), name="out_proj_ln",
    )(*args)


def _comb_body(*refs, has_next):
    if has_next:
        y0_ref, y1_ref, gate_ref, x_ref, gt_ref, g_ref, be_ref, sc_ref, sh_ref, xo_ref, ho_ref = refs
    else:
        y0_ref, y1_ref, gate_ref, x_ref, gt_ref, g_ref, be_ref, xo_ref = refs
    gate = gate_ref[...]
    y = y0_ref[...] * gate[:, 0:1] + y1_ref[...] * gate[:, 1:2]
    xn = _layer_norm(ALPHA * x_ref[...] + (1.0 + gt_ref[...]) * y, g_ref[...], be_ref[...])
    xo_ref[...] = xn
    if has_next:
        ho_ref[...] = (xn * (1.0 + sc_ref[...]) + sh_ref[...]).astype(ho_ref.dtype)


def moe_combine_ln(ys, row0, row1, gate, x, grp, l, ln_g, ln_b, has_next):
    n, d = x.shape
    tm = min(grp.tm, 256)
    g2 = _Group(grp.n, grp.t, tm, grp.modarr, grp.per_row)
    nt = n // tm
    row = pl.BlockSpec((tm, d), lambda i: (i, 0))
    vec = pl.BlockSpec((1, d), lambda i: (0, 0))
    in_specs = [pl.BlockSpec((tm, d), lambda i: (i + row0 // tm, 0)),
                pl.BlockSpec((tm, d), lambda i: (i + row1 // tm, 0)), pl.BlockSpec((tm, 2), lambda i: (i, 0)),
                row, g2.mod(l, 5), vec, vec]
    args = [ys, ys, gate, x, grp.modarr, ln_g.reshape(1, d), ln_b.reshape(1, d)]
    out_specs = [row]
    out_shape = [jax.ShapeDtypeStruct((n, d), F32)]
    if has_next:
        in_specs += [g2.mod(l + 1, 1), g2.mod(l + 1, 0)]
        args += [grp.modarr, grp.modarr]
        out_specs.append(row)
        out_shape.append(jax.ShapeDtypeStruct((n, d), BF16))
    res = pl.pallas_call(
        functools.partial(_comb_body, has_next=has_next), grid=(nt,),
        in_specs=in_specs, out_specs=out_specs, out_shape=out_shape,
        compiler_params=_cp(("parallel",)), name="moe_combine_ln",
    )(*args)
    return (res[0], res[1]) if has_next else (res[0], None)


def _router_body(h_ref, w_ref, b_ref, cnt0_ref, eid_ref, gate_ref, rank_ref, cnt_ref, run_ref):
    i = pl.program_id(0)

    @pl.when(i == 0)
    def _():
        run_ref[...] = cnt0_ref[...]

    tm = h_ref.shape[0]
    logits = _dot3(h_ref[...], w_ref[...]) + b_ref[...]
    lg = logits[:, :LANES]
    le = logits[:, LANES:]
    lane = _iota((tm, LANES), 1).astype(F32)
    neg = jnp.float32(-jnp.inf)
    lgm = jnp.where(lane < N_GROUPS, lg, neg)
    mg = jnp.max(lgm, axis=-1, keepdims=True)
    gidx = jnp.min(jnp.where(lgm == mg, lane, float(LANES)), axis=-1, keepdims=True)
    p_g = 1.0 / jnp.sum(jnp.exp(lgm - mg), axis=-1, keepdims=True)
    lo = gidx * EXPERTS_PER_GROUP
    in_grp = (lane >= lo) & (lane < lo + EXPERTS_PER_GROUP)
    lem = jnp.where(in_grp, le, neg)
    v1 = jnp.max(lem, axis=-1, keepdims=True)
    i1 = jnp.min(jnp.where(lem == v1, lane, float(LANES)), axis=-1, keepdims=True)
    lem2 = jnp.where(lane == i1, neg, lem)
    v2 = jnp.max(lem2, axis=-1, keepdims=True)
    i2 = jnp.min(jnp.where(lem2 == v2, lane, float(LANES)), axis=-1, keepdims=True)
    e2 = jnp.exp(v2 - v1)
    g1 = p_g / (1.0 + e2)
    g2 = p_g * e2 / (1.0 + e2)
    oh1 = jnp.where(lane == i1, 1.0, 0.0)
    oh2 = jnp.where(lane == i2, 1.0, 0.0)
    comb = oh1 + oh2
    tri = jnp.where(_iota((tm, tm), 0) > _iota((tm, tm), 1), 1.0, 0.0).astype(BF16)
    before = jnp.dot(tri, comb.astype(BF16), preferred_element_type=F32) + run_ref[...]
    r1 = jnp.sum(before * oh1, axis=-1, keepdims=True)
    r2 = jnp.sum(before * oh2, axis=-1, keepdims=True)
    run_ref[...] = run_ref[...] + jnp.sum(comb, axis=0, keepdims=True)
    eid_ref[:, 0:1] = i1.astype(I32)
    eid_ref[:, 1:2] = i2.astype(I32)
    gate_ref[:, 0:1] = g1
    gate_ref[:, 1:2] = g2
    rank_ref[:, 0:1] = r1.astype(I32)
    rank_ref[:, 1:2] = r2.astype(I32)
    cnt_ref[...] = run_ref[...]


def router(h, row0, n, wr, br, tm, cnt0):
    d = h.shape[1]
    two = pl.BlockSpec((tm, 2), lambda i: (i, 0))
    one = pl.BlockSpec((1, LANES), lambda i: (0, 0))
    return pl.pallas_call(
        _router_body, grid=(n // tm,),
        in_specs=[pl.BlockSpec((tm, d), lambda i: (i + row0 // tm, 0)),
                  pl.BlockSpec((d, 2 * LANES), lambda i: (0, 0)),
                  pl.BlockSpec((1, 2 * LANES), lambda i: (0, 0)), one],
        out_specs=[two, two, two, one],
        out_shape=[jax.ShapeDtypeStruct((n, 2), I32), jax.ShapeDtypeStruct((n, 2), F32),
                   jax.ShapeDtypeStruct((n, 2), I32), jax.ShapeDtypeStruct((1, LANES), F32)],
        scratch_shapes=[pltpu.VMEM((1, LANES), F32)],
        compiler_params=_cp(("arbitrary",)), name="router",
    )(h, wr, br, cnt0)


def moe_plan(eid, rank, cnt, dst_of_entry, spare_row0):
    nk = 2 * eid.shape[0]
    n_blocks = -(-nk // MOE_ROWS) + N_EXPERTS
    n_rows = n_blocks * MOE_ROWS
    counts = cnt[0, :N_EXPERTS].astype(I32)
    padded = (counts + MOE_ROWS - 1) // MOE_ROWS * MOE_ROWS
    pad_end = jnp.cumsum(padded)
    pad_start = pad_end - padded
    dest = (pad_start[eid] + rank).reshape(-1)
    src = jnp.full((n_rows + MOE_ROWS,), -1, I32).at[dest].set(jnp.arange(nk, dtype=I32), unique_indices=True,
                                                               mode='promise_in_bounds')
    valid = src >= 0
    pos = jnp.arange(n_rows + MOE_ROWS, dtype=I32)
    safe = jnp.maximum(src, 0)
    tok = jnp.where(valid, safe >> 1, 0)
    dst = jnp.where(valid, dst_of_entry[safe], spare_row0 + pos % MOE_ROWS)
    dst = jnp.concatenate([spare_row0 + jnp.arange(MOE_ROWS, dtype=I32), dst])
    n_used = pad_end[-1] // MOE_ROWS
    blk = jnp.arange(n_blocks, dtype=I32)
    be = jnp.minimum(jnp.sum((blk * MOE_ROWS)[:, None] >= pad_end[None, :], axis=-1), N_EXPERTS - 1).astype(I32)
    be = jnp.where(blk < n_used, be, be[jnp.maximum(n_used - 1, 0)])
    return tok, dst, be, n_used.reshape(1).astype(I32), n_blocks


def _moe_body(tok_ref, dst_ref, be_ref, nu_ref, h_hbm, w1_ref, w3_ref, w2_ref, ys_hbm,
              xbuf, ybuf, w1b, w3b, w2b, gsem, ssem, *, spare_row0):
    b = pl.program_id(0)
    n_used = nu_ref[0]
    slot = b % 2

    def gather_rows(blk, sl):
        for i in range(MOE_ROWS):
            pltpu.make_async_copy(h_hbm.at[pl.ds(tok_ref[blk * MOE_ROWS + i], 1)], xbuf.at[sl, pl.ds(i, 1)],
                                  gsem.at[sl]).start(priority=i % 2)

    def wait_gather(sl):
        pltpu.make_async_copy(xbuf.at[sl], xbuf.at[sl], gsem.at[sl]).wait()

    def scatter_rows(blk, sl):
        for i in range(MOE_ROWS):
            pltpu.make_async_copy(ybuf.at[sl, pl.ds(i, 1)],
                                  ys_hbm.at[pl.ds(dst_ref[(blk + 1) * MOE_ROWS + i], 1)], ssem.at[sl]).start(priority=i % 2)

    def wait_scatter(sl):
        pltpu.make_async_copy(ybuf.at[sl], ybuf.at[sl], ssem.at[sl]).wait()

    @pl.when(b == 0)
    def _():
        ybuf[...] = jnp.zeros_like(ybuf)
        gather_rows(0, 0)

    @pl.when(b < n_used)
    def _():
        @pl.when(jnp.logical_or(b == 0, be_ref[b] != be_ref[jnp.maximum(b - 1, 0)]))
        def _():
            w1b[...] = w1_ref[...].astype(BF16)
            w3b[...] = w3_ref[...].astype(BF16)
            w2b[...] = w2_ref[...].astype(BF16)

        wait_gather(slot)
        x = xbuf[slot].astype(BF16)
        h1 = jnp.dot(x, w1b[...], preferred_element_type=F32)
        h3 = jnp.dot(x, w3b[...], preferred_element_type=F32)
        mid = (_silu(h1) * h3).astype(BF16)
        y = jnp.dot(mid, w2b[...], preferred_element_type=F32)
        gather_rows(b + 1, 1 - slot)
        scatter_rows(b - 1, 1 - slot)

        @pl.when(b >= 1)
        def _():
            wait_scatter(slot)

        ybuf[slot] = y

        @pl.when(b == n_used - 1)
        def _():
            scatter_rows(b, slot)
            wait_gather(1 - slot)
            wait_scatter(1 - slot)
            wait_scatter(slot)


def moe_ffn(h, tok, dst, be, n_used, n_blocks, w1, w3, w2, layer):
    n, d = h.shape
    de = w1.shape[-1]
    wspec_in = pl.BlockSpec((None, None, d, de), lambda b, tok, dst, be, nu: (layer, be[b], 0, 0))
    wspec_out = pl.BlockSpec((None, None, de, d), lambda b, tok, dst, be, nu: (layer, be[b], 0, 0))
    return pl.pallas_call(
        functools.partial(_moe_body, spare_row0=2 * n),
        grid_spec=pltpu.PrefetchScalarGridSpec(
            num_scalar_prefetch=4, grid=(n_blocks,),
            in_specs=[pl.BlockSpec(memory_space=pl.ANY), wspec_in, wspec_in, wspec_out],
            out_specs=pl.BlockSpec(memory_space=pl.ANY),
            scratch_shapes=[pltpu.VMEM((2, MOE_ROWS, d), F32), pltpu.VMEM((2, MOE_ROWS, d), F32),
                            pltpu.VMEM((d, de), BF16), pltpu.VMEM((d, de), BF16), pltpu.VMEM((de, d), BF16),
                            pltpu.SemaphoreType.DMA((2,)), pltpu.SemaphoreType.DMA((2,))]),
        out_shape=jax.ShapeDtypeStruct((2 * n + MOE_ROWS, d), F32),
        compiler_params=_cp(("arbitrary",)), name="moe_ffn",
    )(tok, dst, be, n_used, h, w1, w3, w2)


def _conv_body(va_ref, bg_ref, cg_ref, w_ref, buf_ref, o_ref, nb_ref):
    u = cg_ref[...] * va_ref[...]
    t = u.shape[0]
    row = _iota(u.shape, 0)
    b0 = buf_ref[0:1, :]
    b1 = buf_ref[1:2, :]
    u1 = jnp.where(row == 0, b1, pltpu.roll(u, 1, axis=0))
    u2 = jnp.where(row == 0, b0, jnp.where(row == 1, b1, pltpu.roll(u, 2, axis=0)))
    w = w_ref[...]
    y = w[0:1] * u2 + w[1:2] * u1 + w[2:3] * u
    o_ref[...] = (bg_ref[...] * y).astype(o_ref.dtype)
    nb_ref[...] = u[t - 2:t, :]


def conv_seq(z3, conv_w, buf, tc=256):
    bsz, t, _ = z3.shape
    nc = D_CONV // tc
    col = lambda off: pl.BlockSpec((None, t, tc), lambda b, j: (b, 0, off + j))
    return pl.pallas_call(
        _conv_body, grid=(bsz, nc),
        in_specs=[col(0), col(nc), col(2 * nc),
                  pl.BlockSpec((CONV_WIDTH, tc), lambda b, j: (0, j)),
                  pl.BlockSpec((None, 2, tc), lambda b, j: (b, 0, j))],
        out_specs=[pl.BlockSpec((None, t, tc), lambda b, j: (b, 0, j)),
                   pl.BlockSpec((None, 2, tc), lambda b, j: (b, 0, j))],
        out_shape=[jax.ShapeDtypeStruct((bsz, t, D_CONV), BF16), jax.ShapeDtypeStruct((bsz, 2, D_CONV), F32)],
        compiler_params=_cp(("parallel", "parallel")), name="conv_seq",
    )(z3, z3, z3, conv_w, buf)


def _conv_step_body(va_ref, bg_ref, cg_ref, w_ref, b0_ref, b1_ref, o_ref, u_ref):
    u = cg_ref[...] * va_ref[...]
    w = w_ref[...]
    y = w[0:1] * b0_ref[...] + w[1:2] * b1_ref[...] + w[2:3] * u
    o_ref[...] = (bg_ref[...] * y).astype(o_ref.dtype)
    u_ref[...] = u


def conv_step(z2, conv_w, buf):
    bsz = z2.shape[0]
    col = lambda j: pl.BlockSpec((bsz, D_CONV), lambda i: (0, j))
    full = pl.BlockSpec((bsz, D_CONV), lambda i: (0, 0))
    a, u = pl.pallas_call(
        _conv_step_body, grid=(1,),
        in_specs=[col(0), col(1), col(2), pl.BlockSpec((CONV_WIDTH, D_CONV), lambda i: (0, 0)), full, full],
        out_specs=[full, full],
        out_shape=[jax.ShapeDtypeStruct((bsz, D_CONV), BF16), jax.ShapeDtypeStruct((bsz, D_CONV), F32)],
        compiler_params=_cp(("arbitrary",)), name="conv_step",
    )(z2, z2, z2, conv_w, buf[:, 0], buf[:, 1])
    return a, jnp.stack([buf[:, 1], u], axis=1)


def _hgrn_body(q_ref, f_ref, i_ref, g_ref, lb_ref, nw_ref, s0_ref, o_ref, so_ref, st_ref,
               *, chunk, sub, nchunk, nhead, t_real, layer, indep):
    tstep = pl.program_id(2)
    tb = chunk * nchunk
    dk = HGRN_DK

    if not indep:
        @pl.when(tstep == 0)
        def _():
            for hd in range(nhead):
                st_ref[hd] = s0_ref[hd].T

    lbl = lb_ref[...]
    e = jnp.exp(lbl - jnp.max(lbl, axis=0, keepdims=True))
    sm = e / jnp.sum(e, axis=0, keepdims=True)
    lb = jnp.zeros((1, lbl.shape[1]), F32)
    for r in range(1, layer + 1):
        lb = lb + sm[r:r + 1]
    tri = jnp.where(_iota((chunk, chunk), 0) >= _iota((chunk, chunk), 1), 1.0, 0.0).astype(BF16)
    nw = nw_ref[...]
    neg = jnp.float32(-jnp.inf)

    fr = f_ref[...]
    v = i_ref[...]
    logf = jnp.log(lb + (1.0 - lb) * _sigmoid(fr))
    kin = (1.0 - lb) * _sigmoid(-fr)
    if t_real is not None:
        row = _iota(fr.shape, 0)
        live = ((row & (chunk - 1)) if indep else (tstep * tb + row)) < t_real
        logf = jnp.where(live, logf, 0.0)
        kin = jnp.where(live, kin, 0.0)
    qs = _silu(q_ref[...])
    causal = _iota((sub, sub, 1), 1) <= _iota((sub, sub, 1), 0)
    cuts = [(slice(c * chunk, (c + 1) * chunk), slice(hd * dk, (hd + 1) * dk))
            for c in range(nchunk) for hd in range(nhead)]
    bb_all = [_dot_exact_lhs(tri, logf[c * chunk:(c + 1) * chunk]) for c in range(nchunk)]
    bb_l = [bb_all[c][:, hd * dk:(hd + 1) * dk] for c in range(nchunk) for hd in range(nhead)]
    qs_l = [qs[rows, ls] for rows, ls in cuts]
    kin_l = [kin[rows, ls] for rows, ls in cuts]
    v_l = [v[rows, ls] for rows, ls in cuts]
    bl_l = [bb[chunk - 1:chunk] for bb in bb_l]
    kv_l = [_dot(vc, kc * jnp.exp(bl - bb), TN) for vc, kc, bl, bb in zip(v_l, kin_l, bl_l, bb_l)]
    intra_l = []
    for qc, kc, vc, bb in zip(qs_l, kin_l, v_l, bb_l):
        parts = []
        for blk in range(chunk // sub):
            lo = blk * sub
            qi, bi, ki, vi = qc[lo:lo + sub], bb[lo:lo + sub], kc[lo:lo + sub], vc[lo:lo + sub]
            d = bi[:, None, :] - bi[None, :, :]
            sc = jnp.sum(qi[:, None, :] * ki[None, :, :] * jnp.exp(jnp.where(causal, d, neg)), axis=-1)
            oi = _dot(sc, vi)
            if blk > 0:
                anchor = bb[lo - 1:lo]
                qt = qi * jnp.exp(bi - anchor)
                kt = kc[:lo] * jnp.exp(anchor - bb[:lo])
                oi = oi + _dot(_dot(qt, kt, NT), vc[:lo])
            parts.append(oi)
        intra_l.append(parts[0] if len(parts) == 1 else jnp.concatenate(parts, axis=0))
    if indep:
        st_l = [s0_ref[c, hd].T for c in range(nchunk) for hd in range(nhead)]
        for idx, (st, bl, kv) in enumerate(zip(st_l, bl_l, kv_l)):
            so_ref[idx // nhead, idx % nhead] = (st * jnp.exp(bl) + kv).T
    else:
        st_l = []
        cur = [st_ref[hd] for hd in range(nhead)]
        for c in range(nchunk):
            for hd in range(nhead):
                idx = c * nhead + hd
                st_l.append(cur[hd])
                cur[hd] = cur[hd] * jnp.exp(bl_l[idx]) + kv_l[idx]
        for hd in range(nhead):
            st_ref[hd] = cur[hd]
    o_l = [_dot(qc * jnp.exp(bb), sc, NT) + oi for qc, bb, sc, oi in zip(qs_l, bb_l, st_l, intra_l)]
    o_l = [o * lax.rsqrt(jnp.mean(o * o, axis=-1, keepdims=True) + RMS_EPS) * nw for o in o_l]
    rows_l = [o_l[c * nhead] if nhead == 1 else jnp.concatenate(o_l[c * nhead:(c + 1) * nhead], axis=1)
              for c in range(nchunk)]
    o = rows_l[0] if nchunk == 1 else jnp.concatenate(rows_l, axis=0)
    o_ref[...] = (o * _silu(g_ref[...])).astype(o_ref.dtype)

    if not indep:
        @pl.when(tstep == pl.num_programs(2) - 1)
        def _():
            for hd in range(nhead):
                so_ref[hd] = st_ref[hd].T


def hgrn_seq(z3, hgrn_lb, norm_w, s0, layer, chunk, nchunk, nhead=1, t_real=None, indep=False):
    bsz, t, _ = z3.shape
    tb = chunk * nchunk
    sb = None
    if indep:
        sb = nchunk
        z3 = z3.reshape(bsz // nchunk, tb, z3.shape[2])
        bsz, t = bsz // nchunk, tb
    wl = nhead * LANES
    groups = HGRN_HEADS // nhead
    qoff = 3 * D_CONV // wl
    col = lambda k: pl.BlockSpec((None, tb, wl), lambda b, h, s: (b, s, qoff + k * groups + h))
    st = pl.BlockSpec((sb, nhead, HGRN_DK, HGRN_DV), lambda b, h, s: (b, h, 0, 0))
    body = functools.partial(_hgrn_body, chunk=chunk, sub=min(HGRN_SUB, chunk), nchunk=nchunk, nhead=nhead, t_real=t_real,
                             layer=layer, indep=indep)
    nseq = bsz * (sb or 1)
    o, so = pl.pallas_call(
        body, grid=(bsz, groups, t // tb),
        in_specs=[col(0), col(1), col(2), col(3),
                  pl.BlockSpec((N_EVEN, wl), lambda b, h, s: (0, h)),
                  pl.BlockSpec((1, HGRN_DV), lambda b, h, s: (0, 0)), st],
        out_specs=[pl.BlockSpec((None, tb, wl), lambda b, h, s: (b, s, h)), st],
        out_shape=[jax.ShapeDtypeStruct((bsz, t, D_HV), BF16),
                   jax.ShapeDtypeStruct((nseq, HGRN_HEADS, HGRN_DK, HGRN_DV), F32)],
        scratch_shapes=[pltpu.VMEM((nhead, HGRN_DV, HGRN_DK), F32)],
        compiler_params=_cp(("parallel", "parallel", "arbitrary")), name="hgrn_seq",
    )(z3, z3, z3, z3, hgrn_lb, norm_w.reshape(1, HGRN_DV), s0)
    return o.reshape(nseq, chunk if indep else t, D_HV), so


def _attn_body(sink_ref, zq_ref, zkv_ref, cos_ref, sin_ref, ck_ref, cv_ref, o_ref, nk_ref, nv_ref, kp_ref, vp_ref,
               *, prev_valid, t_real, tq):
    i = pl.program_id(1)
    w = WINDOW

    @pl.when(i == 0)
    def _():
        kp_ref[...] = ck_ref[...]
        vp_ref[...] = cv_ref[...]

    cos = cos_ref[...]
    sin = sin_ref[...]

    def rope(x):
        width = x.shape[1]
        reps = width // LANES
        first = (_iota(x.shape, 1) & (HEAD_DIM - 1)) < (HEAD_DIM // 2)
        rot = jnp.where(first, pltpu.roll(x, width - HEAD_DIM // 2, axis=1), pltpu.roll(x, HEAD_DIM // 2, axis=1))
        return x * jnp.tile(cos, (1, reps)) + rot * jnp.tile(sin, (1, reps))

    kv = zkv_ref[...]
    qr = rope(zq_ref[...]) * (HEAD_DIM ** -0.5)
    kr = rope(kv[:, :D_KV])
    v = kv[:, D_KV:]
    kprev = kp_ref[...]
    vprev = vp_ref[...]

    grp = ATTN_HEADS // ATTN_KV_HEADS
    rows = _iota((grp * tq, w + tq), 0)
    cols = _iota((grp * tq, w + tq), 1)
    delta = (rows & (tq - 1)) + w - cols
    valid = (delta >= 0) & (delta <= w)
    if not prev_valid:
        valid = valid & (cols >= jnp.where(i > 0, 0, w))
    head_of_row = _iota((grp * tq, 1), 0) >> int(math.log2(tq))
    neg = jnp.float32(-jnp.inf)
    outs = []
    for g in range(ATTN_KV_HEADS):
        ls = slice(g * HEAD_DIM, (g + 1) * HEAD_DIM)
        kg = jnp.concatenate([kprev[:, ls], kr[:, ls]], axis=0)
        vg = jnp.concatenate([vprev[:, ls], v[:, ls]], axis=0)
        qg = jnp.concatenate([qr[:, (grp * g + hh) * HEAD_DIM:(grp * g + hh + 1) * HEAD_DIM] for hh in range(grp)],
                             axis=0)
        s = jnp.where(valid, _dot(qg, kg, NT), neg)
        sink = jnp.zeros((grp * tq, 1), F32)
        for hh in range(grp):
            sink = jnp.where(head_of_row == hh, sink_ref[grp * g + hh], sink)
        m = jnp.maximum(jnp.max(s, axis=-1, keepdims=True), sink)
        p = jnp.exp(s - m)
        p = p / (jnp.sum(p, axis=-1, keepdims=True) + jnp.exp(sink - m))
        og = _dot(p, vg)
        outs += [og[hh * tq:(hh + 1) * tq] for hh in range(grp)]
    o_ref[...] = jnp.concatenate(outs, axis=1).astype(o_ref.dtype)
    if t_real == w:
        kp_ref[...] = kr
        vp_ref[...] = v

    @pl.when(i == pl.num_programs(1) - 1)
    def _():
        if t_real == w:
            nk_ref[...] = kr
            nv_ref[...] = v
        else:
            last = _iota(kprev.shape, 0) == w - 1
            nk_ref[...] = jnp.where(last, kr[0:1], pltpu.roll(kprev, w - 1, axis=0))
            nv_ref[...] = jnp.where(last, v[0:1], pltpu.roll(vprev, w - 1, axis=0))


def attn_seq(z3, sinks, cos, sin, cache_k, cache_v, prev_valid, t_real, tq=WINDOW):
    bsz, t, _ = z3.shape
    nb = t // tq
    assert tq == WINDOW or nb == 1
    cache = pl.BlockSpec((None, WINDOW, D_KV), lambda b, i: (b, 0, 0))
    tab = pl.BlockSpec((tq, LANES), lambda b, i: (i, 0))
    body = functools.partial(_attn_body, prev_valid=prev_valid, t_real=t_real, tq=tq)
    return pl.pallas_call(
        body, grid=(bsz, nb),
        in_specs=[pl.BlockSpec(memory_space=pltpu.SMEM),
                  pl.BlockSpec((None, tq, D_Q), lambda b, i: (b, i, 0)),
                  pl.BlockSpec((None, tq, 2 * D_KV), lambda b, i: (b, i, D_Q // (2 * D_KV))),
                  tab, tab, cache, cache],
        out_specs=[pl.BlockSpec((None, tq, D_Q), lambda b, i: (b, i, 0)), cache, cache],
        out_shape=[jax.ShapeDtypeStruct((bsz, t, D_Q), BF16),
                   jax.ShapeDtypeStruct((bsz, WINDOW, D_KV), F32), jax.ShapeDtypeStruct((bsz, WINDOW, D_KV), F32)],
        scratch_shapes=[pltpu.VMEM((WINDOW, D_KV), F32), pltpu.VMEM((WINDOW, D_KV), F32)],
        compiler_params=_cp(("parallel", "arbitrary")), name="attn_seq",
    )(sinks, z3, z3, cos, sin, cache_k, cache_v)


def rope_tables(pos):
    half = HEAD_DIM // 2
    inv = jnp.exp(-math.log(ROPE_THETA) * jnp.arange(half, dtype=F32) / half)
    ang = pos.astype(F32)[:, None] * inv[None, :]
    c, s = jnp.cos(ang), jnp.sin(ang)
    cos = jnp.concatenate([c, c, c, c], axis=1)
    sin = jnp.concatenate([-s, s, -s, s], axis=1)
    return cos, sin


def _rwkv_body(r_ref, k_ref, v_ref, wa_ref, gd_ref, sr_ref, sk_ref, sv_ref, swa_ref, sgd_ref,
               mr_ref, mk_ref, mv_ref, mwa_ref, mgd_ref, w0_ref, w2_ref, a0_ref, a2_ref, g2_ref,
               kkp_ref, ka_ref, rk_ref, lg_ref, lbias_ref, s0_ref, o_ref, so_ref, st_ref, prev_ref, prevw_ref,
               *, chunk, nchunk, npair, t_real, indep):
    tstep = pl.program_id(2)
    n = RWKV_N
    ln = chunk
    tb = chunk * nchunk

    if not indep:
        @pl.when(tstep == 0)
        def _():
            st_ref[...] = s0_ref[...]
            prev_ref[0:1, :] = sr_ref[...]
            prev_ref[1:2, :] = sk_ref[...]
            prev_ref[2:3, :] = sv_ref[...]
            prevw_ref[0:1, :] = swa_ref[...]
            prevw_ref[1:2, :] = sgd_ref[...]

    def mix(x_ref, mu_ref, s_ref, p_ref, idx):
        x = x_ref[...]
        row = _iota(x.shape, 0)
        if indep:
            before = jnp.broadcast_to(s_ref[...], (nchunk, ln, x.shape[1])).reshape(tb, x.shape[1])
            shifted = jnp.where((row & (ln - 1)) == 0, before, pltpu.roll(x, 1, axis=0))
        else:
            shifted = jnp.where(row == 0, p_ref[idx:idx + 1, :], pltpu.roll(x, 1, axis=0))
            p_ref[idx:idx + 1, :] = x[tb - 1:tb, :]
        return x + mu_ref[...] * (shifted - x)

    r = mix(r_ref, mr_ref, sr_ref, prev_ref, 0)
    kr = mix(k_ref, mk_ref, sk_ref, prev_ref, 1)
    vr = mix(v_ref, mv_ref, sv_ref, prev_ref, 2)
    wa = mix(wa_ref, mwa_ref, swa_ref, prevw_ref, 0)
    gd = mix(gd_ref, mgd_ref, sgd_ref, prevw_ref, 1)[:, :RWKV_RANK]
    wd = wa[:, :RWKV_RANK]
    ad = wa[:, RWKV_RANK:]

    w_log = -_softplus(-(w0_ref[...] + _dot3(jnp.tanh(wd), w2_ref[...]))) - 0.5
    logw = -jnp.exp(w_log)
    a = _sigmoid(a0_ref[...] + _dot3(ad, a2_ref[...]))
    gate = _dot3(_sigmoid(gd), g2_ref[...])

    sh = int(math.log2(n))
    seg = jnp.where((_iota((LANES, LANES), 0) >> sh) == (_iota((LANES, LANES), 1) >> sh), 1.0, 0.0).astype(BF16)

    def segsum(x):
        tiles = [_dot_exact_rhs(x[:, i * LANES:(i + 1) * LANES], seg) for i in range(npair)]
        return tiles[0] if npair == 1 else jnp.concatenate(tiles, axis=1)

    kkv = kr * kkp_ref[...]
    kk = kkv / jnp.maximum(jnp.sqrt(segsum(kkv * kkv)), 1e-12)
    kf = kr * (1.0 + (a - 1.0) * ka_ref[...])
    bonus = segsum(r * kf * rk_ref[...]) * vr
    if t_real is not None:
        row = _iota(logw.shape, 0)
        live = ((row & (ln - 1)) if indep else (tstep * tb + row)) < t_real
        zero = jnp.zeros_like(logw)
        logw = jnp.where(live, logw, zero)
        kk = jnp.where(live, kk, zero)
        kf = jnp.where(live, kf, zero)
        vr_s = jnp.where(live, vr, zero)
    else:
        vr_s = vr

    tri = jnp.where(_iota((ln, ln), 0) >= _iota((ln, ln), 1), 1.0, 0.0).astype(BF16)
    rr = _iota((ln, ln), 0)
    cc = _iota((ln, ln), 1)
    strict = rr > cc
    incl = rr >= cc
    eye = jnp.where(rr == cc, 1.0, 0.0)
    eye_n = jnp.where(_iota((n, n), 0) == _iota((n, n), 1), 1.0, 0.0)

    nhead = 2 * npair
    am_l, rm_l, bp_l, kp_l, bl_l, kl_l, v_l, gl_l = [], [], [], [], [], [], [], []
    for ci in range(nchunk):
        rows = slice(ci * ln, (ci + 1) * ln)
        lw = logw[rows]
        c = _dot_exact_lhs(tri, lw)
        gam = jnp.exp(c)
        ginv = jnp.exp(-c)
        am = -kk[rows] * jnp.exp(c - lw)
        rm = r[rows] * gam
        bp = kk[rows] * a[rows] * ginv
        kp = kf[rows] * ginv
        g_last = gam[ln - 1:ln, :]
        bpl = bp * g_last
        kpl = kp * g_last
        vv = vr_s[rows]
        for hd in range(nhead):
            ls = slice(hd * n, (hd + 1) * n)
            am_l.append(am[:, ls]); rm_l.append(rm[:, ls]); bp_l.append(bp[:, ls]); kp_l.append(kp[:, ls])
            bl_l.append(bpl[:, ls]); kl_l.append(kpl[:, ls]); v_l.append(vv[:, ls]); gl_l.append(g_last[:, ls])
    each = lambda fn, *ls: [fn(*xs) for xs in zip(*ls)]
    pw_l = each(lambda am, rm, bp, kp: _dot3(jnp.concatenate([am, rm], axis=0),
                                             jnp.concatenate([bp, kp], axis=0), NT), am_l, rm_l, bp_l, kp_l)
    m_l = each(lambda pw: jnp.where(strict, pw[:ln, :ln], 0.0), pw_l)
    nm_l = each(lambda pw: jnp.where(strict, pw[:ln, ln:], 0.0), pw_l)
    qb_l = each(lambda pw: jnp.where(incl, pw[ln:, :ln], 0.0), pw_l)
    qk_l = each(lambda pw: jnp.where(incl, pw[ln:, ln:], 0.0), pw_l)
    nq_l = each(lambda nm, qk, v: _dot(jnp.concatenate([nm, qk], axis=0), v), nm_l, qk_l, v_l)
    nv_l = [x[:ln] for x in nq_l]
    qkv_l = [x[ln:] for x in nq_l]
    kv_l = each(lambda v, kl: _dot(v, kl, TN), v_l, kl_l)
    tinv_l = each(lambda m: eye + m, m_l)
    p_l = m_l
    for _ in range(int(math.log2(ln)) - 1):
        p_l = each(lambda p: _dot(p, p), p_l)
        tinv_l = each(lambda t, p: t + _dot(t, p), tinv_l, p_l)
    wu_l = each(lambda t, am, nv: _dot(t, jnp.concatenate([am, nv], axis=1)), tinv_l, am_l, nv_l)
    ac_l = each(lambda wu, bl: _dot(wu, bl, TN), wu_l, bl_l)
    a_l = each(lambda ac, gl: eye_n * gl + ac[:n], ac_l, gl_l)
    c_l = each(lambda ac, kv: ac[n:] + kv, ac_l, kv_l)
    ro_l = each(_dot, qb_l, wu_l)
    rt_l = each(lambda rm, ro: rm + ro[:, :n], rm_l, ro_l)
    o0_l = each(lambda ro, qkv: ro[:, n:] + qkv, ro_l, qkv_l)

    o_rows = []
    if indep:
        s_l = [s0_ref[ci, hd] for ci in range(nchunk) for hd in range(nhead)]
        o_all = each(lambda rt, s, o0: _dot(rt, s, NT) + o0, rt_l, s_l, o0_l)
        s_l = each(lambda s, am, cm: _dot(s, am) + cm, s_l, a_l, c_l)
        for ci in range(nchunk):
            o_rows.append(o_all[ci * nhead:(ci + 1) * nhead])
            for hd in range(nhead):
                so_ref[ci, hd] = s_l[ci * nhead + hd]
    else:
        s_l = [st_ref[hd] for hd in range(nhead)]
        for ci in range(nchunk):
            sl = slice(ci * nhead, (ci + 1) * nhead)
            o_rows.append(each(lambda rt, s, o0: _dot(rt, s, NT) + o0, rt_l[sl], s_l, o0_l[sl]))
            s_l = each(lambda s, am, cm: _dot(s, am) + cm, s_l, a_l[sl], c_l[sl])
        for hd in range(nhead):
            st_ref[hd] = s_l[hd]
    cols = [o_rows[0][hd] if nchunk == 1 else jnp.concatenate([o_rows[ci][hd] for ci in range(nchunk)], axis=0)
            for hd in range(nhead)]
    o = jnp.concatenate(cols, axis=1)
    mu_o = segsum(o) * (1.0 / n)
    dlt = o - mu_o
    var_o = segsum(dlt * dlt) * (1.0 / n)
    o = dlt * lax.rsqrt(var_o + RWKV_GN_EPS) * lg_ref[...] + lbias_ref[...]
    o_ref[...] = ((o + bonus) * gate).astype(o_ref.dtype)

    if not indep:
        @pl.when(tstep == pl.num_programs(2) - 1)
        def _():
            so_ref[...] = st_ref[...]


def rwkv_seq(z3, shift, P, j, s0, chunk, nchunk=1, npair=1, t_real=None, indep=False):
    bsz, t, _ = z3.shape
    tb = chunk * nchunk
    sb = None
    if indep:
        sb = nchunk
        z3 = z3.reshape(bsz // nchunk, tb, z3.shape[2])
        bsz, t = bsz // nchunk, tb
    wl = npair * LANES
    groups = D_RWKV // wl
    zoff = (D_Q + 2 * D_KV) // LANES
    nb = D_RWKV // LANES
    wa_blk = 3 * nb
    gd_blk = 3 * nb + 1
    zc = lambda off: pl.BlockSpec((None, tb, wl), lambda b, p, s: (b, s, (zoff + off) // npair + p))
    zw = lambda blk: pl.BlockSpec((None, tb, LANES), lambda b, p, s: (b, s, zoff + blk))
    sc = lambda off: pl.BlockSpec((sb, 1, wl), lambda b, p, s: (b, 0, off // npair + p))
    sw = lambda blk: pl.BlockSpec((sb, 1, LANES), lambda b, p, s: (b, 0, blk))
    mc = lambda off: pl.BlockSpec((1, wl), lambda b, p, s: (0, off // npair + p))
    mw = lambda blk: pl.BlockSpec((1, LANES), lambda b, p, s: (0, blk))
    vec = pl.BlockSpec((1, wl), lambda b, p, s: (0, p))
    lora = pl.BlockSpec((RWKV_RANK, wl), lambda b, p, s: (0, p))
    st = pl.BlockSpec((sb, 2 * npair, RWKV_N, RWKV_N), lambda b, p, s: (b, p, 0, 0))
    mu = jnp.pad(P['rwkv_mu'][j], (0, D_SHIFT_PAD - D_SHIFT)).reshape(1, D_SHIFT_PAD)
    row = lambda x: x.reshape(1, D_RWKV)
    body = functools.partial(_rwkv_body, chunk=chunk, nchunk=nchunk, npair=npair, t_real=t_real, indep=indep)
    nseq = bsz * (sb or 1)
    o, so = pl.pallas_call(
        body, grid=(bsz, groups, t // tb),
        in_specs=[zc(0), zc(nb), zc(2 * nb), zw(wa_blk), zw(gd_blk),
                  sc(0), sc(nb), sc(2 * nb), sw(wa_blk), sw(gd_blk),
                  mc(0), mc(nb), mc(2 * nb), mw(wa_blk), mw(gd_blk),
                  vec, lora, vec, lora, lora, vec, vec, vec, vec, vec, st],
        out_specs=[pl.BlockSpec((None, tb, wl), lambda b, p, s: (b, s, p)), st],
        out_shape=[jax.ShapeDtypeStruct((bsz, t, D_RWKV), BF16),
                   jax.ShapeDtypeStruct((nseq, RWKV_HEADS, RWKV_N, RWKV_N), F32)],
        scratch_shapes=[pltpu.VMEM((2 * npair, RWKV_N, RWKV_N), F32), pltpu.VMEM((8, wl), F32),
                        pltpu.VMEM((8, LANES), F32)],
        compiler_params=_cp(("parallel", "parallel", "arbitrary")), name="rwkv_seq",
    )(z3, z3, z3, z3, z3, shift, shift, shift, shift, shift, mu, mu, mu, mu, mu,
      row(P['rwkv_w0'][j]), P['rwkv_w2'][j], row(P['rwkv_a0'][j]), P['rwkv_a2'][j], P['rwkv_g2'][j],
      row(P['rwkv_kk'][j]), row(P['rwkv_ka'][j]), row(P['rwkv_rk'][j]), row(P['rwkv_lnx_g'][j]),
      row(P['rwkv_lnx_b'][j]), s0)
    return o.reshape(nseq, chunk if indep else t, D_RWKV), so


def _pad_time(z2, tp):
    return jnp.pad(z2[:, None, :], ((0, 0), (0, tp - 1), (0, 0)))


def _mix_layer(l, h, x, grp, pos0, W, P, st, single):
    n = x.shape[0]
    t = grp.t
    bsz = n // t
    tm = grp.tm
    j = l // 2
    step_pad = 8
    new = {}
    if l % 2 == 0:
        z = matmul(h, W['w_in_even'][j], tm, D_IN_EVEN // 4)
        if single:
            a_out, new['conv'] = conv_step(z, P['conv_w'][j], st['conv'][j])
            b3, new['hgrn'] = hgrn_seq(_pad_time(z, step_pad), P['hgrn_lb'], P['hgrn_norm'][j], st['hgrn'][j], j,
                                       chunk=step_pad, nchunk=4, nhead=4, t_real=1, indep=True)
            b_out = b3[:, 0]
        else:
            z3 = z.reshape(bsz, t, D_IN_EVEN)
            a3, new['conv'] = conv_seq(z3, P['conv_w'][j], st['conv'][j])
            b3, new['hgrn'] = hgrn_seq(z3, P['hgrn_lb'], P['hgrn_norm'][j], st['hgrn'][j], j, chunk=64, nchunk=8)
            a_out, b_out = a3.reshape(n, D_CONV), b3.reshape(n, D_HV)
        wa, wb = W['w_out_even'][j][:D_CONV], W['w_out_even'][j][D_CONV:]
    else:
        z = matmul(h, W['w_in_odd'][j], tm, D_IN_ODD_PAD // 2)
        shift_in = jnp.pad(st['shift'][j], ((0, 0), (0, D_SHIFT_PAD - D_SHIFT)))[:, None, :]
        kc = st['k'][j].reshape(bsz, WINDOW, D_KV)
        vc = st['v'][j].reshape(bsz, WINDOW, D_KV)
        if single:
            cos, sin = rope_tables(pos0 + jnp.arange(step_pad, dtype=I32))
            a3, nk, nv = attn_seq(_pad_time(z[:, :D_Q + 2 * D_KV], step_pad), P['attn_sinks'][j], cos, sin,
                                  kc, vc, prev_valid=True, t_real=1, tq=step_pad)
            b3, new['rwkv'] = rwkv_seq(_pad_time(z, step_pad), shift_in, P, j, st['rwkv'][j], chunk=step_pad,
                                       nchunk=4, npair=4, t_real=1, indep=True)
            a_out, b_out = a3[:, 0], b3[:, 0]
            new['shift'] = z[:, D_Q + 2 * D_KV:D_IN_ODD]
        else:
            z3 = z.reshape(bsz, t, D_IN_ODD_PAD)
            cos, sin = rope_tables(pos0 + jnp.arange(t, dtype=I32))
            a3, nk, nv = attn_seq(z3, P['attn_sinks'][j], cos, sin, kc, vc, prev_valid=False, t_real=WINDOW)
            b3, new['rwkv'] = rwkv_seq(z3, shift_in, P, j, st['rwkv'][j], chunk=64, nchunk=4, npair=2)
            a_out, b_out = a3.reshape(n, D_Q), b3.reshape(n, D_RWKV)
            new['shift'] = z3[:, t - 1, D_Q + 2 * D_KV:D_IN_ODD]
        new['k'] = nk.reshape(bsz, WINDOW, ATTN_KV_HEADS, HEAD_DIM)
        new['v'] = nv.reshape(bsz, WINDOW, ATTN_KV_HEADS, HEAD_DIM)
        wa, wb = W['w_out_odd'][j][:D_Q], W['w_out_odd'][j][D_Q:]
    return a_out, b_out, wa, wb, new


def kernel(x_prompt, x_sample, c_prompt, c_sample, state_conv, state_hgrn, cache_swa_k, cache_swa_v,
           state_rwkv, state_shift, ada_w, ada_b, ln_g, ln_b, w_in_even, w_out_even, conv_w, hgrn_lb,
           hgrn_norm, w_in_odd, w_out_odd, attn_sinks, rwkv_mu, rwkv_w0, rwkv_w2, rwkv_a0, rwkv_a2,
           rwkv_g2, rwkv_kk, rwkv_ka, rwkv_rk, rwkv_lnx_g, rwkv_lnx_b, moe_w_grp, moe_b_grp, moe_w_exp,
           moe_b_exp, moe_w1, moe_w3, moe_w2):
    P = dict(ln_g=ln_g, ln_b=ln_b, conv_w=conv_w, hgrn_lb=hgrn_lb, hgrn_norm=hgrn_norm, attn_sinks=attn_sinks,
             rwkv_mu=rwkv_mu, rwkv_w0=rwkv_w0, rwkv_w2=rwkv_w2, rwkv_a0=rwkv_a0, rwkv_a2=rwkv_a2,
             rwkv_g2=rwkv_g2, rwkv_kk=rwkv_kk, rwkv_ka=rwkv_ka, rwkv_rk=rwkv_rk.reshape(N_ODD, D_RWKV),
             rwkv_lnx_g=rwkv_lnx_g, rwkv_lnx_b=rwkv_lnx_b)
    bp, tp, d = x_prompt.shape
    bs, ts, _ = x_sample.shape
    n_p, n_s = bp * tp, bs * ts
    router_w = jnp.zeros((DEPTH, d, 2 * LANES), F32)
    router_w = router_w.at[:, :, :N_GROUPS].set(moe_w_grp).at[:, :, LANES:LANES + N_EXPERTS].set(moe_w_exp)
    router_b = jnp.zeros((DEPTH, 1, 2 * LANES), F32)
    router_b = router_b.at[:, 0, :N_GROUPS].set(moe_b_grp).at[:, 0, LANES:LANES + N_EXPERTS].set(moe_b_exp)
    W = dict(w_in_even=w_in_even.astype(BF16), w_out_even=w_out_even.astype(BF16),
             w_in_odd=jnp.pad(w_in_odd.astype(BF16), ((0, 0), (0, 0), (0, D_IN_ODD_PAD - D_IN_ODD))),
             w_out_odd=w_out_odd.astype(BF16))

    mod = ada_mod(jnp.concatenate([c_prompt, c_sample], axis=0), ada_w, ada_b)
    mod = mod.reshape(DEPTH, bp + bs, 6, d).transpose(0, 2, 1, 3)
    grp_p = _Group(n_p, tp, 512, mod[:, :, :bp, None, :], per_row=False)
    grp_s = _Group(n_s, ts, n_s, mod[:, :, bp:], per_row=True)

    zeros = lambda *s: jnp.zeros(s, F32)
    st_p = dict(conv=zeros(N_EVEN, bp, CONV_WIDTH - 1, D_CONV), hgrn=zeros(N_EVEN, bp, HGRN_HEADS, HGRN_DK, HGRN_DV),
                k=zeros(N_ODD, bp, WINDOW, ATTN_KV_HEADS, HEAD_DIM), v=zeros(N_ODD, bp, WINDOW, ATTN_KV_HEADS, HEAD_DIM),
                rwkv=zeros(N_ODD, bp, RWKV_HEADS, RWKV_N, RWKV_N), shift=zeros(N_ODD, bp, D_SHIFT))
    st_s = dict(conv=state_conv, hgrn=state_hgrn, k=cache_swa_k, v=cache_swa_v, rwkv=state_rwkv, shift=state_shift)
    new_p = {k: [] for k in st_p}
    new_s = {k: [] for k in st_s}

    n_all = n_p + n_s
    t_all = jnp.arange(n_all, dtype=I32)
    row_of = lambda k: jnp.where(t_all < n_p, k * n_p + t_all, 2 * n_p + k * n_s + (t_all - n_p))
    dst_of_entry = jnp.stack([row_of(0), row_of(1)], axis=1).reshape(-1)

    x_p, x_s = x_prompt.reshape(n_p, d), x_sample.reshape(n_s, d)
    h_all = zeros(n_all, d)
    h_p, h_s = modulate(x_p, grp_p, 0, 1, 0), modulate(x_s, grp_s, 0, 1, 0)
    for l in range(DEPTH):
        a_p, b_p, wa, wb, np_l = _mix_layer(l, h_p, x_p, grp_p, 0, W, P, st_p, single=False)
        a_s, b_s, _, _, ns_l = _mix_layer(l, h_s, x_s, grp_s, PAST_LEN, W, P, st_s, single=True)
        for k, v in np_l.items():
            new_p[k].append(v)
        for k, v in ns_l.items():
            new_s[k].append(v)
        x_p, h_all = out_proj_ln(a_p, b_p, wa, wb, x_p, grp_p, l, ln_g[l, 0], ln_b[l, 0], n_all, 0, h_all)
        x_s, h_all = out_proj_ln(a_s, b_s, wa, wb, x_s, grp_s, l, ln_g[l, 0], ln_b[l, 0], n_all, n_p, h_all)
        eid_p, gate_p, rank_p, cnt = router(h_all, 0, n_p, router_w[l], router_b[l], 256, zeros(1, LANES))
        eid_s, gate_s, rank_s, cnt = router(h_all, n_p, n_s, router_w[l], router_b[l], n_s, cnt)
        tok, dst, be, n_used, n_blocks = moe_plan(jnp.concatenate([eid_p, eid_s]), jnp.concatenate([rank_p, rank_s]),
                                                  cnt, dst_of_entry, 2 * n_all)
        ys = moe_ffn(h_all, tok, dst, be, n_used, n_blocks, moe_w1, moe_w3, moe_w2, l)
        last = l + 1 == DEPTH
        x_p, h_p = moe_combine_ln(ys, 0, n_p, gate_p, x_p, grp_p, l, ln_g[l, 1], ln_b[l, 1], has_next=not last)
        x_s, h_s = moe_combine_ln(ys, 2 * n_p, 2 * n_p + n_s, gate_s, x_s, grp_s, l, ln_g[l, 1], ln_b[l, 1],
                                  has_next=not last)
    order = ('conv', 'hgrn', 'k', 'v', 'rwkv', 'shift')
    return ((x_p.reshape(bp, tp, d), x_s.reshape(bs, ts, d))
            + tuple(jnp.stack(new_p[k]) for k in order) + tuple(jnp.stack(new_s[k]) for k in order))
```

```python
import functools
import math

import jax
import jax.numpy as jnp
from jax import lax
from jax.experimental import pallas as pl
from jax.experimental.pallas import tpu as pltpu

F32 = jnp.float32
BF16 = jnp.bfloat16
I32 = jnp.int32

D_MODEL = 2048
DEPTH = 4
PAST_LEN = 16384
N_EVEN = (DEPTH + 1) // 2
N_ODD = DEPTH // 2
D_CONV = 1024
CONV_WIDTH = 3
HGRN_HEADS = 8
HGRN_DK = 128
HGRN_DV = 128
D_HK = HGRN_HEADS * HGRN_DK
D_HV = HGRN_HEADS * HGRN_DV
ATTN_HEADS = 16
ATTN_KV_HEADS = 4
HEAD_DIM = 64
WINDOW = 128
ROPE_THETA = 10000.0
D_Q = ATTN_HEADS * HEAD_DIM
D_KV = ATTN_KV_HEADS * HEAD_DIM
RWKV_HEADS = 16
RWKV_N = 64
D_RWKV = RWKV_HEADS * RWKV_N
RWKV_RANK = 64
RWKV_GN_EPS = 64e-5
D_SHIFT = 3 * D_RWKV + 3 * RWKV_RANK
D_IN_EVEN = 3 * D_CONV + 2 * D_HK + 2 * D_HV
D_IN_ODD = D_Q + 2 * D_KV + D_SHIFT
N_GROUPS = 4
EXPERTS_PER_GROUP = 8
N_EXPERTS = N_GROUPS * EXPERTS_PER_GROUP
D_EXPERT = 512
ALPHA = (2 * DEPTH) ** 0.25
LN_EPS = 1e-5
RMS_EPS = 1e-6

LANES = 128
HGRN_SUB = 16
MOE_ROWS = 128
VMEM_LIMIT = 48 * 1024 * 1024

D_IN_ODD_PAD = -(-D_IN_ODD // LANES) * LANES
D_SHIFT_PAD = D_IN_ODD_PAD - (D_Q + 2 * D_KV)

NN = (((1,), (0,)), ((), ()))
NT = (((1,), (1,)), ((), ()))
TN = (((0,), (0,)), ((), ()))


def _cp(sem, vmem=VMEM_LIMIT):
    return pltpu.CompilerParams(dimension_semantics=sem, vmem_limit_bytes=vmem)


def _dot(a, b, dims=NN):
    return lax.dot_general(a.astype(BF16), b.astype(BF16), dims, preferred_element_type=F32)


def _split2(x):
    hi = x.astype(BF16)
    lo = (x - hi.astype(F32)).astype(BF16)
    return hi, lo


def _dot3(a, b, dims=NN):
    ah, al = _split2(a)
    bh, bl = _split2(b)
    d = lambda x, y: lax.dot_general(x, y, dims, preferred_element_type=F32)
    return d(ah, bh) + (d(ah, bl) + d(al, bh))


def _dot_exact_lhs(a_bf16, b, dims=NN):
    b1 = b.astype(BF16)
    r1 = b - b1.astype(F32)
    b2 = r1.astype(BF16)
    b3 = (r1 - b2.astype(F32)).astype(BF16)
    d = lambda y: lax.dot_general(a_bf16, y, dims, preferred_element_type=F32)
    return d(b1) + (d(b2) + d(b3))


def _dot_exact_rhs(a, b_bf16, dims=NN):
    a1 = a.astype(BF16)
    r1 = a - a1.astype(F32)
    a2 = r1.astype(BF16)
    d = lambda x: lax.dot_general(x, b_bf16, dims, preferred_element_type=F32)
    return d(a1) + d(a2)


def _sigmoid(x):
    return 1.0 / (1.0 + jnp.exp(-x))


def _silu(x):
    return x * _sigmoid(x)


def _softplus(x):
    return jnp.maximum(x, 0.0) + jnp.log(1.0 + jnp.exp(-jnp.abs(x)))


def _iota(shape, axis):
    return lax.broadcasted_iota(I32, shape, axis)


def _layer_norm(u, g, b):
    mu = jnp.mean(u, axis=-1, keepdims=True)
    d = u - mu
    var = jnp.mean(d * d, axis=-1, keepdims=True)
    return d * lax.rsqrt(var + LN_EPS) * g + b


def _ada_body(c_ref, w_ref, b_ref, o_ref):
    o_ref[...] = _dot(_silu(c_ref[...]), w_ref[...]) + b_ref[...]


def ada_mod(c_all, ada_w, ada_b, tn=1024):
    nl, d, n = ada_w.shape
    r = c_all.shape[0]
    return pl.pallas_call(
        _ada_body,
        grid=(nl, n // tn),
        in_specs=[pl.BlockSpec((r, d), lambda l, j: (0, 0)),
                  pl.BlockSpec((None, d, tn), lambda l, j: (l, 0, j)),
                  pl.BlockSpec((None, 1, tn), lambda l, j: (l, 0, j))],
        out_specs=pl.BlockSpec((None, r, tn), lambda l, j: (l, 0, j)),
        out_shape=jax.ShapeDtypeStruct((nl, r, n), F32),
        compiler_params=_cp(("parallel", "parallel")),
        name="ada_mod",
    )(c_all, ada_w, ada_b.reshape(nl, 1, n))


class _Group:
    def __init__(self, n, t, tm, modarr, per_row):
        self.n, self.t, self.tm, self.modarr, self.per_row = n, t, tm, modarr, per_row

    def mod(self, l, c):
        d = self.modarr.shape[-1]
        if self.per_row:
            return pl.BlockSpec((None, None, self.tm, d), lambda i: (l, c, i, 0))
        t, tm = self.t, self.tm
        return pl.BlockSpec((None, None, None, 1, d), lambda i: (l, c, (i * tm) // t, 0, 0))


def _modulate_body(x_ref, sc_ref, sh_ref, h_ref):
    h_ref[...] = (x_ref[...] * (1.0 + sc_ref[...]) + sh_ref[...]).astype(h_ref.dtype)


def modulate(x, grp, l, c_sc, c_sh):
    n, d = x.shape
    tm = grp.tm
    row = pl.BlockSpec((tm, d), lambda i: (i, 0))
    return pl.pallas_call(
        _modulate_body, grid=(n // tm,),
        in_specs=[row, grp.mod(l, c_sc), grp.mod(l, c_sh)],
        out_specs=row, out_shape=jax.ShapeDtypeStruct((n, d), BF16),
        compiler_params=_cp(("parallel",)), name="modulate",
    )(x, grp.modarr, grp.modarr)


def _mm_body(a_ref, w_ref, o_ref):
    o_ref[...] = jnp.dot(a_ref[...], w_ref[...], preferred_element_type=F32)


def matmul(a, w, tm, tn):
    n, k = a.shape
    nn = w.shape[1]
    return pl.pallas_call(
        _mm_body, grid=(n // tm, nn // tn),
        in_specs=[pl.BlockSpec((tm, k), lambda i, j: (i, 0)),
                  pl.BlockSpec((k, tn), lambda i, j: (0, j))],
        out_specs=pl.BlockSpec((tm, tn), lambda i, j: (i, j)),
        out_shape=jax.ShapeDtypeStruct((n, nn), F32),
        compiler_params=_cp(("parallel", "parallel")), name="in_proj",
    )(a, w)


def _route(h, w_ref, b_ref, run_ref, eid_ref, gate_ref, rank_ref):
    tm = h.shape[0]
    logits = _dot3(h, w_ref[...]) + b_ref[...]
    lg = logits[:, :LANES]
    le = logits[:, LANES:]
    lane = _iota((tm, LANES), 1).astype(F32)
    neg = jnp.float32(-jnp.inf)
    lgm = jnp.where(lane < N_GROUPS, lg, neg)
    mg = jnp.max(lgm, axis=-1, keepdims=True)
    gidx = jnp.min(jnp.where(lgm == mg, lane, float(LANES)), axis=-1, keepdims=True)
    p_g = 1.0 / jnp.sum(jnp.exp(lgm - mg), axis=-1, keepdims=True)
    lo = gidx * EXPERTS_PER_GROUP
    in_grp = (lane >= lo) & (lane < lo + EXPERTS_PER_GROUP)
    lem = jnp.where(in_grp, le, neg)
    v1 = jnp.max(lem, axis=-1, keepdims=True)
    i1 = jnp.min(jnp.where(lem == v1, lane, float(LANES)), axis=-1, keepdims=True)
    lem2 = jnp.where(lane == i1, neg, lem)
    v2 = jnp.max(lem2, axis=-1, keepdims=True)
    i2 = jnp.min(jnp.where(lem2 == v2, lane, float(LANES)), axis=-1, keepdims=True)
    e2 = jnp.exp(v2 - v1)
    g1 = p_g / (1.0 + e2)
    g2 = p_g * e2 / (1.0 + e2)
    oh1 = jnp.where(lane == i1, 1.0, 0.0)
    oh2 = jnp.where(lane == i2, 1.0, 0.0)
    comb = oh1 + oh2
    tri = jnp.where(_iota((tm, tm), 0) > _iota((tm, tm), 1), 1.0, 0.0).astype(BF16)
    before = jnp.dot(tri, comb.astype(BF16), preferred_element_type=F32) + run_ref[...]
    r1 = jnp.sum(before * oh1, axis=-1, keepdims=True)
    r2 = jnp.sum(before * oh2, axis=-1, keepdims=True)
    run_ref[...] = run_ref[...] + jnp.sum(comb, axis=0, keepdims=True)
    eid_ref[:, 0:1] = i1.astype(I32)
    eid_ref[:, 1:2] = i2.astype(I32)
    gate_ref[:, 0:1] = g1
    gate_ref[:, 1:2] = g2
    rank_ref[:, 0:1] = r1.astype(I32)
    rank_ref[:, 1:2] = r2.astype(I32)


def _outln_body(a_ref, b_ref, wa_ref, wb_ref, x_ref, gt_ref, g_ref, be_ref, sc_ref, sh_ref, wr_ref, br_ref, cnt0_ref,
                hall_ref, xo_ref, ho_ref, eid_ref, gate_ref, rank_ref, cnt_ref, run_ref):
    del hall_ref

    @pl.when(pl.program_id(0) == 0)
    def _():
        run_ref[...] = cnt0_ref[...]

    y = (jnp.dot(a_ref[...], wa_ref[...], preferred_element_type=F32)
         + jnp.dot(b_ref[...], wb_ref[...], preferred_element_type=F32))
    xn = _layer_norm(ALPHA * x_ref[...] + (1.0 + gt_ref[...]) * y, g_ref[...], be_ref[...])
    xo_ref[...] = xn
    h = xn * (1.0 + sc_ref[...]) + sh_ref[...]
    for j in range(h.shape[1] // LANES):
        ho_ref[:, j, :] = h[:, j * LANES:(j + 1) * LANES]
    _route(h, wr_ref, br_ref, run_ref, eid_ref, gate_ref, rank_ref)
    cnt_ref[...] = run_ref[...]


def out_proj_ln(a, b, wa, wb, x, grp, l, ln_g, ln_b, wr, br, cnt0, h_row0, h_all):
    n, d = x.shape
    ka, kb = a.shape[1], b.shape[1]
    tm = min(grp.tm, 256)
    g2 = _Group(grp.n, grp.t, tm, grp.modarr, grp.per_row)
    row = pl.BlockSpec((tm, d), lambda i: (i, 0))
    vec = pl.BlockSpec((1, d), lambda i: (0, 0))
    two = pl.BlockSpec((tm, 2), lambda i: (i, 0))
    one = pl.BlockSpec((1, LANES), lambda i: (0, 0))
    nl = d // LANES
    return pl.pallas_call(
        _outln_body, grid=(n // tm,),
        in_specs=[pl.BlockSpec((tm, ka), lambda i: (i, 0)), pl.BlockSpec((tm, kb), lambda i: (i, 0)),
                  pl.BlockSpec((ka, d), lambda i: (0, 0)), pl.BlockSpec((kb, d), lambda i: (0, 0)),
                  row, g2.mod(l, 2), vec, vec, g2.mod(l, 4), g2.mod(l, 3),
                  pl.BlockSpec((d, 2 * LANES), lambda i: (0, 0)), pl.BlockSpec((1, 2 * LANES), lambda i: (0, 0)), one,
                  pl.BlockSpec(memory_space=pl.ANY)],
        out_specs=[row, pl.BlockSpec((tm, nl, LANES), lambda i: (i + h_row0 // tm, 0, 0)), two, two, two, one],
        out_shape=[jax.ShapeDtypeStruct((n, d), F32), jax.ShapeDtypeStruct(h_all.shape, F32),
                   jax.ShapeDtypeStruct((n, 2), I32), jax.ShapeDtypeStruct((n, 2), F32),
                   jax.ShapeDtypeStruct((n, 2), I32), jax.ShapeDtypeStruct((1, LANES), F32)],
        scratch_shapes=[pltpu.VMEM((1, LANES), F32)],
        input_output_aliases={13: 1},
        compiler_params=_cp(("arbitrary",)), name="out_proj_ln",
    )(a, b, wa, wb, x, grp.modarr, ln_g.reshape(1, d), ln_b.reshape(1, d), grp.modarr, grp.modarr, wr, br, cnt0, h_all)


def _comb_body(*refs, has_next):
    if has_next:
        y0_ref, y1_ref, gate_ref, x_ref, gt_ref, g_ref, be_ref, sc_ref, sh_ref, xo_ref, ho_ref = refs
    else:
        y0_ref, y1_ref, gate_ref, x_ref, gt_ref, g_ref, be_ref, xo_ref = refs
    gate = gate_ref[...]
    flat = lambda r: jnp.concatenate([r[:, j, :] for j in range(r.shape[1])], axis=1)
    y = flat(y0_ref) * gate[:, 0:1] + flat(y1_ref) * gate[:, 1:2]
    xn = _layer_norm(ALPHA * x_ref[...] + (1.0 + gt_ref[...]) * y, g_ref[...], be_ref[...])
    xo_ref[...] = xn
    if has_next:
        ho_ref[...] = (xn * (1.0 + sc_ref[...]) + sh_ref[...]).astype(ho_ref.dtype)


def moe_combine_ln(ys, row0, row1, gate, x, grp, l, ln_g, ln_b, has_next):
    n, d = x.shape
    tm = min(grp.tm, 256)
    g2 = _Group(grp.n, grp.t, tm, grp.modarr, grp.per_row)
    nt = n // tm
    row = pl.BlockSpec((tm, d), lambda i: (i, 0))
    vec = pl.BlockSpec((1, d), lambda i: (0, 0))
    nl = d // LANES
    in_specs = [pl.BlockSpec((tm, nl, LANES), lambda i: (i + row0 // tm, 0, 0)),
                pl.BlockSpec((tm, nl, LANES), lambda i: (i + row1 // tm, 0, 0)), pl.BlockSpec((tm, 2), lambda i: (i, 0)),
                row, g2.mod(l, 5), vec, vec]
    args = [ys, ys, gate, x, grp.modarr, ln_g.reshape(1, d), ln_b.reshape(1, d)]
    out_specs = [row]
    out_shape = [jax.ShapeDtypeStruct((n, d), F32)]
    if has_next:
        in_specs += [g2.mod(l + 1, 1), g2.mod(l + 1, 0)]
        args += [grp.modarr, grp.modarr]
        out_specs.append(row)
        out_shape.append(jax.ShapeDtypeStruct((n, d), BF16))
    res = pl.pallas_call(
        functools.partial(_comb_body, has_next=has_next), grid=(nt,),
        in_specs=in_specs, out_specs=out_specs, out_shape=out_shape,
        compiler_params=_cp(("parallel",)), name="moe_combine_ln",
    )(*args)
    return (res[0], res[1]) if has_next else (res[0], None)


def moe_plan(eid, rank, cnt, dst_of_entry, spare_row0):
    nk = 2 * eid.shape[0]
    n_blocks = -(-nk // MOE_ROWS) + N_EXPERTS
    n_rows = n_blocks * MOE_ROWS
    counts = cnt[0, :N_EXPERTS].astype(I32)
    padded = (counts + MOE_ROWS - 1) // MOE_ROWS * MOE_ROWS
    pad_end = jnp.cumsum(padded)
    pad_start = pad_end - padded
    dest = (pad_start[eid] + rank).reshape(-1)
    src = jnp.full((n_rows + MOE_ROWS,), -1, I32).at[dest].set(jnp.arange(nk, dtype=I32), unique_indices=True,
                                                               mode='promise_in_bounds')
    valid = src >= 0
    pos = jnp.arange(n_rows + MOE_ROWS, dtype=I32)
    safe = jnp.maximum(src, 0)
    tok = jnp.where(valid, safe >> 1, 0)
    dst = jnp.where(valid, dst_of_entry[safe], spare_row0 + pos % MOE_ROWS)
    dst = jnp.concatenate([spare_row0 + jnp.arange(MOE_ROWS, dtype=I32), dst])
    n_used = pad_end[-1] // MOE_ROWS
    blk = jnp.arange(n_blocks, dtype=I32)
    be = jnp.minimum(jnp.sum((blk * MOE_ROWS)[:, None] >= pad_end[None, :], axis=-1), N_EXPERTS - 1).astype(I32)
    be = jnp.where(blk < n_used, be, be[jnp.maximum(n_used - 1, 0)])
    return tok, dst, be, n_used.reshape(1).astype(I32), n_blocks


def _moe_body(tok_ref, dst_ref, be_ref, nu_ref, h_hbm, w1_ref, w3_ref, w2_ref, ys_hbm,
              xbuf, ybuf, w1b, w3b, w2b, gsem, ssem, *, spare_row0):
    b = pl.program_id(0)
    n_used = nu_ref[0]
    slot = b % 2

    def gather_rows(blk, sl):
        for i in range(MOE_ROWS):
            pltpu.make_async_copy(h_hbm.at[pl.ds(tok_ref[blk * MOE_ROWS + i], 1)], xbuf.at[sl, pl.ds(i, 1)],
                                  gsem.at[sl]).start(priority=i % 2)

    def wait_gather(sl):
        pltpu.make_async_copy(xbuf.at[sl], xbuf.at[sl], gsem.at[sl]).wait()

    def scatter_rows(blk, sl):
        for i in range(MOE_ROWS):
            pltpu.make_async_copy(ybuf.at[sl, pl.ds(i, 1)],
                                  ys_hbm.at[pl.ds(dst_ref[(blk + 1) * MOE_ROWS + i], 1)], ssem.at[sl]).start(priority=i % 2)

    def wait_scatter(sl):
        pltpu.make_async_copy(ybuf.at[sl], ybuf.at[sl], ssem.at[sl]).wait()

    @pl.when(b == 0)
    def _():
        ybuf[...] = jnp.zeros_like(ybuf)
        gather_rows(0, 0)

    @pl.when(b < n_used)
    def _():
        @pl.when(jnp.logical_or(b == 0, be_ref[b] != be_ref[jnp.maximum(b - 1, 0)]))
        def _():
            w1b[...] = w1_ref[...].astype(BF16)
            w3b[...] = w3_ref[...].astype(BF16)
            w2b[...] = w2_ref[...].astype(BF16)

        wait_gather(slot)
        xs = xbuf.at[slot]
        x = jnp.concatenate([xs[:, j, :] for j in range(xs.shape[1])], axis=1).astype(BF16)
        h1 = jnp.dot(x, w1b[...], preferred_element_type=F32)
        h3 = jnp.dot(x, w3b[...], preferred_element_type=F32)
        mid = (_silu(h1) * h3).astype(BF16)
        y = jnp.dot(mid, w2b[...], preferred_element_type=F32)
        gather_rows(b + 1, 1 - slot)
        scatter_rows(b - 1, 1 - slot)

        @pl.when(b >= 1)
        def _():
            wait_scatter(slot)

        for j in range(ybuf.shape[2]):
            ybuf[slot, :, j, :] = y[:, j * LANES:(j + 1) * LANES]

        @pl.when(b == n_used - 1)
        def _():
            scatter_rows(b, slot)
            wait_gather(1 - slot)
            wait_scatter(1 - slot)
            wait_scatter(slot)


def moe_ffn(h, tok, dst, be, n_used, n_blocks, w1, w3, w2, layer):
    n, nl, _ = h.shape
    d = nl * LANES
    de = w1.shape[-1]
    wspec_in = pl.BlockSpec((None, None, d, de), lambda b, tok, dst, be, nu: (layer, be[b], 0, 0))
    wspec_out = pl.BlockSpec((None, None, de, d), lambda b, tok, dst, be, nu: (layer, be[b], 0, 0))
    return pl.pallas_call(
        functools.partial(_moe_body, spare_row0=2 * n),
        grid_spec=pltpu.PrefetchScalarGridSpec(
            num_scalar_prefetch=4, grid=(n_blocks,),
            in_specs=[pl.BlockSpec(memory_space=pl.ANY), wspec_in, wspec_in, wspec_out],
            out_specs=pl.BlockSpec(memory_space=pl.ANY),
            scratch_shapes=[pltpu.VMEM((2, MOE_ROWS, nl, LANES), F32), pltpu.VMEM((2, MOE_ROWS, nl, LANES), F32),
                            pltpu.VMEM((d, de), BF16), pltpu.VMEM((d, de), BF16), pltpu.VMEM((de, d), BF16),
                            pltpu.SemaphoreType.DMA((2,)), pltpu.SemaphoreType.DMA((2,))]),
        out_shape=jax.ShapeDtypeStruct((2 * n + MOE_ROWS, nl, LANES), F32),
        compiler_params=_cp(("arbitrary",)), name="moe_ffn",
    )(tok, dst, be, n_used, h, w1, w3, w2)


def _conv_body(va_ref, bg_ref, cg_ref, w_ref, buf_ref, o_ref, nb_ref):
    u = cg_ref[...] * va_ref[...]
    t = u.shape[0]
    row = _iota(u.shape, 0)
    b0 = buf_ref[0:1, :]
    b1 = buf_ref[1:2, :]
    u1 = jnp.where(row == 0, b1, pltpu.roll(u, 1, axis=0))
    u2 = jnp.where(row == 0, b0, jnp.where(row == 1, b1, pltpu.roll(u, 2, axis=0)))
    w = w_ref[...]
    y = w[0:1] * u2 + w[1:2] * u1 + w[2:3] * u
    o_ref[...] = (bg_ref[...] * y).astype(o_ref.dtype)
    nb_ref[...] = u[t - 2:t, :]


def conv_seq(z3, conv_w, buf, tc=256):
    bsz, t, _ = z3.shape
    nc = D_CONV // tc
    col = lambda off: pl.BlockSpec((None, t, tc), lambda b, j: (b, 0, off + j))
    return pl.pallas_call(
        _conv_body, grid=(bsz, nc),
        in_specs=[col(0), col(nc), col(2 * nc),
                  pl.BlockSpec((CONV_WIDTH, tc), lambda b, j: (0, j)),
                  pl.BlockSpec((None, 2, tc), lambda b, j: (b, 0, j))],
        out_specs=[pl.BlockSpec((None, t, tc), lambda b, j: (b, 0, j)),
                   pl.BlockSpec((None, 2, tc), lambda b, j: (b, 0, j))],
        out_shape=[jax.ShapeDtypeStruct((bsz, t, D_CONV), BF16), jax.ShapeDtypeStruct((bsz, 2, D_CONV), F32)],
        compiler_params=_cp(("parallel", "parallel")), name="conv_seq",
    )(z3, z3, z3, conv_w, buf)


def _conv_step_body(va_ref, bg_ref, cg_ref, w_ref, b0_ref, b1_ref, o_ref, u_ref):
    u = cg_ref[...] * va_ref[...]
    w = w_ref[...]
    y = w[0:1] * b0_ref[...] + w[1:2] * b1_ref[...] + w[2:3] * u
    o_ref[...] = (bg_ref[...] * y).astype(o_ref.dtype)
    u_ref[...] = u


def conv_step(z2, conv_w, buf):
    bsz = z2.shape[0]
    col = lambda j: pl.BlockSpec((bsz, D_CONV), lambda i: (0, j))
    full = pl.BlockSpec((bsz, D_CONV), lambda i: (0, 0))
    a, u = pl.pallas_call(
        _conv_step_body, grid=(1,),
        in_specs=[col(0), col(1), col(2), pl.BlockSpec((CONV_WIDTH, D_CONV), lambda i: (0, 0)), full, full],
        out_specs=[full, full],
        out_shape=[jax.ShapeDtypeStruct((bsz, D_CONV), BF16), jax.ShapeDtypeStruct((bsz, D_CONV), F32)],
        compiler_params=_cp(("arbitrary",)), name="conv_step",
    )(z2, z2, z2, conv_w, buf[:, 0], buf[:, 1])
    return a, jnp.stack([buf[:, 1], u], axis=1)


def _hgrn_body(q_ref, f_ref, i_ref, g_ref, lb_ref, nw_ref, s0_ref, o_ref, so_ref, st_ref,
               *, chunk, sub, nchunk, nhead, t_real, layer, indep):
    tstep = pl.program_id(2)
    tb = chunk * nchunk
    dk = HGRN_DK

    if not indep:
        @pl.when(tstep == 0)
        def _():
            for hd in range(nhead):
                st_ref[hd] = s0_ref[hd].T

    lbl = lb_ref[...]
    e = jnp.exp(lbl - jnp.max(lbl, axis=0, keepdims=True))
    sm = e / jnp.sum(e, axis=0, keepdims=True)
    lb = jnp.zeros((1, lbl.shape[1]), F32)
    for r in range(1, layer + 1):
        lb = lb + sm[r:r + 1]
    tri = jnp.where(_iota((chunk, chunk), 0) >= _iota((chunk, chunk), 1), 1.0, 0.0).astype(BF16)
    nw = nw_ref[...]
    neg = jnp.float32(-jnp.inf)

    fr = f_ref[...]
    v = i_ref[...]
    logf = jnp.log(lb + (1.0 - lb) * _sigmoid(fr))
    kin = (1.0 - lb) * _sigmoid(-fr)
    if t_real is not None:
        row = _iota(fr.shape, 0)
        live = ((row & (chunk - 1)) if indep else (tstep * tb + row)) < t_real
        logf = jnp.where(live, logf, 0.0)
        kin = jnp.where(live, kin, 0.0)
    qs = _silu(q_ref[...])
    causal = _iota((sub, sub, 1), 1) <= _iota((sub, sub, 1), 0)
    cuts = [(slice(c * chunk, (c + 1) * chunk), slice(hd * dk, (hd + 1) * dk))
            for c in range(nchunk) for hd in range(nhead)]
    bb_all = [_dot_exact_lhs(tri, logf[c * chunk:(c + 1) * chunk]) for c in range(nchunk)]
    bb_l = [bb_all[c][:, hd * dk:(hd + 1) * dk] for c in range(nchunk) for hd in range(nhead)]
    qs_l = [qs[rows, ls] for rows, ls in cuts]
    kin_l = [kin[rows, ls] for rows, ls in cuts]
    v_l = [v[rows, ls] for rows, ls in cuts]
    bl_l = [bb[chunk - 1:chunk] for bb in bb_l]
    kv_l = [_dot(vc, kc * jnp.exp(bl - bb), TN) for vc, kc, bl, bb in zip(v_l, kin_l, bl_l, bb_l)]
    intra_l = []
    for qc, kc, vc, bb in zip(qs_l, kin_l, v_l, bb_l):
        parts = []
        for blk in range(chunk // sub):
            lo = blk * sub
            qi, bi, ki, vi = qc[lo:lo + sub], bb[lo:lo + sub], kc[lo:lo + sub], vc[lo:lo + sub]
            d = bi[:, None, :] - bi[None, :, :]
            sc = jnp.sum(qi[:, None, :] * ki[None, :, :] * jnp.exp(jnp.where(causal, d, neg)), axis=-1)
            oi = _dot(sc, vi)
            if blk > 0:
                anchor = bb[lo - 1:lo]
                qt = qi * jnp.exp(bi - anchor)
                kt = kc[:lo] * jnp.exp(anchor - bb[:lo])
                oi = oi + _dot(_dot(qt, kt, NT), vc[:lo])
            parts.append(oi)
        intra_l.append(parts[0] if len(parts) == 1 else jnp.concatenate(parts, axis=0))
    if indep:
        st_l = [s0_ref[c, hd].T for c in range(nchunk) for hd in range(nhead)]
        for idx, (st, bl, kv) in enumerate(zip(st_l, bl_l, kv_l)):
            so_ref[idx // nhead, idx % nhead] = (st * jnp.exp(bl) + kv).T
    else:
        st_l = []
        cur = [st_ref[hd] for hd in range(nhead)]
        for c in range(nchunk):
            for hd in range(nhead):
                idx = c * nhead + hd
                st_l.append(cur[hd])
                cur[hd] = cur[hd] * jnp.exp(bl_l[idx]) + kv_l[idx]
        for hd in range(nhead):
            st_ref[hd] = cur[hd]
    o_l = [_dot(qc * jnp.exp(bb), sc, NT) + oi for qc, bb, sc, oi in zip(qs_l, bb_l, st_l, intra_l)]
    o_l = [o * lax.rsqrt(jnp.mean(o * o, axis=-1, keepdims=True) + RMS_EPS) * nw for o in o_l]
    rows_l = [o_l[c * nhead] if nhead == 1 else jnp.concatenate(o_l[c * nhead:(c + 1) * nhead], axis=1)
              for c in range(nchunk)]
    o = rows_l[0] if nchunk == 1 else jnp.concatenate(rows_l, axis=0)
    o_ref[...] = (o * _silu(g_ref[...])).astype(o_ref.dtype)

    if not indep:
        @pl.when(tstep == pl.num_programs(2) - 1)
        def _():
            for hd in range(nhead):
                so_ref[hd] = st_ref[hd].T


def hgrn_seq(z3, hgrn_lb, norm_w, s0, layer, chunk, nchunk, nhead=1, t_real=None, indep=False):
    bsz, t, _ = z3.shape
    tb = chunk * nchunk
    sb = None
    if indep:
        sb = nchunk
        z3 = z3.reshape(bsz // nchunk, tb, z3.shape[2])
        bsz, t = bsz // nchunk, tb
    wl = nhead * LANES
    groups = HGRN_HEADS // nhead
    qoff = 3 * D_CONV // wl
    col = lambda k: pl.BlockSpec((None, tb, wl), lambda b, h, s: (b, s, qoff + k * groups + h))
    st = pl.BlockSpec((sb, nhead, HGRN_DK, HGRN_DV), lambda b, h, s: (b, h, 0, 0))
    body = functools.partial(_hgrn_body, chunk=chunk, sub=min(HGRN_SUB, chunk), nchunk=nchunk, nhead=nhead, t_real=t_real,
                             layer=layer, indep=indep)
    nseq = bsz * (sb or 1)
    o, so = pl.pallas_call(
        body, grid=(bsz, groups, t // tb),
        in_specs=[col(0), col(1), col(2), col(3),
                  pl.BlockSpec((N_EVEN, wl), lambda b, h, s: (0, h)),
                  pl.BlockSpec((1, HGRN_DV), lambda b, h, s: (0, 0)), st],
        out_specs=[pl.BlockSpec((None, tb, wl), lambda b, h, s: (b, s, h)), st],
        out_shape=[jax.ShapeDtypeStruct((bsz, t, D_HV), BF16),
                   jax.ShapeDtypeStruct((nseq, HGRN_HEADS, HGRN_DK, HGRN_DV), F32)],
        scratch_shapes=[pltpu.VMEM((nhead, HGRN_DV, HGRN_DK), F32)],
        compiler_params=_cp(("parallel", "parallel", "arbitrary")), name="hgrn_seq",
    )(z3, z3, z3, z3, hgrn_lb, norm_w.reshape(1, HGRN_DV), s0)
    return o.reshape(nseq, chunk if indep else t, D_HV), so


def _attn_body(sink_ref, zq_ref, zkv_ref, cos_ref, sin_ref, ck_ref, cv_ref, o_ref, nk_ref, nv_ref, kp_ref, vp_ref,
               *, prev_valid, t_real, tq):
    i = pl.program_id(1)
    w = WINDOW

    @pl.when(i == 0)
    def _():
        kp_ref[...] = ck_ref[...]
        vp_ref[...] = cv_ref[...]

    cos = cos_ref[...]
    sin = sin_ref[...]

    def rope(x):
        width = x.shape[1]
        reps = width // LANES
        first = (_iota(x.shape, 1) & (HEAD_DIM - 1)) < (HEAD_DIM // 2)
        rot = jnp.where(first, pltpu.roll(x, width - HEAD_DIM // 2, axis=1), pltpu.roll(x, HEAD_DIM // 2, axis=1))
        return x * jnp.tile(cos, (1, reps)) + rot * jnp.tile(sin, (1, reps))

    kv = zkv_ref[...]
    qr = rope(zq_ref[...]) * (HEAD_DIM ** -0.5)
    kr = rope(kv[:, :D_KV])
    v = kv[:, D_KV:]
    kprev = kp_ref[...]
    vprev = vp_ref[...]

    grp = ATTN_HEADS // ATTN_KV_HEADS
    rows = _iota((grp * tq, w + tq), 0)
    cols = _iota((grp * tq, w + tq), 1)
    delta = (rows & (tq - 1)) + w - cols
    valid = (delta >= 0) & (delta <= w)
    if not prev_valid:
        valid = valid & (cols >= jnp.where(i > 0, 0, w))
    head_of_row = _iota((grp * tq, 1), 0) >> int(math.log2(tq))
    neg = jnp.float32(-jnp.inf)
    outs = []
    for g in range(ATTN_KV_HEADS):
        ls = slice(g * HEAD_DIM, (g + 1) * HEAD_DIM)
        kg = jnp.concatenate([kprev[:, ls], kr[:, ls]], axis=0)
        vg = jnp.concatenate([vprev[:, ls], v[:, ls]], axis=0)
        qg = jnp.concatenate([qr[:, (grp * g + hh) * HEAD_DIM:(grp * g + hh + 1) * HEAD_DIM] for hh in range(grp)],
                             axis=0)
        s = jnp.where(valid, _dot(qg, kg, NT), neg)
        sink = jnp.zeros((grp * tq, 1), F32)
        for hh in range(grp):
            sink = jnp.where(head_of_row == hh, sink_ref[grp * g + hh], sink)
        m = jnp.maximum(jnp.max(s, axis=-1, keepdims=True), sink)
        p = jnp.exp(s - m)
        p = p / (jnp.sum(p, axis=-1, keepdims=True) + jnp.exp(sink - m))
        og = _dot(p, vg)
        outs += [og[hh * tq:(hh + 1) * tq] for hh in range(grp)]
    o_ref[...] = jnp.concatenate(outs, axis=1).astype(o_ref.dtype)
    if t_real == w:
        kp_ref[...] = kr
        vp_ref[...] = v

    @pl.when(i == pl.num_programs(1) - 1)
    def _():
        if t_real == w:
            nk_ref[...] = kr
            nv_ref[...] = v
        else:
            last = _iota(kprev.shape, 0) == w - 1
            nk_ref[...] = jnp.where(last, kr[0:1], pltpu.roll(kprev, w - 1, axis=0))
            nv_ref[...] = jnp.where(last, v[0:1], pltpu.roll(vprev, w - 1, axis=0))


def attn_seq(z3, sinks, cos, sin, cache_k, cache_v, prev_valid, t_real, tq=WINDOW):
    bsz, t, _ = z3.shape
    nb = t // tq
    assert tq == WINDOW or nb == 1
    cache = pl.BlockSpec((None, WINDOW, D_KV), lambda b, i: (b, 0, 0))
    tab = pl.BlockSpec((tq, LANES), lambda b, i: (i, 0))
    body = functools.partial(_attn_body, prev_valid=prev_valid, t_real=t_real, tq=tq)
    return pl.pallas_call(
        body, grid=(bsz, nb),
        in_specs=[pl.BlockSpec(memory_space=pltpu.SMEM),
                  pl.BlockSpec((None, tq, D_Q), lambda b, i: (b, i, 0)),
                  pl.BlockSpec((None, tq, 2 * D_KV), lambda b, i: (b, i, D_Q // (2 * D_KV))),
                  tab, tab, cache, cache],
        out_specs=[pl.BlockSpec((None, tq, D_Q), lambda b, i: (b, i, 0)), cache, cache],
        out_shape=[jax.ShapeDtypeStruct((bsz, t, D_Q), BF16),
                   jax.ShapeDtypeStruct((bsz, WINDOW, D_KV), F32), jax.ShapeDtypeStruct((bsz, WINDOW, D_KV), F32)],
        scratch_shapes=[pltpu.VMEM((WINDOW, D_KV), F32), pltpu.VMEM((WINDOW, D_KV), F32)],
        compiler_params=_cp(("parallel", "arbitrary")), name="attn_seq",
    )(sinks, z3, z3, cos, sin, cache_k, cache_v)


def rope_tables(pos):
    half = HEAD_DIM // 2
    inv = jnp.exp(-math.log(ROPE_THETA) * jnp.arange(half, dtype=F32) / half)
    ang = pos.astype(F32)[:, None] * inv[None, :]
    c, s = jnp.cos(ang), jnp.sin(ang)
    cos = jnp.concatenate([c, c, c, c], axis=1)
    sin = jnp.concatenate([-s, s, -s, s], axis=1)
    return cos, sin


def _rwkv_body(r_ref, k_ref, v_ref, wa_ref, gd_ref, sr_ref, sk_ref, sv_ref, swa_ref, sgd_ref,
               mr_ref, mk_ref, mv_ref, mwa_ref, mgd_ref, w0_ref, w2_ref, a0_ref, a2_ref, g2_ref,
               kkp_ref, ka_ref, rk_ref, lg_ref, lbias_ref, s0_ref, o_ref, so_ref, st_ref, prev_ref, prevw_ref,
               *, chunk, nchunk, npair, t_real, indep):
    tstep = pl.program_id(2)
    n = RWKV_N
    ln = chunk
    tb = chunk * nchunk

    if not indep:
        @pl.when(tstep == 0)
        def _():
            st_ref[...] = s0_ref[...]
            prev_ref[0:1, :] = sr_ref[...]
            prev_ref[1:2, :] = sk_ref[...]
            prev_ref[2:3, :] = sv_ref[...]
            prevw_ref[0:1, :] = swa_ref[...]
            prevw_ref[1:2, :] = sgd_ref[...]

    def mix(x_ref, mu_ref, s_ref, p_ref, idx):
        x = x_ref[...]
        row = _iota(x.shape, 0)
        if indep:
            before = jnp.broadcast_to(s_ref[...], (nchunk, ln, x.shape[1])).reshape(tb, x.shape[1])
            shifted = jnp.where((row & (ln - 1)) == 0, before, pltpu.roll(x, 1, axis=0))
        else:
            shifted = jnp.where(row == 0, p_ref[idx:idx + 1, :], pltpu.roll(x, 1, axis=0))
            p_ref[idx:idx + 1, :] = x[tb - 1:tb, :]
        return x + mu_ref[...] * (shifted - x)

    r = mix(r_ref, mr_ref, sr_ref, prev_ref, 0)
    kr = mix(k_ref, mk_ref, sk_ref, prev_ref, 1)
    vr = mix(v_ref, mv_ref, sv_ref, prev_ref, 2)
    wa = mix(wa_ref, mwa_ref, swa_ref, prevw_ref, 0)
    gd = mix(gd_ref, mgd_ref, sgd_ref, prevw_ref, 1)[:, :RWKV_RANK]
    wd = wa[:, :RWKV_RANK]
    ad = wa[:, RWKV_RANK:]

    w_log = -_softplus(-(w0_ref[...] + _dot3(jnp.tanh(wd), w2_ref[...]))) - 0.5
    logw = -jnp.exp(w_log)
    a = _sigmoid(a0_ref[...] + _dot3(ad, a2_ref[...]))
    gate = _dot3(_sigmoid(gd), g2_ref[...])

    sh = int(math.log2(n))
    seg = jnp.where((_iota((LANES, LANES), 0) >> sh) == (_iota((LANES, LANES), 1) >> sh), 1.0, 0.0).astype(BF16)

    def segsum(x):
        tiles = [_dot_exact_rhs(x[:, i * LANES:(i + 1) * LANES], seg) for i in range(npair)]
        return tiles[0] if npair == 1 else jnp.concatenate(tiles, axis=1)

    kkv = kr * kkp_ref[...]
    kk = kkv / jnp.maximum(jnp.sqrt(segsum(kkv * kkv)), 1e-12)
    kf = kr * (1.0 + (a - 1.0) * ka_ref[...])
    bonus = segsum(r * kf * rk_ref[...]) * vr
    if t_real is not None:
        row = _iota(logw.shape, 0)
        live = ((row & (ln - 1)) if indep else (tstep * tb + row)) < t_real
        zero = jnp.zeros_like(logw)
        logw = jnp.where(live, logw, zero)
        kk = jnp.where(live, kk, zero)
        kf = jnp.where(live, kf, zero)
        vr_s = jnp.where(live, vr, zero)
    else:
        vr_s = vr

    tri = jnp.where(_iota((ln, ln), 0) >= _iota((ln, ln), 1), 1.0, 0.0).astype(BF16)
    rr = _iota((ln, ln), 0)
    cc = _iota((ln, ln), 1)
    strict = rr > cc
    incl = rr >= cc
    eye = jnp.where(rr == cc, 1.0, 0.0)
    eye_n = jnp.where(_iota((n, n), 0) == _iota((n, n), 1), 1.0, 0.0)

    nhead = 2 * npair
    am_l, rm_l, bp_l, kp_l, bl_l, kl_l, v_l, gl_l = [], [], [], [], [], [], [], []
    for ci in range(nchunk):
        rows = slice(ci * ln, (ci + 1) * ln)
        lw = logw[rows]
        c = _dot_exact_lhs(tri, lw)
        gam = jnp.exp(c)
        ginv = jnp.exp(-c)
        am = -kk[rows] * jnp.exp(c - lw)
        rm = r[rows] * gam
        bp = kk[rows] * a[rows] * ginv
        kp = kf[rows] * ginv
        g_last = gam[ln - 1:ln, :]
        bpl = bp * g_last
        kpl = kp * g_last
        vv = vr_s[rows]
        for hd in range(nhead):
            ls = slice(hd * n, (hd + 1) * n)
            am_l.append(am[:, ls]); rm_l.append(rm[:, ls]); bp_l.append(bp[:, ls]); kp_l.append(kp[:, ls])
            bl_l.append(bpl[:, ls]); kl_l.append(kpl[:, ls]); v_l.append(vv[:, ls]); gl_l.append(g_last[:, ls])
    each = lambda fn, *ls: [fn(*xs) for xs in zip(*ls)]
    pw_l = each(lambda am, rm, bp, kp: _dot3(jnp.concatenate([am, rm], axis=0),
                                             jnp.concatenate([bp, kp], axis=0), NT), am_l, rm_l, bp_l, kp_l)
    m_l = each(lambda pw: jnp.where(strict, pw[:ln, :ln], 0.0), pw_l)
    nm_l = each(lambda pw: jnp.where(strict, pw[:ln, ln:], 0.0), pw_l)
    qb_l = each(lambda pw: jnp.where(incl, pw[ln:, :ln], 0.0), pw_l)
    qk_l = each(lambda pw: jnp.where(incl, pw[ln:, ln:], 0.0), pw_l)
    nq_l = each(lambda nm, qk, v: _dot(jnp.concatenate([nm, qk], axis=0), v), nm_l, qk_l, v_l)
    nv_l = [x[:ln] for x in nq_l]
    qkv_l = [x[ln:] for x in nq_l]
    kv_l = each(lambda v, kl: _dot(v, kl, TN), v_l, kl_l)
    tinv_l = each(lambda m: eye + m, m_l)
    p_l = m_l
    for _ in range(int(math.log2(ln)) - 1):
        p_l = each(lambda p: _dot(p, p), p_l)
        tinv_l = each(lambda t, p: t + _dot(t, p), tinv_l, p_l)
    wu_l = each(lambda t, am, nv: _dot(t, jnp.concatenate([am, nv], axis=1)), tinv_l, am_l, nv_l)
    ac_l = each(lambda wu, bl: _dot(wu, bl, TN), wu_l, bl_l)
    a_l = each(lambda ac, gl: eye_n * gl + ac[:n], ac_l, gl_l)
    c_l = each(lambda ac, kv: ac[n:] + kv, ac_l, kv_l)
    ro_l = each(_dot, qb_l, wu_l)
    rt_l = each(lambda rm, ro: rm + ro[:, :n], rm_l, ro_l)
    o0_l = each(lambda ro, qkv: ro[:, n:] + qkv, ro_l, qkv_l)

    o_rows = []
    if indep:
        s_l = [s0_ref[ci, hd] for ci in range(nchunk) for hd in range(nhead)]
        o_all = each(lambda rt, s, o0: _dot(rt, s, NT) + o0, rt_l, s_l, o0_l)
        s_l = each(lambda s, am, cm: _dot(s, am) + cm, s_l, a_l, c_l)
        for ci in range(nchunk):
            o_rows.append(o_all[ci * nhead:(ci + 1) * nhead])
            for hd in range(nhead):
                so_ref[ci, hd] = s_l[ci * nhead + hd]
    else:
        s_l = [st_ref[hd] for hd in range(nhead)]
        for ci in range(nchunk):
            sl = slice(ci * nhead, (ci + 1) * nhead)
            o_rows.append(each(lambda rt, s, o0: _dot(rt, s, NT) + o0, rt_l[sl], s_l, o0_l[sl]))
            s_l = each(lambda s, am, cm: _dot(s, am) + cm, s_l, a_l[sl], c_l[sl])
        for hd in range(nhead):
            st_ref[hd] = s_l[hd]
    cols = [o_rows[0][hd] if nchunk == 1 else jnp.concatenate([o_rows[ci][hd] for ci in range(nchunk)], axis=0)
            for hd in range(nhead)]
    o = jnp.concatenate(cols, axis=1)
    mu_o = segsum(o) * (1.0 / n)
    dlt = o - mu_o
    var_o = segsum(dlt * dlt) * (1.0 / n)
    o = dlt * lax.rsqrt(var_o + RWKV_GN_EPS) * lg_ref[...] + lbias_ref[...]
    o_ref[...] = ((o + bonus) * gate).astype(o_ref.dtype)

    if not indep:
        @pl.when(tstep == pl.num_programs(2) - 1)
        def _():
            so_ref[...] = st_ref[...]


def rwkv_seq(z3, shift, P, j, s0, chunk, nchunk=1, npair=1, t_real=None, indep=False):
    bsz, t, _ = z3.shape
    tb = chunk * nchunk
    sb = None
    if indep:
        sb = nchunk
        z3 = z3.reshape(bsz // nchunk, tb, z3.shape[2])
        bsz, t = bsz // nchunk, tb
    wl = npair * LANES
    groups = D_RWKV // wl
    zoff = (D_Q + 2 * D_KV) // LANES
    nb = D_RWKV // LANES
    wa_blk = 3 * nb
    gd_blk = 3 * nb + 1
    zc = lambda off: pl.BlockSpec((None, tb, wl), lambda b, p, s: (b, s, (zoff + off) // npair + p))
    zw = lambda blk: pl.BlockSpec((None, tb, LANES), lambda b, p, s: (b, s, zoff + blk))
    sc = lambda off: pl.BlockSpec((sb, 1, wl), lambda b, p, s: (b, 0, off // npair + p))
    sw = lambda blk: pl.BlockSpec((sb, 1, LANES), lambda b, p, s: (b, 0, blk))
    mc = lambda off: pl.BlockSpec((1, wl), lambda b, p, s: (0, off // npair + p))
    mw = lambda blk: pl.BlockSpec((1, LANES), lambda b, p, s: (0, blk))
    vec = pl.BlockSpec((1, wl), lambda b, p, s: (0, p))
    lora = pl.BlockSpec((RWKV_RANK, wl), lambda b, p, s: (0, p))
    st = pl.BlockSpec((sb, 2 * npair, RWKV_N, RWKV_N), lambda b, p, s: (b, p, 0, 0))
    mu = jnp.pad(P['rwkv_mu'][j], (0, D_SHIFT_PAD - D_SHIFT)).reshape(1, D_SHIFT_PAD)
    row = lambda x: x.reshape(1, D_RWKV)
    body = functools.partial(_rwkv_body, chunk=chunk, nchunk=nchunk, npair=npair, t_real=t_real, indep=indep)
    nseq = bsz * (sb or 1)
    o, so = pl.pallas_call(
        body, grid=(bsz, groups, t // tb),
        in_specs=[zc(0), zc(nb), zc(2 * nb), zw(wa_blk), zw(gd_blk),
                  sc(0), sc(nb), sc(2 * nb), sw(wa_blk), sw(gd_blk),
                  mc(0), mc(nb), mc(2 * nb), mw(wa_blk), mw(gd_blk),
                  vec, lora, vec, lora, lora, vec, vec, vec, vec, vec, st],
        out_specs=[pl.BlockSpec((None, tb, wl), lambda b, p, s: (b, s, p)), st],
        out_shape=[jax.ShapeDtypeStruct((bsz, t, D_RWKV), BF16),
                   jax.ShapeDtypeStruct((nseq, RWKV_HEADS, RWKV_N, RWKV_N), F32)],
        scratch_shapes=[pltpu.VMEM((2 * npair, RWKV_N, RWKV_N), F32), pltpu.VMEM((8, wl), F32),
                        pltpu.VMEM((8, LANES), F32)],
        compiler_params=_cp(("parallel", "parallel", "arbitrary")), name="rwkv_seq",
    )(z3, z3, z3, z3, z3, shift, shift, shift, shift, shift, mu, mu, mu, mu, mu,
      row(P['rwkv_w0'][j]), P['rwkv_w2'][j], row(P['rwkv_a0'][j]), P['rwkv_a2'][j], P['rwkv_g2'][j],
      row(P['rwkv_kk'][j]), row(P['rwkv_ka'][j]), row(P['rwkv_rk'][j]), row(P['rwkv_lnx_g'][j]),
      row(P['rwkv_lnx_b'][j]), s0)
    return o.reshape(nseq, chunk if indep else t, D_RWKV), so


def _pad_time(z2, tp):
    return jnp.pad(z2[:, None, :], ((0, 0), (0, tp - 1), (0, 0)))


def _mix_layer(l, h, x, grp, pos0, W, P, st, single):
    n = x.shape[0]
    t = grp.t
    bsz = n // t
    tm = grp.tm
    j = l // 2
    step_pad = 8
    new = {}
    if l % 2 == 0:
        z = matmul(h, W['w_in_even'][j], tm, D_IN_EVEN // 4)
        if single:
            a_out, new['conv'] = conv_step(z, P['conv_w'][j], st['conv'][j])
            b3, new['hgrn'] = hgrn_seq(_pad_time(z, step_pad), P['hgrn_lb'], P['hgrn_norm'][j], st['hgrn'][j], j,
                                       chunk=step_pad, nchunk=4, nhead=4, t_real=1, indep=True)
            b_out = b3[:, 0]
        else:
            z3 = z.reshape(bsz, t, D_IN_EVEN)
            a3, new['conv'] = conv_seq(z3, P['conv_w'][j], st['conv'][j])
            b3, new['hgrn'] = hgrn_seq(z3, P['hgrn_lb'], P['hgrn_norm'][j], st['hgrn'][j], j, chunk=64, nchunk=8)
            a_out, b_out = a3.reshape(n, D_CONV), b3.reshape(n, D_HV)
        wa, wb = W['w_out_even'][j][:D_CONV], W['w_out_even'][j][D_CONV:]
    else:
        z = matmul(h, W['w_in_odd'][j], tm, D_IN_ODD_PAD // 2)
        shift_in = jnp.pad(st['shift'][j], ((0, 0), (0, D_SHIFT_PAD - D_SHIFT)))[:, None, :]
        kc = st['k'][j].reshape(bsz, WINDOW, D_KV)
        vc = st['v'][j].reshape(bsz, WINDOW, D_KV)
        if single:
            cos, sin = rope_tables(pos0 + jnp.arange(step_pad, dtype=I32))
            a3, nk, nv = attn_seq(_pad_time(z[:, :D_Q + 2 * D_KV], step_pad), P['attn_sinks'][j], cos, sin,
                                  kc, vc, prev_valid=True, t_real=1, tq=step_pad)
            b3, new['rwkv'] = rwkv_seq(_pad_time(z, step_pad), shift_in, P, j, st['rwkv'][j], chunk=step_pad,
                                       nchunk=4, npair=4, t_real=1, indep=True)
            a_out, b_out = a3[:, 0], b3[:, 0]
            new['shift'] = z[:, D_Q + 2 * D_KV:D_IN_ODD]
        else:
            z3 = z.reshape(bsz, t, D_IN_ODD_PAD)
            cos, sin = rope_tables(pos0 + jnp.arange(t, dtype=I32))
            a3, nk, nv = attn_seq(z3, P['attn_sinks'][j], cos, sin, kc, vc, prev_valid=False, t_real=WINDOW)
            b3, new['rwkv'] = rwkv_seq(z3, shift_in, P, j, st['rwkv'][j], chunk=64, nchunk=4, npair=2)
            a_out, b_out = a3.reshape(n, D_Q), b3.reshape(n, D_RWKV)
            new['shift'] = z3[:, t - 1, D_Q + 2 * D_KV:D_IN_ODD]
        new['k'] = nk.reshape(bsz, WINDOW, ATTN_KV_HEADS, HEAD_DIM)
        new['v'] = nv.reshape(bsz, WINDOW, ATTN_KV_HEADS, HEAD_DIM)
        wa, wb = W['w_out_odd'][j][:D_Q], W['w_out_odd'][j][D_Q:]
    return a_out, b_out, wa, wb, new


def kernel(x_prompt, x_sample, c_prompt, c_sample, state_conv, state_hgrn, cache_swa_k, cache_swa_v,
           state_rwkv, state_shift, ada_w, ada_b, ln_g, ln_b, w_in_even, w_out_even, conv_w, hgrn_lb,
           hgrn_norm, w_in_odd, w_out_odd, attn_sinks, rwkv_mu, rwkv_w0, rwkv_w2, rwkv_a0, rwkv_a2,
           rwkv_g2, rwkv_kk, rwkv_ka, rwkv_rk, rwkv_lnx_g, rwkv_lnx_b, moe_w_grp, moe_b_grp, moe_w_exp,
           moe_b_exp, moe_w1, moe_w3, moe_w2):
    P = dict(ln_g=ln_g, ln_b=ln_b, conv_w=conv_w, hgrn_lb=hgrn_lb, hgrn_norm=hgrn_norm, attn_sinks=attn_sinks,
             rwkv_mu=rwkv_mu, rwkv_w0=rwkv_w0, rwkv_w2=rwkv_w2, rwkv_a0=rwkv_a0, rwkv_a2=rwkv_a2,
             rwkv_g2=rwkv_g2, rwkv_kk=rwkv_kk, rwkv_ka=rwkv_ka, rwkv_rk=rwkv_rk.reshape(N_ODD, D_RWKV),
             rwkv_lnx_g=rwkv_lnx_g, rwkv_lnx_b=rwkv_lnx_b)
    bp, tp, d = x_prompt.shape
    bs, ts, _ = x_sample.shape
    n_p, n_s = bp * tp, bs * ts
    router_w = jnp.zeros((DEPTH, d, 2 * LANES), F32)
    router_w = router_w.at[:, :, :N_GROUPS].set(moe_w_grp).at[:, :, LANES:LANES + N_EXPERTS].set(moe_w_exp)
    router_b = jnp.zeros((DEPTH, 1, 2 * LANES), F32)
    router_b = router_b.at[:, 0, :N_GROUPS].set(moe_b_grp).at[:, 0, LANES:LANES + N_EXPERTS].set(moe_b_exp)
    W = dict(w_in_even=w_in_even.astype(BF16), w_out_even=w_out_even.astype(BF16),
             w_in_odd=jnp.pad(w_in_odd.astype(BF16), ((0, 0), (0, 0), (0, D_IN_ODD_PAD - D_IN_ODD))),
             w_out_odd=w_out_odd.astype(BF16))

    mod = ada_mod(jnp.concatenate([c_prompt, c_sample], axis=0), ada_w, ada_b)
    mod = mod.reshape(DEPTH, bp + bs, 6, d).transpose(0, 2, 1, 3)
    grp_p = _Group(n_p, tp, 512, mod[:, :, :bp, None, :], per_row=False)
    grp_s = _Group(n_s, ts, n_s, mod[:, :, bp:], per_row=True)

    zeros = lambda *s: jnp.zeros(s, F32)
    st_p = dict(conv=zeros(N_EVEN, bp, CONV_WIDTH - 1, D_CONV), hgrn=zeros(N_EVEN, bp, HGRN_HEADS, HGRN_DK, HGRN_DV),
                k=zeros(N_ODD, bp, WINDOW, ATTN_KV_HEADS, HEAD_DIM), v=zeros(N_ODD, bp, WINDOW, ATTN_KV_HEADS, HEAD_DIM),
                rwkv=zeros(N_ODD, bp, RWKV_HEADS, RWKV_N, RWKV_N), shift=zeros(N_ODD, bp, D_SHIFT))
    st_s = dict(conv=state_conv, hgrn=state_hgrn, k=cache_swa_k, v=cache_swa_v, rwkv=state_rwkv, shift=state_shift)
    new_p = {k: [] for k in st_p}
    new_s = {k: [] for k in st_s}

    n_all = n_p + n_s
    t_all = jnp.arange(n_all, dtype=I32)
    row_of = lambda k: jnp.where(t_all < n_p, k * n_p + t_all, 2 * n_p + k * n_s + (t_all - n_p))
    dst_of_entry = jnp.stack([row_of(0), row_of(1)], axis=1).reshape(-1)

    x_p, x_s = x_prompt.reshape(n_p, d), x_sample.reshape(n_s, d)
    h_all = zeros(n_all, d // LANES, LANES)
    h_p, h_s = modulate(x_p, grp_p, 0, 1, 0), modulate(x_s, grp_s, 0, 1, 0)
    for l in range(DEPTH):
        a_p, b_p, wa, wb, np_l = _mix_layer(l, h_p, x_p, grp_p, 0, W, P, st_p, single=False)
        a_s, b_s, _, _, ns_l = _mix_layer(l, h_s, x_s, grp_s, PAST_LEN, W, P, st_s, single=True)
        for k, v in np_l.items():
            new_p[k].append(v)
        for k, v in ns_l.items():
            new_s[k].append(v)
        x_p, h_all, eid_p, gate_p, rank_p, cnt = out_proj_ln(
            a_p, b_p, wa, wb, x_p, grp_p, l, ln_g[l, 0], ln_b[l, 0], router_w[l], router_b[l], zeros(1, LANES), 0, h_all)
        x_s, h_all, eid_s, gate_s, rank_s, cnt = out_proj_ln(
            a_s, b_s, wa, wb, x_s, grp_s, l, ln_g[l, 0], ln_b[l, 0], router_w[l], router_b[l], cnt, n_p, h_all)
        tok, dst, be, n_used, n_blocks = moe_plan(jnp.concatenate([eid_p, eid_s]), jnp.concatenate([rank_p, rank_s]),
                                                  cnt, dst_of_entry, 2 * n_all)
        ys = moe_ffn(h_all, tok, dst, be, n_used, n_blocks, moe_w1, moe_w3, moe_w2, l)
        last = l + 1 == DEPTH
        x_p, h_p = moe_combine_ln(ys, 0, n_p, gate_p, x_p, grp_p, l, ln_g[l, 1], ln_b[l, 1], has_next=not last)
        x_s, h_s = moe_combine_ln(ys, 2 * n_p, 2 * n_p + n_s, gate_s, x_s, grp_s, l, ln_g[l, 1], ln_b[l, 1],
                                  has_next=not last)
    order = ('conv', 'hgrn', 'k', 'v', 'rwkv', 'shift')
    return ((x_p.reshape(bp, tp, d), x_s.reshape(bs, ts, d))
            + tuple(jnp.stack(new_p[k]) for k in order) + tuple(jnp.stack(new_s[k]) for k in order))
```

```python
import functools
import math

import jax
import jax.numpy as jnp
from jax import lax
from jax.experimental import pallas as pl
from jax.experimental.pallas import tpu as pltpu

F32 = jnp.float32
BF16 = jnp.bfloat16
I32 = jnp.int32

D_MODEL = 2048
DEPTH = 4
PAST_LEN = 16384
N_EVEN = (DEPTH + 1) // 2
N_ODD = DEPTH // 2
D_CONV = 1024
CONV_WIDTH = 3
HGRN_HEADS = 8
HGRN_DK = 128
HGRN_DV = 128
D_HK = HGRN_HEADS * HGRN_DK
D_HV = HGRN_HEADS * HGRN_DV
ATTN_HEADS = 16
ATTN_KV_HEADS = 4
HEAD_DIM = 64
WINDOW = 128
ROPE_THETA = 10000.0
D_Q = ATTN_HEADS * HEAD_DIM
D_KV = ATTN_KV_HEADS * HEAD_DIM
RWKV_HEADS = 16
RWKV_N = 64
D_RWKV = RWKV_HEADS * RWKV_N
RWKV_RANK = 64
RWKV_GN_EPS = 64e-5
D_SHIFT = 3 * D_RWKV + 3 * RWKV_RANK
D_IN_EVEN = 3 * D_CONV + 2 * D_HK + 2 * D_HV
D_IN_ODD = D_Q + 2 * D_KV + D_SHIFT
N_GROUPS = 4
EXPERTS_PER_GROUP = 8
N_EXPERTS = N_GROUPS * EXPERTS_PER_GROUP
D_EXPERT = 512
ALPHA = (2 * DEPTH) ** 0.25
LN_EPS = 1e-5
RMS_EPS = 1e-6

LANES = 128
HGRN_SUB = 16
MOE_ROWS = 128
VMEM_LIMIT = 48 * 1024 * 1024

D_IN_ODD_PAD = -(-D_IN_ODD // LANES) * LANES
D_SHIFT_PAD = D_IN_ODD_PAD - (D_Q + 2 * D_KV)

NN = (((1,), (0,)), ((), ()))
NT = (((1,), (1,)), ((), ()))
TN = (((0,), (0,)), ((), ()))


def _cp(sem, vmem=VMEM_LIMIT):
    return pltpu.CompilerParams(dimension_semantics=sem, vmem_limit_bytes=vmem)


def _dot(a, b, dims=NN):
    return lax.dot_general(a.astype(BF16), b.astype(BF16), dims, preferred_element_type=F32)


def _split2(x):
    hi = x.astype(BF16)
    lo = (x - hi.astype(F32)).astype(BF16)
    return hi, lo


def _dot3(a, b, dims=NN):
    ah, al = _split2(a)
    bh, bl = _split2(b)
    d = lambda x, y: lax.dot_general(x, y, dims, preferred_element_type=F32)
    return d(ah, bh) + (d(ah, bl) + d(al, bh))


def _dot_exact_lhs(a_bf16, b, dims=NN):
    b1 = b.astype(BF16)
    r1 = b - b1.astype(F32)
    b2 = r1.astype(BF16)
    b3 = (r1 - b2.astype(F32)).astype(BF16)
    d = lambda y: lax.dot_general(a_bf16, y, dims, preferred_element_type=F32)
    return d(b1) + (d(b2) + d(b3))


def _dot_exact_rhs(a, b_bf16, dims=NN):
    a1 = a.astype(BF16)
    r1 = a - a1.astype(F32)
    a2 = r1.astype(BF16)
    d = lambda x: lax.dot_general(x, b_bf16, dims, preferred_element_type=F32)
    return d(a1) + d(a2)


def _sigmoid(x):
    return 1.0 / (1.0 + jnp.exp(-x))


def _silu(x):
    return x * _sigmoid(x)


def _softplus(x):
    return jnp.maximum(x, 0.0) + jnp.log(1.0 + jnp.exp(-jnp.abs(x)))


def _iota(shape, axis):
    return lax.broadcasted_iota(I32, shape, axis)


def _layer_norm(u, g, b):
    mu = jnp.mean(u, axis=-1, keepdims=True)
    d = u - mu
    var = jnp.mean(d * d, axis=-1, keepdims=True)
    return d * lax.rsqrt(var + LN_EPS) * g + b


def _ada_body(c_ref, w_ref, b_ref, o_ref):
    o_ref[...] = _dot(_silu(c_ref[...]), w_ref[...]) + b_ref[...]


def ada_mod(c_all, ada_w, ada_b, tn=1024):
    nl, d, n = ada_w.shape
    r = c_all.shape[0]
    return pl.pallas_call(
        _ada_body,
        grid=(nl, n // tn),
        in_specs=[pl.BlockSpec((r, d), lambda l, j: (0, 0)),
                  pl.BlockSpec((None, d, tn), lambda l, j: (l, 0, j)),
                  pl.BlockSpec((None, 1, tn), lambda l, j: (l, 0, j))],
        out_specs=pl.BlockSpec((None, r, tn), lambda l, j: (l, 0, j)),
        out_shape=jax.ShapeDtypeStruct((nl, r, n), F32),
        compiler_params=_cp(("parallel", "parallel")),
        name="ada_mod",
    )(c_all, ada_w, ada_b.reshape(nl, 1, n))


class _Group:
    def __init__(self, n, t, tm, modarr, per_row):
        self.n, self.t, self.tm, self.modarr, self.per_row = n, t, tm, modarr, per_row

    def mod(self, l, c):
        d = self.modarr.shape[-1]
        if self.per_row:
            return pl.BlockSpec((None, None, self.tm, d), lambda i: (l, c, i, 0))
        t, tm = self.t, self.tm
        return pl.BlockSpec((None, None, None, 1, d), lambda i: (l, c, (i * tm) // t, 0, 0))


def _modulate_body(x_ref, sc_ref, sh_ref, h_ref):
    h_ref[...] = (x_ref[...] * (1.0 + sc_ref[...]) + sh_ref[...]).astype(h_ref.dtype)


def modulate(x, grp, l, c_sc, c_sh):
    n, d = x.shape
    tm = grp.tm
    row = pl.BlockSpec((tm, d), lambda i: (i, 0))
    return pl.pallas_call(
        _modulate_body, grid=(n // tm,),
        in_specs=[row, grp.mod(l, c_sc), grp.mod(l, c_sh)],
        out_specs=row, out_shape=jax.ShapeDtypeStruct((n, d), BF16),
        compiler_params=_cp(("parallel",)), name="modulate",
    )(x, grp.modarr, grp.modarr)


def _mm_body(a_ref, w_ref, o_ref):
    o_ref[...] = jnp.dot(a_ref[...], w_ref[...], preferred_element_type=F32)


def matmul(a, w, tm, tn):
    n, k = a.shape
    nn = w.shape[1]
    return pl.pallas_call(
        _mm_body, grid=(n // tm, nn // tn),
        in_specs=[pl.BlockSpec((tm, k), lambda i, j: (i, 0)),
                  pl.BlockSpec((k, tn), lambda i, j: (0, j))],
        out_specs=pl.BlockSpec((tm, tn), lambda i, j: (i, j)),
        out_shape=jax.ShapeDtypeStruct((n, nn), F32),
        compiler_params=_cp(("parallel", "parallel")), name="in_proj",
    )(a, w)


def _outln_body(a_ref, b_ref, wa_ref, wb_ref, x_ref, gt_ref, g_ref, be_ref, sc_ref, sh_ref, *rest):
    xo_ref, ho_ref = rest[-2:]
    y = (jnp.dot(a_ref[...], wa_ref[...], preferred_element_type=F32)
         + jnp.dot(b_ref[...], wb_ref[...], preferred_element_type=F32))
    xn = _layer_norm(ALPHA * x_ref[...] + (1.0 + gt_ref[...]) * y, g_ref[...], be_ref[...])
    xo_ref[...] = xn
    ho_ref[...] = xn * (1.0 + sc_ref[...]) + sh_ref[...]


def out_proj_ln(a, b, wa, wb, x, grp, l, ln_g, ln_b, h_rows, h_row0, h_all=None):
    n, d = x.shape
    ka, kb = a.shape[1], b.shape[1]
    tm = min(grp.tm, 256)
    g2 = _Group(grp.n, grp.t, tm, grp.modarr, grp.per_row)
    row = pl.BlockSpec((tm, d), lambda i: (i, 0))
    vec = pl.BlockSpec((1, d), lambda i: (0, 0))
    in_specs = [pl.BlockSpec((tm, ka), lambda i: (i, 0)), pl.BlockSpec((tm, kb), lambda i: (i, 0)),
                pl.BlockSpec((ka, d), lambda i: (0, 0)), pl.BlockSpec((kb, d), lambda i: (0, 0)),
                row, g2.mod(l, 2), vec, vec, g2.mod(l, 4), g2.mod(l, 3)]
    args = [a, b, wa, wb, x, grp.modarr, ln_g.reshape(1, d), ln_b.reshape(1, d), grp.modarr, grp.modarr]
    aliases = {}
    if h_all is not None:
        in_specs.append(pl.BlockSpec(memory_space=pl.ANY))
        args.append(h_all)
        aliases = {len(args) - 1: 1}
    return pl.pallas_call(
        _outln_body, grid=(n // tm,),
        in_specs=in_specs,
        out_specs=[row, pl.BlockSpec((tm, d), lambda i: (i + h_row0 // tm, 0))],
        out_shape=[jax.ShapeDtypeStruct((n, d), F32), jax.ShapeDtypeStruct((h_rows, d), F32)],
        input_output_aliases=aliases,
        compiler_params=_cp(("parallel",)), name="out_proj_ln",
    )(*args)


def _comb_body(*refs, has_next):
    if has_next:
        y0_ref, y1_ref, gate_ref, x_ref, gt_ref, g_ref, be_ref, sc_ref, sh_ref, xo_ref, ho_ref = refs
    else:
        y0_ref, y1_ref, gate_ref, x_ref, gt_ref, g_ref, be_ref, xo_ref = refs
    gate = gate_ref[...]
    y = y0_ref[...] * gate[:, 0:1] + y1_ref[...] * gate[:, 1:2]
    xn = _layer_norm(ALPHA * x_ref[...] + (1.0 + gt_ref[...]) * y, g_ref[...], be_ref[...])
    xo_ref[...] = xn
    if has_next:
        ho_ref[...] = (xn * (1.0 + sc_ref[...]) + sh_ref[...]).astype(ho_ref.dtype)


def moe_combine_ln(ys, row0, row1, gate, x, grp, l, ln_g, ln_b, has_next):
    n, d = x.shape
    tm = min(grp.tm, 256)
    g2 = _Group(grp.n, grp.t, tm, grp.modarr, grp.per_row)
    nt = n // tm
    row = pl.BlockSpec((tm, d), lambda i: (i, 0))
    vec = pl.BlockSpec((1, d), lambda i: (0, 0))
    in_specs = [pl.BlockSpec((tm, d), lambda i: (i + row0 // tm, 0)),
                pl.BlockSpec((tm, d), lambda i: (i + row1 // tm, 0)), pl.BlockSpec((tm, 2), lambda i: (i, 0)),
                row, g2.mod(l, 5), vec, vec]
    args = [ys, ys, gate, x, grp.modarr, ln_g.reshape(1, d), ln_b.reshape(1, d)]
    out_specs = [row]
    out_shape = [jax.ShapeDtypeStruct((n, d), F32)]
    if has_next:
        in_specs += [g2.mod(l + 1, 1), g2.mod(l + 1, 0)]
        args += [grp.modarr, grp.modarr]
        out_specs.append(row)
        out_shape.append(jax.ShapeDtypeStruct((n, d), BF16))
    res = pl.pallas_call(
        functools.partial(_comb_body, has_next=has_next), grid=(nt,),
        in_specs=in_specs, out_specs=out_specs, out_shape=out_shape,
        compiler_params=_cp(("parallel",)), name="moe_combine_ln",
    )(*args)
    return (res[0], res[1]) if has_next else (res[0], None)


def _router_body(h_ref, w_ref, b_ref, cnt0_ref, eid_ref, gate_ref, rank_ref, cnt_ref, run_ref):
    i = pl.program_id(0)

    @pl.when(i == 0)
    def _():
        run_ref[...] = cnt0_ref[...]

    tm = h_ref.shape[0]
    logits = _dot3(h_ref[...], w_ref[...]) + b_ref[...]
    lg = logits[:, :LANES]
    le = logits[:, LANES:]
    lane = _iota((tm, LANES), 1).astype(F32)
    neg = jnp.float32(-jnp.inf)
    lgm = jnp.where(lane < N_GROUPS, lg, neg)
    mg = jnp.max(lgm, axis=-1, keepdims=True)
    gidx = jnp.min(jnp.where(lgm == mg, lane, float(LANES)), axis=-1, keepdims=True)
    p_g = 1.0 / jnp.sum(jnp.exp(lgm - mg), axis=-1, keepdims=True)
    lo = gidx * EXPERTS_PER_GROUP
    in_grp = (lane >= lo) & (lane < lo + EXPERTS_PER_GROUP)
    lem = jnp.where(in_grp, le, neg)
    v1 = jnp.max(lem, axis=-1, keepdims=True)
    i1 = jnp.min(jnp.where(lem == v1, lane, float(LANES)), axis=-1, keepdims=True)
    lem2 = jnp.where(lane == i1, neg, lem)
    v2 = jnp.max(lem2, axis=-1, keepdims=True)
    i2 = jnp.min(jnp.where(lem2 == v2, lane, float(LANES)), axis=-1, keepdims=True)
    e2 = jnp.exp(v2 - v1)
    g1 = p_g / (1.0 + e2)
    g2 = p_g * e2 / (1.0 + e2)
    oh1 = jnp.where(lane == i1, 1.0, 0.0)
    oh2 = jnp.where(lane == i2, 1.0, 0.0)
    comb = oh1 + oh2
    tri = jnp.where(_iota((tm, tm), 0) > _iota((tm, tm), 1), 1.0, 0.0).astype(BF16)
    before = jnp.dot(tri, comb.astype(BF16), preferred_element_type=F32) + run_ref[...]
    r1 = jnp.sum(before * oh1, axis=-1, keepdims=True)
    r2 = jnp.sum(before * oh2, axis=-1, keepdims=True)
    run_ref[...] = run_ref[...] + jnp.sum(comb, axis=0, keepdims=True)
    eid_ref[:, 0:1] = i1.astype(I32)
    eid_ref[:, 1:2] = i2.astype(I32)
    gate_ref[:, 0:1] = g1
    gate_ref[:, 1:2] = g2
    rank_ref[:, 0:1] = r1.astype(I32)
    rank_ref[:, 1:2] = r2.astype(I32)
    cnt_ref[...] = run_ref[...]


def router(h, row0, n, wr, br, tm, cnt0):
    d = h.shape[1]
    two = pl.BlockSpec((tm, 2), lambda i: (i, 0))
    one = pl.BlockSpec((1, LANES), lambda i: (0, 0))
    return pl.pallas_call(
        _router_body, grid=(n // tm,),
        in_specs=[pl.BlockSpec((tm, d), lambda i: (i + row0 // tm, 0)),
                  pl.BlockSpec((d, 2 * LANES), lambda i: (0, 0)),
                  pl.BlockSpec((1, 2 * LANES), lambda i: (0, 0)), one],
        out_specs=[two, two, two, one],
        out_shape=[jax.ShapeDtypeStruct((n, 2), I32), jax.ShapeDtypeStruct((n, 2), F32),
                   jax.ShapeDtypeStruct((n, 2), I32), jax.ShapeDtypeStruct((1, LANES), F32)],
        scratch_shapes=[pltpu.VMEM((1, LANES), F32)],
        compiler_params=_cp(("arbitrary",)), name="router",
    )(h, wr, br, cnt0)


def moe_plan(eid, rank, cnt, dst_of_entry, spare_row0):
    nk = 2 * eid.shape[0]
    n_blocks = -(-nk // MOE_ROWS) + N_EXPERTS
    n_rows = n_blocks * MOE_ROWS
    counts = cnt[0, :N_EXPERTS].astype(I32)
    padded = (counts + MOE_ROWS - 1) // MOE_ROWS * MOE_ROWS
    pad_end = jnp.cumsum(padded)
    pad_start = pad_end - padded
    dest = (pad_start[eid] + rank).reshape(-1)
    src = jnp.full((n_rows + MOE_ROWS,), -1, I32).at[dest].set(jnp.arange(nk, dtype=I32), unique_indices=True,
                                                               mode='promise_in_bounds')
    valid = src >= 0
    pos = jnp.arange(n_rows + MOE_ROWS, dtype=I32)
    safe = jnp.maximum(src, 0)
    tok = jnp.where(valid, safe >> 1, 0)
    dst = jnp.where(valid, dst_of_entry[safe], spare_row0 + pos % MOE_ROWS)
    dst = jnp.concatenate([spare_row0 + jnp.arange(MOE_ROWS, dtype=I32), dst])
    n_used = pad_end[-1] // MOE_ROWS
    blk = jnp.arange(n_blocks, dtype=I32)
    be = jnp.minimum(jnp.sum((blk * MOE_ROWS)[:, None] >= pad_end[None, :], axis=-1), N_EXPERTS - 1).astype(I32)
    be = jnp.where(blk < n_used, be, be[jnp.maximum(n_used - 1, 0)])
    first = jnp.concatenate([jnp.ones((1,), I32), (be[1:] != be[:-1]).astype(I32)])
    run = jnp.cumsum(first) - 1
    starts = jnp.where(first == 1, blk, n_blocks)
    nxt_start = jnp.concatenate([lax.cummin(starts, reverse=True)[1:], jnp.full((1,), n_blocks, I32)])
    nxt = jnp.where(nxt_start < n_used, be[jnp.minimum(nxt_start, n_blocks - 1)], -1)
    meta = jnp.stack([be, first, run & 1, nxt], axis=1).reshape(-1).astype(I32)
    return tok, dst, meta, n_used.reshape(1).astype(I32), n_blocks


def _moe_body(tok_ref, dst_ref, meta_ref, nu_ref, h_hbm, w1_hbm, w3_hbm, w2_hbm, ys_hbm,
              xbuf, ybuf, w1f, w3f, w2f, w1b, w3b, w2b, gsem, ssem, wsem, *, spare_row0, layer):
    b = pl.program_id(0)
    n_used = nu_ref[0]
    slot = b % 2
    expert, first, half, nxt = meta_ref[4 * b], meta_ref[4 * b + 1], meta_ref[4 * b + 2], meta_ref[4 * b + 3]

    def gather_rows(blk, sl):
        for i in range(MOE_ROWS):
            pltpu.make_async_copy(h_hbm.at[pl.ds(tok_ref[blk * MOE_ROWS + i], 1)], xbuf.at[sl, pl.ds(i, 1)],
                                  gsem.at[sl]).start()

    def wait_gather(sl):
        pltpu.make_async_copy(xbuf.at[sl], xbuf.at[sl], gsem.at[sl]).wait()

    def scatter_rows(blk, sl):
        for i in range(MOE_ROWS):
            pltpu.make_async_copy(ybuf.at[sl, pl.ds(i, 1)],
                                  ys_hbm.at[pl.ds(dst_ref[(blk + 1) * MOE_ROWS + i], 1)], ssem.at[sl]).start()

    def wait_scatter(sl):
        pltpu.make_async_copy(ybuf.at[sl], ybuf.at[sl], ssem.at[sl]).wait()

    def weight_copies(e, hf):
        return (pltpu.make_async_copy(w1_hbm.at[layer, e], w1f.at[hf], wsem.at[hf]),
                pltpu.make_async_copy(w3_hbm.at[layer, e], w3f.at[hf], wsem.at[hf]),
                pltpu.make_async_copy(w2_hbm.at[layer, e], w2f.at[hf], wsem.at[hf]))

    @pl.when(b == 0)
    def _():
        ybuf[...] = jnp.zeros_like(ybuf)
        gather_rows(0, 0)
        for c in weight_copies(expert, 0):
            c.start()

    @pl.when(b < n_used)
    def _():
        @pl.when(first == 1)
        def _():
            for c in weight_copies(expert, half):
                c.wait()

            @pl.when(nxt >= 0)
            def _():
                for c in weight_copies(nxt, 1 - half):
                    c.start()

            w1b[...] = w1f[half].astype(BF16)
            w3b[...] = w3f[half].astype(BF16)
            w2b[...] = w2f[half].astype(BF16)

        wait_gather(slot)
        x = xbuf[slot].astype(BF16)
        h1 = jnp.dot(x, w1b[...], preferred_element_type=F32)
        h3 = jnp.dot(x, w3b[...], preferred_element_type=F32)
        mid = (_silu(h1) * h3).astype(BF16)
        y = jnp.dot(mid, w2b[...], preferred_element_type=F32)
        gather_rows(b + 1, 1 - slot)
        scatter_rows(b - 1, 1 - slot)

        @pl.when(b >= 1)
        def _():
            wait_scatter(slot)

        ybuf[slot] = y

        @pl.when(b == n_used - 1)
        def _():
            scatter_rows(b, slot)
            wait_gather(1 - slot)
            wait_scatter(1 - slot)
            wait_scatter(slot)


def moe_ffn(h, tok, dst, meta, n_used, n_blocks, w1, w3, w2, layer):
    n, d = h.shape
    de = w1.shape[-1]
    anyspace = pl.BlockSpec(memory_space=pl.ANY)
    return pl.pallas_call(
        functools.partial(_moe_body, spare_row0=2 * n, layer=layer),
        grid_spec=pltpu.PrefetchScalarGridSpec(
            num_scalar_prefetch=4, grid=(n_blocks,),
            in_specs=[anyspace, anyspace, anyspace, anyspace],
            out_specs=anyspace,
            scratch_shapes=[pltpu.VMEM((2, MOE_ROWS, d), F32), pltpu.VMEM((2, MOE_ROWS, d), F32),
                            pltpu.VMEM((2, d, de), F32), pltpu.VMEM((2, d, de), F32), pltpu.VMEM((2, de, d), F32),
                            pltpu.VMEM((d, de), BF16), pltpu.VMEM((d, de), BF16), pltpu.VMEM((de, d), BF16),
                            pltpu.SemaphoreType.DMA((2,)), pltpu.SemaphoreType.DMA((2,)),
                            pltpu.SemaphoreType.DMA((2,))]),
        out_shape=jax.ShapeDtypeStruct((2 * n + MOE_ROWS, d), F32),
        compiler_params=_cp(("arbitrary",)), name="moe_ffn",
    )(tok, dst, meta, n_used, h, w1, w3, w2)


def _conv_body(va_ref, bg_ref, cg_ref, w_ref, buf_ref, o_ref, nb_ref):
    u = cg_ref[...] * va_ref[...]
    t = u.shape[0]
    row = _iota(u.shape, 0)
    b0 = buf_ref[0:1, :]
    b1 = buf_ref[1:2, :]
    u1 = jnp.where(row == 0, b1, pltpu.roll(u, 1, axis=0))
    u2 = jnp.where(row == 0, b0, jnp.where(row == 1, b1, pltpu.roll(u, 2, axis=0)))
    w = w_ref[...]
    y = w[0:1] * u2 + w[1:2] * u1 + w[2:3] * u
    o_ref[...] = (bg_ref[...] * y).astype(o_ref.dtype)
    nb_ref[...] = u[t - 2:t, :]


def conv_seq(z3, conv_w, buf, tc=256):
    bsz, t, _ = z3.shape
    nc = D_CONV // tc
    col = lambda off: pl.BlockSpec((None, t, tc), lambda b, j: (b, 0, off + j))
    return pl.pallas_call(
        _conv_body, grid=(bsz, nc),
        in_specs=[col(0), col(nc), col(2 * nc),
                  pl.BlockSpec((CONV_WIDTH, tc), lambda b, j: (0, j)),
                  pl.BlockSpec((None, 2, tc), lambda b, j: (b, 0, j))],
        out_specs=[pl.BlockSpec((None, t, tc), lambda b, j: (b, 0, j)),
                   pl.BlockSpec((None, 2, tc), lambda b, j: (b, 0, j))],
        out_shape=[jax.ShapeDtypeStruct((bsz, t, D_CONV), BF16), jax.ShapeDtypeStruct((bsz, 2, D_CONV), F32)],
        compiler_params=_cp(("parallel", "parallel")), name="conv_seq",
    )(z3, z3, z3, conv_w, buf)


def _conv_step_body(va_ref, bg_ref, cg_ref, w_ref, b0_ref, b1_ref, o_ref, u_ref):
    u = cg_ref[...] * va_ref[...]
    w = w_ref[...]
    y = w[0:1] * b0_ref[...] + w[1:2] * b1_ref[...] + w[2:3] * u
    o_ref[...] = (bg_ref[...] * y).astype(o_ref.dtype)
    u_ref[...] = u


def conv_step(z2, conv_w, buf):
    bsz = z2.shape[0]
    col = lambda j: pl.BlockSpec((bsz, D_CONV), lambda i: (0, j))
    full = pl.BlockSpec((bsz, D_CONV), lambda i: (0, 0))
    a, u = pl.pallas_call(
        _conv_step_body, grid=(1,),
        in_specs=[col(0), col(1), col(2), pl.BlockSpec((CONV_WIDTH, D_CONV), lambda i: (0, 0)), full, full],
        out_specs=[full, full],
        out_shape=[jax.ShapeDtypeStruct((bsz, D_CONV), BF16), jax.ShapeDtypeStruct((bsz, D_CONV), F32)],
        compiler_params=_cp(("arbitrary",)), name="conv_step",
    )(z2, z2, z2, conv_w, buf[:, 0], buf[:, 1])
    return a, jnp.stack([buf[:, 1], u], axis=1)


def _hgrn_body(q_ref, f_ref, i_ref, g_ref, lb_ref, nw_ref, s0_ref, o_ref, so_ref, st_ref,
               *, chunk, sub, nchunk, nhead, t_real, layer, indep):
    tstep = pl.program_id(2)
    tb = chunk * nchunk
    dk = HGRN_DK

    if not indep:
        @pl.when(tstep == 0)
        def _():
            for hd in range(nhead):
                st_ref[hd] = s0_ref[hd].T

    lbl = lb_ref[...]
    e = jnp.exp(lbl - jnp.max(lbl, axis=0, keepdims=True))
    sm = e / jnp.sum(e, axis=0, keepdims=True)
    lb = jnp.zeros((1, lbl.shape[1]), F32)
    for r in range(1, layer + 1):
        lb = lb + sm[r:r + 1]
    tri = jnp.where(_iota((chunk, chunk), 0) >= _iota((chunk, chunk), 1), 1.0, 0.0).astype(BF16)
    nw = nw_ref[...]
    neg = jnp.float32(-jnp.inf)

    fr = f_ref[...]
    v = i_ref[...]
    logf = jnp.log(lb + (1.0 - lb) * _sigmoid(fr))
    kin = (1.0 - lb) * _sigmoid(-fr)
    if t_real is not None:
        row = _iota(fr.shape, 0)
        live = ((row & (chunk - 1)) if indep else (tstep * tb + row)) < t_real
        logf = jnp.where(live, logf, 0.0)
        kin = jnp.where(live, kin, 0.0)
    qs = _silu(q_ref[...])
    causal = _iota((sub, sub, 1), 1) <= _iota((sub, sub, 1), 0)
    cuts = [(slice(c * chunk, (c + 1) * chunk), slice(hd * dk, (hd + 1) * dk))
            for c in range(nchunk) for hd in range(nhead)]
    bb_all = [_dot_exact_lhs(tri, logf[c * chunk:(c + 1) * chunk]) for c in range(nchunk)]
    bb_l = [bb_all[c][:, hd * dk:(hd + 1) * dk] for c in range(nchunk) for hd in range(nhead)]
    qs_l = [qs[rows, ls] for rows, ls in cuts]
    kin_l = [kin[rows, ls] for rows, ls in cuts]
    v_l = [v[rows, ls] for rows, ls in cuts]
    bl_l = [bb[chunk - 1:chunk] for bb in bb_l]
    kv_l = [_dot(vc, kc * jnp.exp(bl - bb), TN) for vc, kc, bl, bb in zip(v_l, kin_l, bl_l, bb_l)]
    intra_l = []
    for qc, kc, vc, bb in zip(qs_l, kin_l, v_l, bb_l):
        parts = []
        for blk in range(chunk // sub):
            lo = blk * sub
            qi, bi, ki, vi = qc[lo:lo + sub], bb[lo:lo + sub], kc[lo:lo + sub], vc[lo:lo + sub]
            d = bi[:, None, :] - bi[None, :, :]
            sc = jnp.sum(qi[:, None, :] * ki[None, :, :] * jnp.exp(jnp.where(causal, d, neg)), axis=-1)
            oi = _dot(sc, vi)
            if blk > 0:
                anchor = bb[lo - 1:lo]
                qt = qi * jnp.exp(bi - anchor)
                kt = kc[:lo] * jnp.exp(anchor - bb[:lo])
                oi = oi + _dot(_dot(qt, kt, NT), vc[:lo])
            parts.append(oi)
        intra_l.append(parts[0] if len(parts) == 1 else jnp.concatenate(parts, axis=0))
    if indep:
        st_l = [s0_ref[c, hd].T for c in range(nchunk) for hd in range(nhead)]
        for idx, (st, bl, kv) in enumerate(zip(st_l, bl_l, kv_l)):
            so_ref[idx // nhead, idx % nhead] = (st * jnp.exp(bl) + kv).T
    else:
        st_l = []
        cur = [st_ref[hd] for hd in range(nhead)]
        for c in range(nchunk):
            for hd in range(nhead):
                idx = c * nhead + hd
                st_l.append(cur[hd])
                cur[hd] = cur[hd] * jnp.exp(bl_l[idx]) + kv_l[idx]
        for hd in range(nhead):
            st_ref[hd] = cur[hd]
    o_l = [_dot(qc * jnp.exp(bb), sc, NT) + oi for qc, bb, sc, oi in zip(qs_l, bb_l, st_l, intra_l)]
    o_l = [o * lax.rsqrt(jnp.mean(o * o, axis=-1, keepdims=True) + RMS_EPS) * nw for o in o_l]
    rows_l = [o_l[c * nhead] if nhead == 1 else jnp.concatenate(o_l[c * nhead:(c + 1) * nhead], axis=1)
              for c in range(nchunk)]
    o = rows_l[0] if nchunk == 1 else jnp.concatenate(rows_l, axis=0)
    o_ref[...] = (o * _silu(g_ref[...])).astype(o_ref.dtype)

    if not indep:
        @pl.when(tstep == pl.num_programs(2) - 1)
        def _():
            for hd in range(nhead):
                so_ref[hd] = st_ref[hd].T


def hgrn_seq(z3, hgrn_lb, norm_w, s0, layer, chunk, nchunk, nhead=1, t_real=None, indep=False):
    bsz, t, _ = z3.shape
    tb = chunk * nchunk
    sb = None
    if indep:
        sb = nchunk
        z3 = z3.reshape(bsz // nchunk, tb, z3.shape[2])
        bsz, t = bsz // nchunk, tb
    wl = nhead * LANES
    groups = HGRN_HEADS // nhead
    qoff = 3 * D_CONV // wl
    col = lambda k: pl.BlockSpec((None, tb, wl), lambda b, h, s: (b, s, qoff + k * groups + h))
    st = pl.BlockSpec((sb, nhead, HGRN_DK, HGRN_DV), lambda b, h, s: (b, h, 0, 0))
    body = functools.partial(_hgrn_body, chunk=chunk, sub=min(HGRN_SUB, chunk), nchunk=nchunk, nhead=nhead, t_real=t_real,
                             layer=layer, indep=indep)
    nseq = bsz * (sb or 1)
    o, so = pl.pallas_call(
        body, grid=(bsz, groups, t // tb),
        in_specs=[col(0), col(1), col(2), col(3),
                  pl.BlockSpec((N_EVEN, wl), lambda b, h, s: (0, h)),
                  pl.BlockSpec((1, HGRN_DV), lambda b, h, s: (0, 0)), st],
        out_specs=[pl.BlockSpec((None, tb, wl), lambda b, h, s: (b, s, h)), st],
        out_shape=[jax.ShapeDtypeStruct((bsz, t, D_HV), BF16),
                   jax.ShapeDtypeStruct((nseq, HGRN_HEADS, HGRN_DK, HGRN_DV), F32)],
        scratch_shapes=[pltpu.VMEM((nhead, HGRN_DV, HGRN_DK), F32)],
        compiler_params=_cp(("parallel", "parallel", "arbitrary")), name="hgrn_seq",
    )(z3, z3, z3, z3, hgrn_lb, norm_w.reshape(1, HGRN_DV), s0)
    return o.reshape(nseq, chunk if indep else t, D_HV), so


def _attn_body(sink_ref, zq_ref, zkv_ref, cos_ref, sin_ref, ck_ref, cv_ref, o_ref, nk_ref, nv_ref, kp_ref, vp_ref,
               *, prev_valid, t_real, tq):
    i = pl.program_id(1)
    w = WINDOW

    @pl.when(i == 0)
    def _():
        kp_ref[...] = ck_ref[...]
        vp_ref[...] = cv_ref[...]

    cos = cos_ref[...]
    sin = sin_ref[...]

    def rope(x):
        width = x.shape[1]
        reps = width // LANES
        first = (_iota(x.shape, 1) & (HEAD_DIM - 1)) < (HEAD_DIM // 2)
        rot = jnp.where(first, pltpu.roll(x, width - HEAD_DIM // 2, axis=1), pltpu.roll(x, HEAD_DIM // 2, axis=1))
        return x * jnp.tile(cos, (1, reps)) + rot * jnp.tile(sin, (1, reps))

    kv = zkv_ref[...]
    qr = rope(zq_ref[...]) * (HEAD_DIM ** -0.5)
    kr = rope(kv[:, :D_KV])
    v = kv[:, D_KV:]
    kprev = kp_ref[...]
    vprev = vp_ref[...]

    grp = ATTN_HEADS // ATTN_KV_HEADS
    rows = _iota((grp * tq, w + tq), 0)
    cols = _iota((grp * tq, w + tq), 1)
    delta = (rows & (tq - 1)) + w - cols
    valid = (delta >= 0) & (delta <= w)
    if not prev_valid:
        valid = valid & (cols >= jnp.where(i > 0, 0, w))
    head_of_row = _iota((grp * tq, 1), 0) >> int(math.log2(tq))
    neg = jnp.float32(-jnp.inf)
    outs = []
    for g in range(ATTN_KV_HEADS):
        ls = slice(g * HEAD_DIM, (g + 1) * HEAD_DIM)
        kg = jnp.concatenate([kprev[:, ls], kr[:, ls]], axis=0)
        vg = jnp.concatenate([vprev[:, ls], v[:, ls]], axis=0)
        qg = jnp.concatenate([qr[:, (grp * g + hh) * HEAD_DIM:(grp * g + hh + 1) * HEAD_DIM] for hh in range(grp)],
                             axis=0)
        s = jnp.where(valid, _dot(qg, kg, NT), neg)
        sink = jnp.zeros((grp * tq, 1), F32)
        for hh in range(grp):
            sink = jnp.where(head_of_row == hh, sink_ref[grp * g + hh], sink)
        m = jnp.maximum(jnp.max(s, axis=-1, keepdims=True), sink)
        p = jnp.exp(s - m)
        p = p / (jnp.sum(p, axis=-1, keepdims=True) + jnp.exp(sink - m))
        og = _dot(p, vg)
        outs += [og[hh * tq:(hh + 1) * tq] for hh in range(grp)]
    o_ref[...] = jnp.concatenate(outs, axis=1).astype(o_ref.dtype)
    if t_real == w:
        kp_ref[...] = kr
        vp_ref[...] = v

    @pl.when(i == pl.num_programs(1) - 1)
    def _():
        if t_real == w:
            nk_ref[...] = kr
            nv_ref[...] = v
        else:
            last = _iota(kprev.shape, 0) == w - 1
            nk_ref[...] = jnp.where(last, kr[0:1], pltpu.roll(kprev, w - 1, axis=0))
            nv_ref[...] = jnp.where(last, v[0:1], pltpu.roll(vprev, w - 1, axis=0))


def attn_seq(z3, sinks, cos, sin, cache_k, cache_v, prev_valid, t_real, tq=WINDOW):
    bsz, t, _ = z3.shape
    nb = t // tq
    assert tq == WINDOW or nb == 1
    cache = pl.BlockSpec((None, WINDOW, D_KV), lambda b, i: (b, 0, 0))
    tab = pl.BlockSpec((tq, LANES), lambda b, i: (i, 0))
    body = functools.partial(_attn_body, prev_valid=prev_valid, t_real=t_real, tq=tq)
    return pl.pallas_call(
        body, grid=(bsz, nb),
        in_specs=[pl.BlockSpec(memory_space=pltpu.SMEM),
                  pl.BlockSpec((None, tq, D_Q), lambda b, i: (b, i, 0)),
                  pl.BlockSpec((None, tq, 2 * D_KV), lambda b, i: (b, i, D_Q // (2 * D_KV))),
                  tab, tab, cache, cache],
        out_specs=[pl.BlockSpec((None, tq, D_Q), lambda b, i: (b, i, 0)), cache, cache],
        out_shape=[jax.ShapeDtypeStruct((bsz, t, D_Q), BF16),
                   jax.ShapeDtypeStruct((bsz, WINDOW, D_KV), F32), jax.ShapeDtypeStruct((bsz, WINDOW, D_KV), F32)],
        scratch_shapes=[pltpu.VMEM((WINDOW, D_KV), F32), pltpu.VMEM((WINDOW, D_KV), F32)],
        compiler_params=_cp(("parallel", "arbitrary")), name="attn_seq",
    )(sinks, z3, z3, cos, sin, cache_k, cache_v)


def rope_tables(pos):
    half = HEAD_DIM // 2
    inv = jnp.exp(-math.log(ROPE_THETA) * jnp.arange(half, dtype=F32) / half)
    ang = pos.astype(F32)[:, None] * inv[None, :]
    c, s = jnp.cos(ang), jnp.sin(ang)
    cos = jnp.concatenate([c, c, c, c], axis=1)
    sin = jnp.concatenate([-s, s, -s, s], axis=1)
    return cos, sin


def _rwkv_body(r_ref, k_ref, v_ref, wa_ref, gd_ref, sr_ref, sk_ref, sv_ref, swa_ref, sgd_ref,
               mr_ref, mk_ref, mv_ref, mwa_ref, mgd_ref, w0_ref, w2_ref, a0_ref, a2_ref, g2_ref,
               kkp_ref, ka_ref, rk_ref, lg_ref, lbias_ref, s0_ref, o_ref, so_ref, st_ref, prev_ref, prevw_ref,
               *, chunk, nchunk, npair, t_real, indep):
    tstep = pl.program_id(2)
    n = RWKV_N
    ln = chunk
    tb = chunk * nchunk

    if not indep:
        @pl.when(tstep == 0)
        def _():
            st_ref[...] = s0_ref[...]
            prev_ref[0:1, :] = sr_ref[...]
            prev_ref[1:2, :] = sk_ref[...]
            prev_ref[2:3, :] = sv_ref[...]
            prevw_ref[0:1, :] = swa_ref[...]
            prevw_ref[1:2, :] = sgd_ref[...]

    def mix(x_ref, mu_ref, s_ref, p_ref, idx):
        x = x_ref[...]
        row = _iota(x.shape, 0)
        if indep:
            before = jnp.broadcast_to(s_ref[...], (nchunk, ln, x.shape[1])).reshape(tb, x.shape[1])
            shifted = jnp.where((row & (ln - 1)) == 0, before, pltpu.roll(x, 1, axis=0))
        else:
            shifted = jnp.where(row == 0, p_ref[idx:idx + 1, :], pltpu.roll(x, 1, axis=0))
            p_ref[idx:idx + 1, :] = x[tb - 1:tb, :]
        return x + mu_ref[...] * (shifted - x)

    r = mix(r_ref, mr_ref, sr_ref, prev_ref, 0)
    kr = mix(k_ref, mk_ref, sk_ref, prev_ref, 1)
    vr = mix(v_ref, mv_ref, sv_ref, prev_ref, 2)
    wa = mix(wa_ref, mwa_ref, swa_ref, prevw_ref, 0)
    gd = mix(gd_ref, mgd_ref, sgd_ref, prevw_ref, 1)[:, :RWKV_RANK]
    wd = wa[:, :RWKV_RANK]
    ad = wa[:, RWKV_RANK:]

    w_log = -_softplus(-(w0_ref[...] + _dot3(jnp.tanh(wd), w2_ref[...]))) - 0.5
    logw = -jnp.exp(w_log)
    a = _sigmoid(a0_ref[...] + _dot3(ad, a2_ref[...]))
    gate = _dot3(_sigmoid(gd), g2_ref[...])

    sh = int(math.log2(n))
    seg = jnp.where((_iota((LANES, LANES), 0) >> sh) == (_iota((LANES, LANES), 1) >> sh), 1.0, 0.0).astype(BF16)

    def segsum(x):
        tiles = [_dot_exact_rhs(x[:, i * LANES:(i + 1) * LANES], seg) for i in range(npair)]
        return tiles[0] if npair == 1 else jnp.concatenate(tiles, axis=1)

    kkv = kr * kkp_ref[...]
    kk = kkv / jnp.maximum(jnp.sqrt(segsum(kkv * kkv)), 1e-12)
    kf = kr * (1.0 + (a - 1.0) * ka_ref[...])
    bonus = segsum(r * kf * rk_ref[...]) * vr
    if t_real is not None:
        row = _iota(logw.shape, 0)
        live = ((row & (ln - 1)) if indep else (tstep * tb + row)) < t_real
        zero = jnp.zeros_like(logw)
        logw = jnp.where(live, logw, zero)
        kk = jnp.where(live, kk, zero)
        kf = jnp.where(live, kf, zero)
        vr_s = jnp.where(live, vr, zero)
    else:
        vr_s = vr

    tri = jnp.where(_iota((ln, ln), 0) >= _iota((ln, ln), 1), 1.0, 0.0).astype(BF16)
    rr = _iota((ln, ln), 0)
    cc = _iota((ln, ln), 1)
    strict = rr > cc
    incl = rr >= cc
    eye = jnp.where(rr == cc, 1.0, 0.0)
    eye_n = jnp.where(_iota((n, n), 0) == _iota((n, n), 1), 1.0, 0.0)

    nhead = 2 * npair
    am_l, rm_l, bp_l, kp_l, bl_l, kl_l, v_l, gl_l = [], [], [], [], [], [], [], []
    for ci in range(nchunk):
        rows = slice(ci * ln, (ci + 1) * ln)
        lw = logw[rows]
        c = _dot_exact_lhs(tri, lw)
        gam = jnp.exp(c)
        ginv = jnp.exp(-c)
        am = -kk[rows] * jnp.exp(c - lw)
        rm = r[rows] * gam
        bp = kk[rows] * a[rows] * ginv
        kp = kf[rows] * ginv
        g_last = gam[ln - 1:ln, :]
        bpl = bp * g_last
        kpl = kp * g_last
        vv = vr_s[rows]
        for hd in range(nhead):
            ls = slice(hd * n, (hd + 1) * n)
            am_l.append(am[:, ls]); rm_l.append(rm[:, ls]); bp_l.append(bp[:, ls]); kp_l.append(kp[:, ls])
            bl_l.append(bpl[:, ls]); kl_l.append(kpl[:, ls]); v_l.append(vv[:, ls]); gl_l.append(g_last[:, ls])
    each = lambda fn, *ls: [fn(*xs) for xs in zip(*ls)]
    pw_l = each(lambda am, rm, bp, kp: _dot3(jnp.concatenate([am, rm], axis=0),
                                             jnp.concatenate([bp, kp], axis=0), NT), am_l, rm_l, bp_l, kp_l)
    m_l = each(lambda pw: jnp.where(strict, pw[:ln, :ln], 0.0), pw_l)
    nm_l = each(lambda pw: jnp.where(strict, pw[:ln, ln:], 0.0), pw_l)
    qb_l = each(lambda pw: jnp.where(incl, pw[ln:, :ln], 0.0), pw_l)
    qk_l = each(lambda pw: jnp.where(incl, pw[ln:, ln:], 0.0), pw_l)
    nq_l = each(lambda nm, qk, v: _dot(jnp.concatenate([nm, qk], axis=0), v), nm_l, qk_l, v_l)
    nv_l = [x[:ln] for x in nq_l]
    qkv_l = [x[ln:] for x in nq_l]
    kv_l = each(lambda v, kl: _dot(v, kl, TN), v_l, kl_l)
    tinv_l = each(lambda m: eye + m, m_l)
    p_l = m_l
    for _ in range(int(math.log2(ln)) - 1):
        p_l = each(lambda p: _dot(p, p), p_l)
        tinv_l = each(lambda t, p: t + _dot(t, p), tinv_l, p_l)
    wu_l = each(lambda t, am, nv: _dot(t, jnp.concatenate([am, nv], axis=1)), tinv_l, am_l, nv_l)
    ac_l = each(lambda wu, bl: _dot(wu, bl, TN), wu_l, bl_l)
    a_l = each(lambda ac, gl: eye_n * gl + ac[:n], ac_l, gl_l)
    c_l = each(lambda ac, kv: ac[n:] + kv, ac_l, kv_l)
    ro_l = each(_dot, qb_l, wu_l)
    rt_l = each(lambda rm, ro: rm + ro[:, :n], rm_l, ro_l)
    o0_l = each(lambda ro, qkv: ro[:, n:] + qkv, ro_l, qkv_l)

    o_rows = []
    if indep:
        s_l = [s0_ref[ci, hd] for ci in range(nchunk) for hd in range(nhead)]
        o_all = each(lambda rt, s, o0: _dot(rt, s, NT) + o0, rt_l, s_l, o0_l)
        s_l = each(lambda s, am, cm: _dot(s, am) + cm, s_l, a_l, c_l)
        for ci in range(nchunk):
            o_rows.append(o_all[ci * nhead:(ci + 1) * nhead])
            for hd in range(nhead):
                so_ref[ci, hd] = s_l[ci * nhead + hd]
    else:
        s_l = [st_ref[hd] for hd in range(nhead)]
        for ci in range(nchunk):
            sl = slice(ci * nhead, (ci + 1) * nhead)
            o_rows.append(each(lambda rt, s, o0: _dot(rt, s, NT) + o0, rt_l[sl], s_l, o0_l[sl]))
            s_l = each(lambda s, am, cm: _dot(s, am) + cm, s_l, a_l[sl], c_l[sl])
        for hd in range(nhead):
            st_ref[hd] = s_l[hd]
    cols = [o_rows[0][hd] if nchunk == 1 else jnp.concatenate([o_rows[ci][hd] for ci in range(nchunk)], axis=0)
            for hd in range(nhead)]
    o = jnp.concatenate(cols, axis=1)
    mu_o = segsum(o) * (1.0 / n)
    dlt = o - mu_o
    var_o = segsum(dlt * dlt) * (1.0 / n)
    o = dlt * lax.rsqrt(var_o + RWKV_GN_EPS) * lg_ref[...] + lbias_ref[...]
    o_ref[...] = ((o + bonus) * gate).astype(o_ref.dtype)

    if not indep:
        @pl.when(tstep == pl.num_programs(2) - 1)
        def _():
            so_ref[...] = st_ref[...]


def rwkv_seq(z3, shift, P, j, s0, chunk, nchunk=1, npair=1, t_real=None, indep=False):
    bsz, t, _ = z3.shape
    tb = chunk * nchunk
    sb = None
    if indep:
        sb = nchunk
        z3 = z3.reshape(bsz // nchunk, tb, z3.shape[2])
        bsz, t = bsz // nchunk, tb
    wl = npair * LANES
    groups = D_RWKV // wl
    zoff = (D_Q + 2 * D_KV) // LANES
    nb = D_RWKV // LANES
    wa_blk = 3 * nb
    gd_blk = 3 * nb + 1
    zc = lambda off: pl.BlockSpec((None, tb, wl), lambda b, p, s: (b, s, (zoff + off) // npair + p))
    zw = lambda blk: pl.BlockSpec((None, tb, LANES), lambda b, p, s: (b, s, zoff + blk))
    sc = lambda off: pl.BlockSpec((sb, 1, wl), lambda b, p, s: (b, 0, off // npair + p))
    sw = lambda blk: pl.BlockSpec((sb, 1, LANES), lambda b, p, s: (b, 0, blk))
    mc = lambda off: pl.BlockSpec((1, wl), lambda b, p, s: (0, off // npair + p))
    mw = lambda blk: pl.BlockSpec((1, LANES), lambda b, p, s: (0, blk))
    vec = pl.BlockSpec((1, wl), lambda b, p, s: (0, p))
    lora = pl.BlockSpec((RWKV_RANK, wl), lambda b, p, s: (0, p))
    st = pl.BlockSpec((sb, 2 * npair, RWKV_N, RWKV_N), lambda b, p, s: (b, p, 0, 0))
    mu = jnp.pad(P['rwkv_mu'][j], (0, D_SHIFT_PAD - D_SHIFT)).reshape(1, D_SHIFT_PAD)
    row = lambda x: x.reshape(1, D_RWKV)
    body = functools.partial(_rwkv_body, chunk=chunk, nchunk=nchunk, npair=npair, t_real=t_real, indep=indep)
    nseq = bsz * (sb or 1)
    o, so = pl.pallas_call(
        body, grid=(bsz, groups, t // tb),
        in_specs=[zc(0), zc(nb), zc(2 * nb), zw(wa_blk), zw(gd_blk),
                  sc(0), sc(nb), sc(2 * nb), sw(wa_blk), sw(gd_blk),
                  mc(0), mc(nb), mc(2 * nb), mw(wa_blk), mw(gd_blk),
                  vec, lora, vec, lora, lora, vec, vec, vec, vec, vec, st],
        out_specs=[pl.BlockSpec((None, tb, wl), lambda b, p, s: (b, s, p)), st],
        out_shape=[jax.ShapeDtypeStruct((bsz, t, D_RWKV), BF16),
                   jax.ShapeDtypeStruct((nseq, RWKV_HEADS, RWKV_N, RWKV_N), F32)],
        scratch_shapes=[pltpu.VMEM((2 * npair, RWKV_N, RWKV_N), F32), pltpu.VMEM((8, wl), F32),
                        pltpu.VMEM((8, LANES), F32)],
        compiler_params=_cp(("parallel", "parallel", "arbitrary")), name="rwkv_seq",
    )(z3, z3, z3, z3, z3, shift, shift, shift, shift, shift, mu, mu, mu, mu, mu,
      row(P['rwkv_w0'][j]), P['rwkv_w2'][j], row(P['rwkv_a0'][j]), P['rwkv_a2'][j], P['rwkv_g2'][j],
      row(P['rwkv_kk'][j]), row(P['rwkv_ka'][j]), row(P['rwkv_rk'][j]), row(P['rwkv_lnx_g'][j]),
      row(P['rwkv_lnx_b'][j]), s0)
    return o.reshape(nseq, chunk if indep else t, D_RWKV), so


def _pad_time(z2, tp):
    return jnp.pad(z2[:, None, :], ((0, 0), (0, tp - 1), (0, 0)))


def _mix_layer(l, h, x, grp, pos0, W, P, st, single):
    n = x.shape[0]
    t = grp.t
    bsz = n // t
    tm = grp.tm
    j = l // 2
    step_pad = 8
    new = {}
    if l % 2 == 0:
        z = matmul(h, W['w_in_even'][j], tm, D_IN_EVEN // 4)
        if single:
            a_out, new['conv'] = conv_step(z, P['conv_w'][j], st['conv'][j])
            b3, new['hgrn'] = hgrn_seq(_pad_time(z, step_pad), P['hgrn_lb'], P['hgrn_norm'][j], st['hgrn'][j], j,
                                       chunk=step_pad, nchunk=4, nhead=4, t_real=1, indep=True)
            b_out = b3[:, 0]
        else:
            z3 = z.reshape(bsz, t, D_IN_EVEN)
            a3, new['conv'] = conv_seq(z3, P['conv_w'][j], st['conv'][j])
            b3, new['hgrn'] = hgrn_seq(z3, P['hgrn_lb'], P['hgrn_norm'][j], st['hgrn'][j], j, chunk=64, nchunk=8)
            a_out, b_out = a3.reshape(n, D_CONV), b3.reshape(n, D_HV)
        wa, wb = W['w_out_even'][j][:D_CONV], W['w_out_even'][j][D_CONV:]
    else:
        z = matmul(h, W['w_in_odd'][j], tm, D_IN_ODD_PAD // 2)
        shift_in = jnp.pad(st['shift'][j], ((0, 0), (0, D_SHIFT_PAD - D_SHIFT)))[:, None, :]
        kc = st['k'][j].reshape(bsz, WINDOW, D_KV)
        vc = st['v'][j].reshape(bsz, WINDOW, D_KV)
        if single:
            cos, sin = rope_tables(pos0 + jnp.arange(step_pad, dtype=I32))
            a3, nk, nv = attn_seq(_pad_time(z[:, :D_Q + 2 * D_KV], step_pad), P['attn_sinks'][j], cos, sin,
                                  kc, vc, prev_valid=True, t_real=1, tq=step_pad)
            b3, new['rwkv'] = rwkv_seq(_pad_time(z, step_pad), shift_in, P, j, st['rwkv'][j], chunk=step_pad,
                                       nchunk=4, npair=4, t_real=1, indep=True)
            a_out, b_out = a3[:, 0], b3[:, 0]
            new['shift'] = z[:, D_Q + 2 * D_KV:D_IN_ODD]
        else:
            z3 = z.reshape(bsz, t, D_IN_ODD_PAD)
            cos, sin = rope_tables(pos0 + jnp.arange(t, dtype=I32))
            a3, nk, nv = attn_seq(z3, P['attn_sinks'][j], cos, sin, kc, vc, prev_valid=False, t_real=WINDOW)
            b3, new['rwkv'] = rwkv_seq(z3, shift_in, P, j, st['rwkv'][j], chunk=64, nchunk=4, npair=2)
            a_out, b_out = a3.reshape(n, D_Q), b3.reshape(n, D_RWKV)
            new['shift'] = z3[:, t - 1, D_Q + 2 * D_KV:D_IN_ODD]
        new['k'] = nk.reshape(bsz, WINDOW, ATTN_KV_HEADS, HEAD_DIM)
        new['v'] = nv.reshape(bsz, WINDOW, ATTN_KV_HEADS, HEAD_DIM)
        wa, wb = W['w_out_odd'][j][:D_Q], W['w_out_odd'][j][D_Q:]
    return a_out, b_out, wa, wb, new


def kernel(x_prompt, x_sample, c_prompt, c_sample, state_conv, state_hgrn, cache_swa_k, cache_swa_v,
           state_rwkv, state_shift, ada_w, ada_b, ln_g, ln_b, w_in_even, w_out_even, conv_w, hgrn_lb,
           hgrn_norm, w_in_odd, w_out_odd, attn_sinks, rwkv_mu, rwkv_w0, rwkv_w2, rwkv_a0, rwkv_a2,
           rwkv_g2, rwkv_kk, rwkv_ka, rwkv_rk, rwkv_lnx_g, rwkv_lnx_b, moe_w_grp, moe_b_grp, moe_w_exp,
           moe_b_exp, moe_w1, moe_w3, moe_w2):
    P = dict(ln_g=ln_g, ln_b=ln_b, conv_w=conv_w, hgrn_lb=hgrn_lb, hgrn_norm=hgrn_norm, attn_sinks=attn_sinks,
             rwkv_mu=rwkv_mu, rwkv_w0=rwkv_w0, rwkv_w2=rwkv_w2, rwkv_a0=rwkv_a0, rwkv_a2=rwkv_a2,
             rwkv_g2=rwkv_g2, rwkv_kk=rwkv_kk, rwkv_ka=rwkv_ka, rwkv_rk=rwkv_rk.reshape(N_ODD, D_RWKV),
             rwkv_lnx_g=rwkv_lnx_g, rwkv_lnx_b=rwkv_lnx_b)
    bp, tp, d = x_prompt.shape
    bs, ts, _ = x_sample.shape
    n_p, n_s = bp * tp, bs * ts
    router_w = jnp.zeros((DEPTH, d, 2 * LANES), F32)
    router_w = router_w.at[:, :, :N_GROUPS].set(moe_w_grp).at[:, :, LANES:LANES + N_EXPERTS].set(moe_w_exp)
    router_b = jnp.zeros((DEPTH, 1, 2 * LANES), F32)
    router_b = router_b.at[:, 0, :N_GROUPS].set(moe_b_grp).at[:, 0, LANES:LANES + N_EXPERTS].set(moe_b_exp)
    W = dict(w_in_even=w_in_even.astype(BF16), w_out_even=w_out_even.astype(BF16),
             w_in_odd=jnp.pad(w_in_odd.astype(BF16), ((0, 0), (0, 0), (0, D_IN_ODD_PAD - D_IN_ODD))),
             w_out_odd=w_out_odd.astype(BF16))

    mod = ada_mod(jnp.concatenate([c_prompt, c_sample], axis=0), ada_w, ada_b)
    mod = mod.reshape(DEPTH, bp + bs, 6, d).transpose(0, 2, 1, 3)
    grp_p = _Group(n_p, tp, 512, mod[:, :, :bp, None, :], per_row=False)
    grp_s = _Group(n_s, ts, n_s, mod[:, :, bp:], per_row=True)

    zeros = lambda *s: jnp.zeros(s, F32)
    st_p = dict(conv=zeros(N_EVEN, bp, CONV_WIDTH - 1, D_CONV), hgrn=zeros(N_EVEN, bp, HGRN_HEADS, HGRN_DK, HGRN_DV),
                k=zeros(N_ODD, bp, WINDOW, ATTN_KV_HEADS, HEAD_DIM), v=zeros(N_ODD, bp, WINDOW, ATTN_KV_HEADS, HEAD_DIM),
                rwkv=zeros(N_ODD, bp, RWKV_HEADS, RWKV_N, RWKV_N), shift=zeros(N_ODD, bp, D_SHIFT))
    st_s = dict(conv=state_conv, hgrn=state_hgrn, k=cache_swa_k, v=cache_swa_v, rwkv=state_rwkv, shift=state_shift)
    new_p = {k: [] for k in st_p}
    new_s = {k: [] for k in st_s}

    n_all = n_p + n_s
    t_all = jnp.arange(n_all, dtype=I32)
    row_of = lambda k: jnp.where(t_all < n_p, k * n_p + t_all, 2 * n_p + k * n_s + (t_all - n_p))
    dst_of_entry = jnp.stack([row_of(0), row_of(1)], axis=1).reshape(-1)

    x_p, x_s = x_prompt.reshape(n_p, d), x_sample.reshape(n_s, d)
    h_all = zeros(n_all, d)
    h_p, h_s = modulate(x_p, grp_p, 0, 1, 0), modulate(x_s, grp_s, 0, 1, 0)
    for l in range(DEPTH):
        a_p, b_p, wa, wb, np_l = _mix_layer(l, h_p, x_p, grp_p, 0, W, P, st_p, single=False)
        a_s, b_s, _, _, ns_l = _mix_layer(l, h_s, x_s, grp_s, PAST_LEN, W, P, st_s, single=True)
        for k, v in np_l.items():
            new_p[k].append(v)
        for k, v in ns_l.items():
            new_s[k].append(v)
        x_p, h_all = out_proj_ln(a_p, b_p, wa, wb, x_p, grp_p, l, ln_g[l, 0], ln_b[l, 0], n_all, 0, h_all)
        x_s, h_all = out_proj_ln(a_s, b_s, wa, wb, x_s, grp_s, l, ln_g[l, 0], ln_b[l, 0], n_all, n_p, h_all)
        eid_p, gate_p, rank_p, cnt = router(h_all, 0, n_p, router_w[l], router_b[l], 256, zeros(1, LANES))
        eid_s, gate_s, rank_s, cnt = router(h_all, n_p, n_s, router_w[l], router_b[l], n_s, cnt)
        tok, dst, be, n_used, n_blocks = moe_plan(jnp.concatenate([eid_p, eid_s]), jnp.concatenate([rank_p, rank_s]),
                                                  cnt, dst_of_entry, 2 * n_all)
        ys = moe_ffn(h_all, tok, dst, be, n_used, n_blocks, moe_w1, moe_w3, moe_w2, l)
        last = l + 1 == DEPTH
        x_p, h_p = moe_combine_ln(ys, 0, n_p, gate_p, x_p, grp_p, l, ln_g[l, 1], ln_b[l, 1], has_next=not last)
        x_s, h_s = moe_combine_ln(ys, 2 * n_p, 2 * n_p + n_s, gate_s, x_s, grp_s, l, ln_g[l, 1], ln_b[l, 1],
                                  has_next=not last)
    order = ('conv', 'hgrn', 'k', 'v', 'rwkv', 'shift')
    return ((x_p.reshape(bp, tp, d), x_s.reshape(bs, ts, d))
            + tuple(jnp.stack(new_p[k]) for k in order) + tuple(jnp.stack(new_s[k]) for k in order))
```

```python
import functools
import math

import jax
import jax.numpy as jnp
from jax import lax
from jax.experimental import pallas as pl
from jax.experimental.pallas import tpu as pltpu

F32 = jnp.float32
BF16 = jnp.bfloat16
I32 = jnp.int32

D_MODEL = 2048
DEPTH = 4
PAST_LEN = 16384
N_EVEN = (DEPTH + 1) // 2
N_ODD = DEPTH // 2
D_CONV = 1024
CONV_WIDTH = 3
HGRN_HEADS = 8
HGRN_DK = 128
HGRN_DV = 128
D_HK = HGRN_HEADS * HGRN_DK
D_HV = HGRN_HEADS * HGRN_DV
ATTN_HEADS = 16
ATTN_KV_HEADS = 4
HEAD_DIM = 64
WINDOW = 128
ROPE_THETA = 10000.0
D_Q = ATTN_HEADS * HEAD_DIM
D_KV = ATTN_KV_HEADS * HEAD_DIM
RWKV_HEADS = 16
RWKV_N = 64
D_RWKV = RWKV_HEADS * RWKV_N
RWKV_RANK = 64
RWKV_GN_EPS = 64e-5
D_SHIFT = 3 * D_RWKV + 3 * RWKV_RANK
D_IN_EVEN = 3 * D_CONV + 2 * D_HK + 2 * D_HV
D_IN_ODD = D_Q + 2 * D_KV + D_SHIFT
N_GROUPS = 4
EXPERTS_PER_GROUP = 8
N_EXPERTS = N_GROUPS * EXPERTS_PER_GROUP
D_EXPERT = 512
ALPHA = (2 * DEPTH) ** 0.25
LN_EPS = 1e-5
RMS_EPS = 1e-6

LANES = 128
HGRN_SUB = 16
MOE_ROWS = 384
VMEM_LIMIT = 48 * 1024 * 1024
MOE_VMEM_LIMIT = 56 * 1024 * 1024

D_IN_ODD_PAD = -(-D_IN_ODD // LANES) * LANES
D_SHIFT_PAD = D_IN_ODD_PAD - (D_Q + 2 * D_KV)

NN = (((1,), (0,)), ((), ()))
NT = (((1,), (1,)), ((), ()))
TN = (((0,), (0,)), ((), ()))


def _cp(sem, vmem=VMEM_LIMIT):
    return pltpu.CompilerParams(dimension_semantics=sem, vmem_limit_bytes=vmem)


def _dot(a, b, dims=NN):
    return lax.dot_general(a.astype(BF16), b.astype(BF16), dims, preferred_element_type=F32)


def _split2(x):
    hi = x.astype(BF16)
    lo = (x - hi.astype(F32)).astype(BF16)
    return hi, lo


def _dot3(a, b, dims=NN):
    ah, al = _split2(a)
    bh, bl = _split2(b)
    d = lambda x, y: lax.dot_general(x, y, dims, preferred_element_type=F32)
    return d(ah, bh) + (d(ah, bl) + d(al, bh))


def _dot_exact_lhs(a_bf16, b, dims=NN):
    b1 = b.astype(BF16)
    r1 = b - b1.astype(F32)
    b2 = r1.astype(BF16)
    b3 = (r1 - b2.astype(F32)).astype(BF16)
    d = lambda y: lax.dot_general(a_bf16, y, dims, preferred_element_type=F32)
    return d(b1) + (d(b2) + d(b3))


def _dot_exact_rhs(a, b_bf16, dims=NN):
    a1 = a.astype(BF16)
    r1 = a - a1.astype(F32)
    a2 = r1.astype(BF16)
    d = lambda x: lax.dot_general(x, b_bf16, dims, preferred_element_type=F32)
    return d(a1) + d(a2)


def _sigmoid(x):
    return 1.0 / (1.0 + jnp.exp(-x))


def _silu(x):
    return x * _sigmoid(x)


def _softplus(x):
    return jnp.maximum(x, 0.0) + jnp.log(1.0 + jnp.exp(-jnp.abs(x)))


def _iota(shape, axis):
    return lax.broadcasted_iota(I32, shape, axis)


def _layer_norm(u, g, b):
    mu = jnp.mean(u, axis=-1, keepdims=True)
    d = u - mu
    var = jnp.mean(d * d, axis=-1, keepdims=True)
    return d * lax.rsqrt(var + LN_EPS) * g + b


def _ada_body(c_ref, w_ref, b_ref, o_ref):
    o_ref[...] = _dot(_silu(c_ref[...]), w_ref[...]) + b_ref[...]


def ada_mod(c_all, ada_w, ada_b, tn=1024):
    nl, d, n = ada_w.shape
    r = c_all.shape[0]
    return pl.pallas_call(
        _ada_body,
        grid=(nl, n // tn),
        in_specs=[pl.BlockSpec((r, d), lambda l, j: (0, 0)),
                  pl.BlockSpec((None, d, tn), lambda l, j: (l, 0, j)),
                  pl.BlockSpec((None, 1, tn), lambda l, j: (l, 0, j))],
        out_specs=pl.BlockSpec((None, r, tn), lambda l, j: (l, 0, j)),
        out_shape=jax.ShapeDtypeStruct((nl, r, n), F32),
        compiler_params=_cp(("parallel", "parallel")),
        name="ada_mod",
    )(c_all, ada_w, ada_b.reshape(nl, 1, n))


class _Group:
    def __init__(self, n, t, tm, modarr, per_row):
        self.n, self.t, self.tm, self.modarr, self.per_row = n, t, tm, modarr, per_row

    def mod(self, l, c):
        d = self.modarr.shape[-1]
        if self.per_row:
            return pl.BlockSpec((None, None, self.tm, d), lambda i: (l, c, i, 0))
        t, tm = self.t, self.tm
        return pl.BlockSpec((None, None, None, 1, d), lambda i: (l, c, (i * tm) // t, 0, 0))


def _modulate_body(x_ref, sc_ref, sh_ref, h_ref):
    h_ref[...] = (x_ref[...] * (1.0 + sc_ref[...]) + sh_ref[...]).astype(h_ref.dtype)


def modulate(x, grp, l, c_sc, c_sh):
    n, d = x.shape
    tm = grp.tm
    row = pl.BlockSpec((tm, d), lambda i: (i, 0))
    return pl.pallas_call(
        _modulate_body, grid=(n // tm,),
        in_specs=[row, grp.mod(l, c_sc), grp.mod(l, c_sh)],
        out_specs=row, out_shape=jax.ShapeDtypeStruct((n, d), BF16),
        compiler_params=_cp(("parallel",)), name="modulate",
    )(x, grp.modarr, grp.modarr)


def _mm_body(a_ref, w_ref, o_ref):
    o_ref[...] = jnp.dot(a_ref[...], w_ref[...], preferred_element_type=F32)


def matmul(a, w, tm, tn):
    n, k = a.shape
    nn = w.shape[1]
    return pl.pallas_call(
        _mm_body, grid=(n // tm, nn // tn),
        in_specs=[pl.BlockSpec((tm, k), lambda i, j: (i, 0)),
                  pl.BlockSpec((k, tn), lambda i, j: (0, j))],
        out_specs=pl.BlockSpec((tm, tn), lambda i, j: (i, j)),
        out_shape=jax.ShapeDtypeStruct((n, nn), F32),
        compiler_params=_cp(("parallel", "parallel")), name="in_proj",
    )(a, w)


def _outln_body(a_ref, b_ref, wa_ref, wb_ref, x_ref, gt_ref, g_ref, be_ref, sc_ref, sh_ref, *rest):
    xo_ref, ho_ref = rest[-2:]
    y = (jnp.dot(a_ref[...], wa_ref[...], preferred_element_type=F32)
         + jnp.dot(b_ref[...], wb_ref[...], preferred_element_type=F32))
    xn = _layer_norm(ALPHA * x_ref[...] + (1.0 + gt_ref[...]) * y, g_ref[...], be_ref[...])
    xo_ref[...] = xn
    ho_ref[...] = xn * (1.0 + sc_ref[...]) + sh_ref[...]


def out_proj_ln(a, b, wa, wb, x, grp, l, ln_g, ln_b, h_rows, h_row0, h_all=None):
    n, d = x.shape
    ka, kb = a.shape[1], b.shape[1]
    tm = min(grp.tm, 256)
    g2 = _Group(grp.n, grp.t, tm, grp.modarr, grp.per_row)
    row = pl.BlockSpec((tm, d), lambda i: (i, 0))
    vec = pl.BlockSpec((1, d), lambda i: (0, 0))
    in_specs = [pl.BlockSpec((tm, ka), lambda i: (i, 0)), pl.BlockSpec((tm, kb), lambda i: (i, 0)),
                pl.BlockSpec((ka, d), lambda i: (0, 0)), pl.BlockSpec((kb, d), lambda i: (0, 0)),
                row, g2.mod(l, 2), vec, vec, g2.mod(l, 4), g2.mod(l, 3)]
    args = [a, b, wa, wb, x, grp.modarr, ln_g.reshape(1, d), ln_b.reshape(1, d), grp.modarr, grp.modarr]
    aliases = {}
    if h_all is not None:
        in_specs.append(pl.BlockSpec(memory_space=pl.ANY))
        args.append(h_all)
        aliases = {len(args) - 1: 1}
    return pl.pallas_call(
        _outln_body, grid=(n // tm,),
        in_specs=in_specs,
        out_specs=[row, pl.BlockSpec((tm, d), lambda i: (i + h_row0 // tm, 0))],
        out_shape=[jax.ShapeDtypeStruct((n, d), F32), jax.ShapeDtypeStruct((h_rows, d), F32)],
        input_output_aliases=aliases,
        compiler_params=_cp(("parallel",)), name="out_proj_ln",
    )(*args)


def _comb_body(*refs, has_next):
    if has_next:
        y0_ref, y1_ref, gate_ref, x_ref, gt_ref, g_ref, be_ref, sc_ref, sh_ref, xo_ref, ho_ref = refs
    else:
        y0_ref, y1_ref, gate_ref, x_ref, gt_ref, g_ref, be_ref, xo_ref = refs
    gate = gate_ref[...]
    y = y0_ref[...] * gate[:, 0:1] + y1_ref[...] * gate[:, 1:2]
    xn = _layer_norm(ALPHA * x_ref[...] + (1.0 + gt_ref[...]) * y, g_ref[...], be_ref[...])
    xo_ref[...] = xn
    if has_next:
        ho_ref[...] = (xn * (1.0 + sc_ref[...]) + sh_ref[...]).astype(ho_ref.dtype)


def moe_combine_ln(ys, row0, row1, gate, x, grp, l, ln_g, ln_b, has_next):
    n, d = x.shape
    tm = min(grp.tm, 256)
    g2 = _Group(grp.n, grp.t, tm, grp.modarr, grp.per_row)
    nt = n // tm
    row = pl.BlockSpec((tm, d), lambda i: (i, 0))
    vec = pl.BlockSpec((1, d), lambda i: (0, 0))
    in_specs = [pl.BlockSpec((tm, d), lambda i: (i + row0 // tm, 0)),
                pl.BlockSpec((tm, d), lambda i: (i + row1 // tm, 0)), pl.BlockSpec((tm, 2), lambda i: (i, 0)),
                row, g2.mod(l, 5), vec, vec]
    args = [ys, ys, gate, x, grp.modarr, ln_g.reshape(1, d), ln_b.reshape(1, d)]
    out_specs = [row]
    out_shape = [jax.ShapeDtypeStruct((n, d), F32)]
    if has_next:
        in_specs += [g2.mod(l + 1, 1), g2.mod(l + 1, 0)]
        args += [grp.modarr, grp.modarr]
        out_specs.append(row)
        out_shape.append(jax.ShapeDtypeStruct((n, d), BF16))
    res = pl.pallas_call(
        functools.partial(_comb_body, has_next=has_next), grid=(nt,),
        in_specs=in_specs, out_specs=out_specs, out_shape=out_shape,
        compiler_params=_cp(("parallel",)), name="moe_combine_ln",
    )(*args)
    return (res[0], res[1]) if has_next else (res[0], None)


def _router_body(h_ref, w_ref, b_ref, cnt0_ref, eid_ref, gate_ref, rank_ref, cnt_ref, run_ref):
    i = pl.program_id(0)

    @pl.when(i == 0)
    def _():
        run_ref[...] = cnt0_ref[...]

    tm = h_ref.shape[0]
    logits = _dot3(h_ref[...], w_ref[...]) + b_ref[...]
    lg = logits[:, :LANES]
    le = logits[:, LANES:]
    lane = _iota((tm, LANES), 1).astype(F32)
    neg = jnp.float32(-jnp.inf)
    lgm = jnp.where(lane < N_GROUPS, lg, neg)
    mg = jnp.max(lgm, axis=-1, keepdims=True)
    gidx = jnp.min(jnp.where(lgm == mg, lane, float(LANES)), axis=-1, keepdims=True)
    p_g = 1.0 / jnp.sum(jnp.exp(lgm - mg), axis=-1, keepdims=True)
    lo = gidx * EXPERTS_PER_GROUP
    in_grp = (lane >= lo) & (lane < lo + EXPERTS_PER_GROUP)
    lem = jnp.where(in_grp, le, neg)
    v1 = jnp.max(lem, axis=-1, keepdims=True)
    i1 = jnp.min(jnp.where(lem == v1, lane, float(LANES)), axis=-1, keepdims=True)
    lem2 = jnp.where(lane == i1, neg, lem)
    v2 = jnp.max(lem2, axis=-1, keepdims=True)
    i2 = jnp.min(jnp.where(lem2 == v2, lane, float(LANES)), axis=-1, keepdims=True)
    e2 = jnp.exp(v2 - v1)
    g1 = p_g / (1.0 + e2)
    g2 = p_g * e2 / (1.0 + e2)
    oh1 = jnp.where(lane == i1, 1.0, 0.0)
    oh2 = jnp.where(lane == i2, 1.0, 0.0)
    comb = oh1 + oh2
    tri = jnp.where(_iota((tm, tm), 0) > _iota((tm, tm), 1), 1.0, 0.0).astype(BF16)
    before = jnp.dot(tri, comb.astype(BF16), preferred_element_type=F32) + run_ref[...]
    r1 = jnp.sum(before * oh1, axis=-1, keepdims=True)
    r2 = jnp.sum(before * oh2, axis=-1, keepdims=True)
    run_ref[...] = run_ref[...] + jnp.sum(comb, axis=0, keepdims=True)
    eid_ref[:, 0:1] = i1.astype(I32)
    eid_ref[:, 1:2] = i2.astype(I32)
    gate_ref[:, 0:1] = g1
    gate_ref[:, 1:2] = g2
    rank_ref[:, 0:1] = r1.astype(I32)
    rank_ref[:, 1:2] = r2.astype(I32)
    cnt_ref[...] = run_ref[...]


def router(h, row0, n, wr, br, tm, cnt0):
    d = h.shape[1]
    two = pl.BlockSpec((tm, 2), lambda i: (i, 0))
    one = pl.BlockSpec((1, LANES), lambda i: (0, 0))
    return pl.pallas_call(
        _router_body, grid=(n // tm,),
        in_specs=[pl.BlockSpec((tm, d), lambda i: (i + row0 // tm, 0)),
                  pl.BlockSpec((d, 2 * LANES), lambda i: (0, 0)),
                  pl.BlockSpec((1, 2 * LANES), lambda i: (0, 0)), one],
        out_specs=[two, two, two, one],
        out_shape=[jax.ShapeDtypeStruct((n, 2), I32), jax.ShapeDtypeStruct((n, 2), F32),
                   jax.ShapeDtypeStruct((n, 2), I32), jax.ShapeDtypeStruct((1, LANES), F32)],
        scratch_shapes=[pltpu.VMEM((1, LANES), F32)],
        compiler_params=_cp(("arbitrary",)), name="router",
    )(h, wr, br, cnt0)


def moe_plan(eid, rank, cnt, dst_of_entry, spare_row0):
    nk = 2 * eid.shape[0]
    n_blocks = -(-nk // MOE_ROWS) + N_EXPERTS
    n_rows = n_blocks * MOE_ROWS
    counts = cnt[0, :N_EXPERTS].astype(I32)
    padded = (counts + MOE_ROWS - 1) // MOE_ROWS * MOE_ROWS
    pad_end = jnp.cumsum(padded)
    pad_start = pad_end - padded
    dest = (pad_start[eid] + rank).reshape(-1)
    src = jnp.full((n_rows + MOE_ROWS,), -1, I32).at[dest].set(jnp.arange(nk, dtype=I32), unique_indices=True,
                                                               mode='promise_in_bounds')
    valid = src >= 0
    pos = jnp.arange(n_rows + MOE_ROWS, dtype=I32)
    safe = jnp.maximum(src, 0)
    tok = jnp.where(valid, safe >> 1, 0)
    dst = jnp.where(valid, dst_of_entry[safe], spare_row0 + pos % (2 * MOE_ROWS))
    dst = jnp.concatenate([spare_row0 + MOE_ROWS + jnp.arange(MOE_ROWS, dtype=I32), dst])
    n_used = pad_end[-1] // MOE_ROWS
    blk = jnp.arange(n_blocks, dtype=I32)
    be = jnp.minimum(jnp.sum((blk * MOE_ROWS)[:, None] >= pad_end[None, :], axis=-1), N_EXPERTS - 1).astype(I32)
    be = jnp.where(blk < n_used, be, be[jnp.maximum(n_used - 1, 0)])
    return tok, dst, be, n_used.reshape(1).astype(I32), n_blocks


def _moe_body(tok_ref, dst_ref, be_ref, nu_ref, h_hbm, w1_ref, w3_ref, w2_ref, ys_hbm,
              xbuf, ybuf, w1b, w3b, w2b, gsem, ssem, *, spare_row0):
    b = pl.program_id(0)
    n_used = nu_ref[0]
    slot = b % 2

    def gather_rows(blk, sl):
        for i in range(MOE_ROWS):
            pltpu.make_async_copy(h_hbm.at[pl.ds(tok_ref[blk * MOE_ROWS + i], 1)], xbuf.at[sl, pl.ds(i, 1)],
                                  gsem.at[sl]).start()

    def wait_gather(sl):
        pltpu.make_async_copy(xbuf.at[sl], xbuf.at[sl], gsem.at[sl]).wait()

    def scatter_rows(blk, sl):
        for i in range(MOE_ROWS):
            pltpu.make_async_copy(ybuf.at[sl, pl.ds(i, 1)],
                                  ys_hbm.at[pl.ds(dst_ref[(blk + 1) * MOE_ROWS + i], 1)], ssem.at[sl]).start()

    def wait_scatter(sl):
        pltpu.make_async_copy(ybuf.at[sl], ybuf.at[sl], ssem.at[sl]).wait()

    @pl.when(b == 0)
    def _():
        ybuf[...] = jnp.zeros_like(ybuf)
        first_spare = pltpu.make_async_copy(ybuf.at[0], ys_hbm.at[pl.ds(spare_row0, MOE_ROWS)], ssem.at[0])
        first_spare.start()
        first_spare.wait()
        gather_rows(0, 0)

    @pl.when(b < n_used)
    def _():
        @pl.when(jnp.logical_or(b == 0, be_ref[b] != be_ref[jnp.maximum(b - 1, 0)]))
        def _():
            w1b[...] = w1_ref[...].astype(BF16)
            w3b[...] = w3_ref[...].astype(BF16)
            w2b[...] = w2_ref[...].astype(BF16)

        wait_gather(slot)
        x = xbuf[slot].astype(BF16)
        h1 = jnp.dot(x, w1b[...], preferred_element_type=F32)
        h3 = jnp.dot(x, w3b[...], preferred_element_type=F32)
        mid = (_silu(h1) * h3).astype(BF16)
        y = jnp.dot(mid, w2b[...], preferred_element_type=F32)
        gather_rows(b + 1, 1 - slot)
        scatter_rows(b - 1, 1 - slot)

        @pl.when(b >= 1)
        def _():
            wait_scatter(slot)

        ybuf[slot] = y

        @pl.when(b == n_used - 1)
        def _():
            scatter_rows(b, slot)
            wait_gather(1 - slot)
            wait_scatter(1 - slot)
            wait_scatter(slot)


def moe_ffn(h, tok, dst, be, n_used, n_blocks, w1, w3, w2, layer):
    n, d = h.shape
    de = w1.shape[-1]
    wspec_in = pl.BlockSpec((None, None, d, de), lambda b, tok, dst, be, nu: (layer, be[b], 0, 0))
    wspec_out = pl.BlockSpec((None, None, de, d), lambda b, tok, dst, be, nu: (layer, be[b], 0, 0))
    return pl.pallas_call(
        functools.partial(_moe_body, spare_row0=2 * n),
        grid_spec=pltpu.PrefetchScalarGridSpec(
            num_scalar_prefetch=4, grid=(n_blocks,),
            in_specs=[pl.BlockSpec(memory_space=pl.ANY), wspec_in, wspec_in, wspec_out],
            out_specs=pl.BlockSpec(memory_space=pl.ANY),
            scratch_shapes=[pltpu.VMEM((2, MOE_ROWS, d), F32), pltpu.VMEM((2, MOE_ROWS, d), F32),
                            pltpu.VMEM((d, de), BF16), pltpu.VMEM((d, de), BF16), pltpu.VMEM((de, d), BF16),
                            pltpu.SemaphoreType.DMA((2,)), pltpu.SemaphoreType.DMA((2,))]),
        out_shape=jax.ShapeDtypeStruct((2 * n + 2 * MOE_ROWS, d), F32),
        compiler_params=_cp(("arbitrary",), MOE_VMEM_LIMIT), name="moe_ffn",
    )(tok, dst, be, n_used, h, w1, w3, w2)


def _conv_body(va_ref, bg_ref, cg_ref, w_ref, buf_ref, o_ref, nb_ref):
    u = cg_ref[...] * va_ref[...]
    t = u.shape[0]
    row = _iota(u.shape, 0)
    b0 = buf_ref[0:1, :]
    b1 = buf_ref[1:2, :]
    u1 = jnp.where(row == 0, b1, pltpu.roll(u, 1, axis=0))
    u2 = jnp.where(row == 0, b0, jnp.where(row == 1, b1, pltpu.roll(u, 2, axis=0)))
    w = w_ref[...]
    y = w[0:1] * u2 + w[1:2] * u1 + w[2:3] * u
    o_ref[...] = (bg_ref[...] * y).astype(o_ref.dtype)
    nb_ref[...] = u[t - 2:t, :]


def conv_seq(z3, conv_w, buf, tc=256):
    bsz, t, _ = z3.shape
    nc = D_CONV // tc
    col = lambda off: pl.BlockSpec((None, t, tc), lambda b, j: (b, 0, off + j))
    return pl.pallas_call(
        _conv_body, grid=(bsz, nc),
        in_specs=[col(0), col(nc), col(2 * nc),
                  pl.BlockSpec((CONV_WIDTH, tc), lambda b, j: (0, j)),
                  pl.BlockSpec((None, 2, tc), lambda b, j: (b, 0, j))],
        out_specs=[pl.BlockSpec((None, t, tc), lambda b, j: (b, 0, j)),
                   pl.BlockSpec((None, 2, tc), lambda b, j: (b, 0, j))],
        out_shape=[jax.ShapeDtypeStruct((bsz, t, D_CONV), BF16), jax.ShapeDtypeStruct((bsz, 2, D_CONV), F32)],
        compiler_params=_cp(("parallel", "parallel")), name="conv_seq",
    )(z3, z3, z3, conv_w, buf)


def _conv_step_body(va_ref, bg_ref, cg_ref, w_ref, b0_ref, b1_ref, o_ref, u_ref):
    u = cg_ref[...] * va_ref[...]
    w = w_ref[...]
    y = w[0:1] * b0_ref[...] + w[1:2] * b1_ref[...] + w[2:3] * u
    o_ref[...] = (bg_ref[...] * y).astype(o_ref.dtype)
    u_ref[...] = u


def conv_step(z2, conv_w, buf):
    bsz = z2.shape[0]
    col = lambda j: pl.BlockSpec((bsz, D_CONV), lambda i: (0, j))
    full = pl.BlockSpec((bsz, D_CONV), lambda i: (0, 0))
    a, u = pl.pallas_call(
        _conv_step_body, grid=(1,),
        in_specs=[col(0), col(1), col(2), pl.BlockSpec((CONV_WIDTH, D_CONV), lambda i: (0, 0)), full, full],
        out_specs=[full, full],
        out_shape=[jax.ShapeDtypeStruct((bsz, D_CONV), BF16), jax.ShapeDtypeStruct((bsz, D_CONV), F32)],
        compiler_params=_cp(("arbitrary",)), name="conv_step",
    )(z2, z2, z2, conv_w, buf[:, 0], buf[:, 1])
    return a, jnp.stack([buf[:, 1], u], axis=1)


def _hgrn_body(q_ref, f_ref, i_ref, g_ref, lb_ref, nw_ref, s0_ref, o_ref, so_ref, st_ref,
               *, chunk, sub, nchunk, nhead, t_real, layer, indep):
    tstep = pl.program_id(2)
    tb = chunk * nchunk
    dk = HGRN_DK

    if not indep:
        @pl.when(tstep == 0)
        def _():
            for hd in range(nhead):
                st_ref[hd] = s0_ref[hd].T

    lbl = lb_ref[...]
    e = jnp.exp(lbl - jnp.max(lbl, axis=0, keepdims=True))
    sm = e / jnp.sum(e, axis=0, keepdims=True)
    lb = jnp.zeros((1, lbl.shape[1]), F32)
    for r in range(1, layer + 1):
        lb = lb + sm[r:r + 1]
    tri = jnp.where(_iota((chunk, chunk), 0) >= _iota((chunk, chunk), 1), 1.0, 0.0).astype(BF16)
    nw = nw_ref[...]
    neg = jnp.float32(-jnp.inf)

    fr = f_ref[...]
    v = i_ref[...]
    logf = jnp.log(lb + (1.0 - lb) * _sigmoid(fr))
    kin = (1.0 - lb) * _sigmoid(-fr)
    if t_real is not None:
        row = _iota(fr.shape, 0)
        live = ((row & (chunk - 1)) if indep else (tstep * tb + row)) < t_real
        logf = jnp.where(live, logf, 0.0)
        kin = jnp.where(live, kin, 0.0)
    qs = _silu(q_ref[...])
    causal = _iota((sub, sub, 1), 1) <= _iota((sub, sub, 1), 0)
    cuts = [(slice(c * chunk, (c + 1) * chunk), slice(hd * dk, (hd + 1) * dk))
            for c in range(nchunk) for hd in range(nhead)]
    bb_all = [_dot_exact_lhs(tri, logf[c * chunk:(c + 1) * chunk]) for c in range(nchunk)]
    bb_l = [bb_all[c][:, hd * dk:(hd + 1) * dk] for c in range(nchunk) for hd in range(nhead)]
    qs_l = [qs[rows, ls] for rows, ls in cuts]
    kin_l = [kin[rows, ls] for rows, ls in cuts]
    v_l = [v[rows, ls] for rows, ls in cuts]
    bl_l = [bb[chunk - 1:chunk] for bb in bb_l]
    kv_l = [_dot(vc, kc * jnp.exp(bl - bb), TN) for vc, kc, bl, bb in zip(v_l, kin_l, bl_l, bb_l)]
    intra_l = []
    for qc, kc, vc, bb in zip(qs_l, kin_l, v_l, bb_l):
        parts = []
        for blk in range(chunk // sub):
            lo = blk * sub
            qi, bi, ki, vi = qc[lo:lo + sub], bb[lo:lo + sub], kc[lo:lo + sub], vc[lo:lo + sub]
            d = bi[:, None, :] - bi[None, :, :]
            sc = jnp.sum(qi[:, None, :] * ki[None, :, :] * jnp.exp(jnp.where(causal, d, neg)), axis=-1)
            oi = _dot(sc, vi)
            if blk > 0:
                anchor = bb[lo - 1:lo]
                qt = qi * jnp.exp(bi - anchor)
                kt = kc[:lo] * jnp.exp(anchor - bb[:lo])
                oi = oi + _dot(_dot(qt, kt, NT), vc[:lo])
            parts.append(oi)
        intra_l.append(parts[0] if len(parts) == 1 else jnp.concatenate(parts, axis=0))
    if indep:
        st_l = [s0_ref[c, hd].T for c in range(nchunk) for hd in range(nhead)]
        for idx, (st, bl, kv) in enumerate(zip(st_l, bl_l, kv_l)):
            so_ref[idx // nhead, idx % nhead] = (st * jnp.exp(bl) + kv).T
    else:
        st_l = []
        cur = [st_ref[hd] for hd in range(nhead)]
        for c in range(nchunk):
            for hd in range(nhead):
                idx = c * nhead + hd
                st_l.append(cur[hd])
                cur[hd] = cur[hd] * jnp.exp(bl_l[idx]) + kv_l[idx]
        for hd in range(nhead):
            st_ref[hd] = cur[hd]
    o_l = [_dot(qc * jnp.exp(bb), sc, NT) + oi for qc, bb, sc, oi in zip(qs_l, bb_l, st_l, intra_l)]
    o_l = [o * lax.rsqrt(jnp.mean(o * o, axis=-1, keepdims=True) + RMS_EPS) * nw for o in o_l]
    rows_l = [o_l[c * nhead] if nhead == 1 else jnp.concatenate(o_l[c * nhead:(c + 1) * nhead], axis=1)
              for c in range(nchunk)]
    o = rows_l[0] if nchunk == 1 else jnp.concatenate(rows_l, axis=0)
    o_ref[...] = (o * _silu(g_ref[...])).astype(o_ref.dtype)

    if not indep:
        @pl.when(tstep == pl.num_programs(2) - 1)
        def _():
            for hd in range(nhead):
                so_ref[hd] = st_ref[hd].T


def hgrn_seq(z3, hgrn_lb, norm_w, s0, layer, chunk, nchunk, nhead=1, t_real=None, indep=False):
    bsz, t, _ = z3.shape
    tb = chunk * nchunk
    sb = None
    if indep:
        sb = nchunk
        z3 = z3.reshape(bsz // nchunk, tb, z3.shape[2])
        bsz, t = bsz // nchunk, tb
    wl = nhead * LANES
    groups = HGRN_HEADS // nhead
    qoff = 3 * D_CONV // wl
    col = lambda k: pl.BlockSpec((None, tb, wl), lambda b, h, s: (b, s, qoff + k * groups + h))
    st = pl.BlockSpec((sb, nhead, HGRN_DK, HGRN_DV), lambda b, h, s: (b, h, 0, 0))
    body = functools.partial(_hgrn_body, chunk=chunk, sub=min(HGRN_SUB, chunk), nchunk=nchunk, nhead=nhead, t_real=t_real,
                             layer=layer, indep=indep)
    nseq = bsz * (sb or 1)
    o, so = pl.pallas_call(
        body, grid=(bsz, groups, t // tb),
        in_specs=[col(0), col(1), col(2), col(3),
                  pl.BlockSpec((N_EVEN, wl), lambda b, h, s: (0, h)),
                  pl.BlockSpec((1, HGRN_DV), lambda b, h, s: (0, 0)), st],
        out_specs=[pl.BlockSpec((None, tb, wl), lambda b, h, s: (b, s, h)), st],
        out_shape=[jax.ShapeDtypeStruct((bsz, t, D_HV), BF16),
                   jax.ShapeDtypeStruct((nseq, HGRN_HEADS, HGRN_DK, HGRN_DV), F32)],
        scratch_shapes=[pltpu.VMEM((nhead, HGRN_DV, HGRN_DK), F32)],
        compiler_params=_cp(("parallel", "parallel", "arbitrary")), name="hgrn_seq",
    )(z3, z3, z3, z3, hgrn_lb, norm_w.reshape(1, HGRN_DV), s0)
    return o.reshape(nseq, chunk if indep else t, D_HV), so


def _attn_body(sink_ref, zq_ref, zkv_ref, cos_ref, sin_ref, ck_ref, cv_ref, o_ref, nk_ref, nv_ref, kp_ref, vp_ref,
               *, prev_valid, t_real, tq):
    i = pl.program_id(1)
    w = WINDOW

    @pl.when(i == 0)
    def _():
        kp_ref[...] = ck_ref[...]
        vp_ref[...] = cv_ref[...]

    cos = cos_ref[...]
    sin = sin_ref[...]

    def rope(x):
        width = x.shape[1]
        reps = width // LANES
        first = (_iota(x.shape, 1) & (HEAD_DIM - 1)) < (HEAD_DIM // 2)
        rot = jnp.where(first, pltpu.roll(x, width - HEAD_DIM // 2, axis=1), pltpu.roll(x, HEAD_DIM // 2, axis=1))
        return x * jnp.tile(cos, (1, reps)) + rot * jnp.tile(sin, (1, reps))

    kv = zkv_ref[...]
    qr = rope(zq_ref[...]) * (HEAD_DIM ** -0.5)
    kr = rope(kv[:, :D_KV])
    v = kv[:, D_KV:]
    kprev = kp_ref[...]
    vprev = vp_ref[...]

    grp = ATTN_HEADS // ATTN_KV_HEADS
    rows = _iota((grp * tq, w + tq), 0)
    cols = _iota((grp * tq, w + tq), 1)
    delta = (rows & (tq - 1)) + w - cols
    valid = (delta >= 0) & (delta <= w)
    if not prev_valid:
        valid = valid & (cols >= jnp.where(i > 0, 0, w))
    head_of_row = _iota((grp * tq, 1), 0) >> int(math.log2(tq))
    neg = jnp.float32(-jnp.inf)
    outs = []
    for g in range(ATTN_KV_HEADS):
        ls = slice(g * HEAD_DIM, (g + 1) * HEAD_DIM)
        kg = jnp.concatenate([kprev[:, ls], kr[:, ls]], axis=0)
        vg = jnp.concatenate([vprev[:, ls], v[:, ls]], axis=0)
        qg = jnp.concatenate([qr[:, (grp * g + hh) * HEAD_DIM:(grp * g + hh + 1) * HEAD_DIM] for hh in range(grp)],
                             axis=0)
        s = jnp.where(valid, _dot(qg, kg, NT), neg)
        sink = jnp.zeros((grp * tq, 1), F32)
        for hh in range(grp):
            sink = jnp.where(head_of_row == hh, sink_ref[grp * g + hh], sink)
        m = jnp.maximum(jnp.max(s, axis=-1, keepdims=True), sink)
        p = jnp.exp(s - m)
        p = p / (jnp.sum(p, axis=-1, keepdims=True) + jnp.exp(sink - m))
        og = _dot(p, vg)
        outs += [og[hh * tq:(hh + 1) * tq] for hh in range(grp)]
    o_ref[...] = jnp.concatenate(outs, axis=1).astype(o_ref.dtype)
    if t_real == w:
        kp_ref[...] = kr
        vp_ref[...] = v

    @pl.when(i == pl.num_programs(1) - 1)
    def _():
        if t_real == w:
            nk_ref[...] = kr
            nv_ref[...] = v
        else:
            last = _iota(kprev.shape, 0) == w - 1
            nk_ref[...] = jnp.where(last, kr[0:1], pltpu.roll(kprev, w - 1, axis=0))
            nv_ref[...] = jnp.where(last, v[0:1], pltpu.roll(vprev, w - 1, axis=0))


def attn_seq(z3, sinks, cos, sin, cache_k, cache_v, prev_valid, t_real, tq=WINDOW):
    bsz, t, _ = z3.shape
    nb = t // tq
    assert tq == WINDOW or nb == 1
    cache = pl.BlockSpec((None, WINDOW, D_KV), lambda b, i: (b, 0, 0))
    tab = pl.BlockSpec((tq, LANES), lambda b, i: (i, 0))
    body = functools.partial(_attn_body, prev_valid=prev_valid, t_real=t_real, tq=tq)
    return pl.pallas_call(
        body, grid=(bsz, nb),
        in_specs=[pl.BlockSpec(memory_space=pltpu.SMEM),
                  pl.BlockSpec((None, tq, D_Q), lambda b, i: (b, i, 0)),
                  pl.BlockSpec((None, tq, 2 * D_KV), lambda b, i: (b, i, D_Q // (2 * D_KV))),
                  tab, tab, cache, cache],
        out_specs=[pl.BlockSpec((None, tq, D_Q), lambda b, i: (b, i, 0)), cache, cache],
        out_shape=[jax.ShapeDtypeStruct((bsz, t, D_Q), BF16),
                   jax.ShapeDtypeStruct((bsz, WINDOW, D_KV), F32), jax.ShapeDtypeStruct((bsz, WINDOW, D_KV), F32)],
        scratch_shapes=[pltpu.VMEM((WINDOW, D_KV), F32), pltpu.VMEM((WINDOW, D_KV), F32)],
        compiler_params=_cp(("parallel", "arbitrary")), name="attn_seq",
    )(sinks, z3, z3, cos, sin, cache_k, cache_v)


def rope_tables(pos):
    half = HEAD_DIM // 2
    inv = jnp.exp(-math.log(ROPE_THETA) * jnp.arange(half, dtype=F32) / half)
    ang = pos.astype(F32)[:, None] * inv[None, :]
    c, s = jnp.cos(ang), jnp.sin(ang)
    cos = jnp.concatenate([c, c, c, c], axis=1)
    sin = jnp.concatenate([-s, s, -s, s], axis=1)
    return cos, sin


def _rwkv_body(r_ref, k_ref, v_ref, wa_ref, gd_ref, sr_ref, sk_ref, sv_ref, swa_ref, sgd_ref,
               mr_ref, mk_ref, mv_ref, mwa_ref, mgd_ref, w0_ref, w2_ref, a0_ref, a2_ref, g2_ref,
               kkp_ref, ka_ref, rk_ref, lg_ref, lbias_ref, s0_ref, o_ref, so_ref, st_ref, prev_ref, prevw_ref,
               *, chunk, nchunk, npair, t_real, indep):
    tstep = pl.program_id(2)
    n = RWKV_N
    ln = chunk
    tb = chunk * nchunk

    if not indep:
        @pl.when(tstep == 0)
        def _():
            st_ref[...] = s0_ref[...]
            prev_ref[0:1, :] = sr_ref[...]
            prev_ref[1:2, :] = sk_ref[...]
            prev_ref[2:3, :] = sv_ref[...]
            prevw_ref[0:1, :] = swa_ref[...]
            prevw_ref[1:2, :] = sgd_ref[...]

    def mix(x_ref, mu_ref, s_ref, p_ref, idx):
        x = x_ref[...]
        row = _iota(x.shape, 0)
        if indep:
            before = jnp.broadcast_to(s_ref[...], (nchunk, ln, x.shape[1])).reshape(tb, x.shape[1])
            shifted = jnp.where((row & (ln - 1)) == 0, before, pltpu.roll(x, 1, axis=0))
        else:
            shifted = jnp.where(row == 0, p_ref[idx:idx + 1, :], pltpu.roll(x, 1, axis=0))
            p_ref[idx:idx + 1, :] = x[tb - 1:tb, :]
        return x + mu_ref[...] * (shifted - x)

    r = mix(r_ref, mr_ref, sr_ref, prev_ref, 0)
    kr = mix(k_ref, mk_ref, sk_ref, prev_ref, 1)
    vr = mix(v_ref, mv_ref, sv_ref, prev_ref, 2)
    wa = mix(wa_ref, mwa_ref, swa_ref, prevw_ref, 0)
    gd = mix(gd_ref, mgd_ref, sgd_ref, prevw_ref, 1)[:, :RWKV_RANK]
    wd = wa[:, :RWKV_RANK]
    ad = wa[:, RWKV_RANK:]

    w_log = -_softplus(-(w0_ref[...] + _dot3(jnp.tanh(wd), w2_ref[...]))) - 0.5
    logw = -jnp.exp(w_log)
    a = _sigmoid(a0_ref[...] + _dot3(ad, a2_ref[...]))
    gate = _dot3(_sigmoid(gd), g2_ref[...])

    sh = int(math.log2(n))
    seg = jnp.where((_iota((LANES, LANES), 0) >> sh) == (_iota((LANES, LANES), 1) >> sh), 1.0, 0.0).astype(BF16)

    def segsum(x):
        tiles = [_dot_exact_rhs(x[:, i * LANES:(i + 1) * LANES], seg) for i in range(npair)]
        return tiles[0] if npair == 1 else jnp.concatenate(tiles, axis=1)

    kkv = kr * kkp_ref[...]
    kk = kkv / jnp.maximum(jnp.sqrt(segsum(kkv * kkv)), 1e-12)
    kf = kr * (1.0 + (a - 1.0) * ka_ref[...])
    bonus = segsum(r * kf * rk_ref[...]) * vr
    if t_real is not None:
        row = _iota(logw.shape, 0)
        live = ((row & (ln - 1)) if indep else (tstep * tb + row)) < t_real
        zero = jnp.zeros_like(logw)
        logw = jnp.where(live, logw, zero)
        kk = jnp.where(live, kk, zero)
        kf = jnp.where(live, kf, zero)
        vr_s = jnp.where(live, vr, zero)
    else:
        vr_s = vr

    tri = jnp.where(_iota((ln, ln), 0) >= _iota((ln, ln), 1), 1.0, 0.0).astype(BF16)
    rr = _iota((ln, ln), 0)
    cc = _iota((ln, ln), 1)
    strict = rr > cc
    incl = rr >= cc
    eye = jnp.where(rr == cc, 1.0, 0.0)
    eye_n = jnp.where(_iota((n, n), 0) == _iota((n, n), 1), 1.0, 0.0)

    nhead = 2 * npair
    am_l, rm_l, bp_l, kp_l, bl_l, kl_l, v_l, gl_l = [], [], [], [], [], [], [], []
    for ci in range(nchunk):
        rows = slice(ci * ln, (ci + 1) * ln)
        lw = logw[rows]
        c = _dot_exact_lhs(tri, lw)
        gam = jnp.exp(c)
        ginv = jnp.exp(-c)
        am = -kk[rows] * jnp.exp(c - lw)
        rm = r[rows] * gam
        bp = kk[rows] * a[rows] * ginv
        kp = kf[rows] * ginv
        g_last = gam[ln - 1:ln, :]
        bpl = bp * g_last
        kpl = kp * g_last
        vv = vr_s[rows]
        for hd in range(nhead):
            ls = slice(hd * n, (hd + 1) * n)
            am_l.append(am[:, ls]); rm_l.append(rm[:, ls]); bp_l.append(bp[:, ls]); kp_l.append(kp[:, ls])
            bl_l.append(bpl[:, ls]); kl_l.append(kpl[:, ls]); v_l.append(vv[:, ls]); gl_l.append(g_last[:, ls])
    each = lambda fn, *ls: [fn(*xs) for xs in zip(*ls)]
    pw_l = each(lambda am, rm, bp, kp: _dot3(jnp.concatenate([am, rm], axis=0),
                                             jnp.concatenate([bp, kp], axis=0), NT), am_l, rm_l, bp_l, kp_l)
    m_l = each(lambda pw: jnp.where(strict, pw[:ln, :ln], 0.0), pw_l)
    nm_l = each(lambda pw: jnp.where(strict, pw[:ln, ln:], 0.0), pw_l)
    qb_l = each(lambda pw: jnp.where(incl, pw[ln:, :ln], 0.0), pw_l)
    qk_l = each(lambda pw: jnp.where(incl, pw[ln:, ln:], 0.0), pw_l)
    nq_l = each(lambda nm, qk, v: _dot(jnp.concatenate([nm, qk], axis=0), v), nm_l, qk_l, v_l)
    nv_l = [x[:ln] for x in nq_l]
    qkv_l = [x[ln:] for x in nq_l]
    kv_l = each(lambda v, kl: _dot(v, kl, TN), v_l, kl_l)
    tinv_l = each(lambda m: eye + m, m_l)
    p_l = m_l
    for _ in range(int(math.log2(ln)) - 1):
        p_l = each(lambda p: _dot(p, p), p_l)
        tinv_l = each(lambda t, p: t + _dot(t, p), tinv_l, p_l)
    wu_l = each(lambda t, am, nv: _dot(t, jnp.concatenate([am, nv], axis=1)), tinv_l, am_l, nv_l)
    ac_l = each(lambda wu, bl: _dot(wu, bl, TN), wu_l, bl_l)
    a_l = each(lambda ac, gl: eye_n * gl + ac[:n], ac_l, gl_l)
    c_l = each(lambda ac, kv: ac[n:] + kv, ac_l, kv_l)
    ro_l = each(_dot, qb_l, wu_l)
    rt_l = each(lambda rm, ro: rm + ro[:, :n], rm_l, ro_l)
    o0_l = each(lambda ro, qkv: ro[:, n:] + qkv, ro_l, qkv_l)

    o_rows = []
    if indep:
        s_l = [s0_ref[ci, hd] for ci in range(nchunk) for hd in range(nhead)]
        o_all = each(lambda rt, s, o0: _dot(rt, s, NT) + o0, rt_l, s_l, o0_l)
        s_l = each(lambda s, am, cm: _dot(s, am) + cm, s_l, a_l, c_l)
        for ci in range(nchunk):
            o_rows.append(o_all[ci * nhead:(ci + 1) * nhead])
            for hd in range(nhead):
                so_ref[ci, hd] = s_l[ci * nhead + hd]
    else:
        s_l = [st_ref[hd] for hd in range(nhead)]
        for ci in range(nchunk):
            sl = slice(ci * nhead, (ci + 1) * nhead)
            o_rows.append(each(lambda rt, s, o0: _dot(rt, s, NT) + o0, rt_l[sl], s_l, o0_l[sl]))
            s_l = each(lambda s, am, cm: _dot(s, am) + cm, s_l, a_l[sl], c_l[sl])
        for hd in range(nhead):
            st_ref[hd] = s_l[hd]
    cols = [o_rows[0][hd] if nchunk == 1 else jnp.concatenate([o_rows[ci][hd] for ci in range(nchunk)], axis=0)
            for hd in range(nhead)]
    o = jnp.concatenate(cols, axis=1)
    mu_o = segsum(o) * (1.0 / n)
    dlt = o - mu_o
    var_o = segsum(dlt * dlt) * (1.0 / n)
    o = dlt * lax.rsqrt(var_o + RWKV_GN_EPS) * lg_ref[...] + lbias_ref[...]
    o_ref[...] = ((o + bonus) * gate).astype(o_ref.dtype)

    if not indep:
        @pl.when(tstep == pl.num_programs(2) - 1)
        def _():
            so_ref[...] = st_ref[...]


def rwkv_seq(z3, shift, P, j, s0, chunk, nchunk=1, npair=1, t_real=None, indep=False):
    bsz, t, _ = z3.shape
    tb = chunk * nchunk
    sb = None
    if indep:
        sb = nchunk
        z3 = z3.reshape(bsz // nchunk, tb, z3.shape[2])
        bsz, t = bsz // nchunk, tb
    wl = npair * LANES
    groups = D_RWKV // wl
    zoff = (D_Q + 2 * D_KV) // LANES
    nb = D_RWKV // LANES
    wa_blk = 3 * nb
    gd_blk = 3 * nb + 1
    zc = lambda off: pl.BlockSpec((None, tb, wl), lambda b, p, s: (b, s, (zoff + off) // npair + p))
    zw = lambda blk: pl.BlockSpec((None, tb, LANES), lambda b, p, s: (b, s, zoff + blk))
    sc = lambda off: pl.BlockSpec((sb, 1, wl), lambda b, p, s: (b, 0, off // npair + p))
    sw = lambda blk: pl.BlockSpec((sb, 1, LANES), lambda b, p, s: (b, 0, blk))
    mc = lambda off: pl.BlockSpec((1, wl), lambda b, p, s: (0, off // npair + p))
    mw = lambda blk: pl.BlockSpec((1, LANES), lambda b, p, s: (0, blk))
    vec = pl.BlockSpec((1, wl), lambda b, p, s: (0, p))
    lora = pl.BlockSpec((RWKV_RANK, wl), lambda b, p, s: (0, p))
    st = pl.BlockSpec((sb, 2 * npair, RWKV_N, RWKV_N), lambda b, p, s: (b, p, 0, 0))
    mu = jnp.pad(P['rwkv_mu'][j], (0, D_SHIFT_PAD - D_SHIFT)).reshape(1, D_SHIFT_PAD)
    row = lambda x: x.reshape(1, D_RWKV)
    body = functools.partial(_rwkv_body, chunk=chunk, nchunk=nchunk, npair=npair, t_real=t_real, indep=indep)
    nseq = bsz * (sb or 1)
    o, so = pl.pallas_call(
        body, grid=(bsz, groups, t // tb),
        in_specs=[zc(0), zc(nb), zc(2 * nb), zw(wa_blk), zw(gd_blk),
                  sc(0), sc(nb), sc(2 * nb), sw(wa_blk), sw(gd_blk),
                  mc(0), mc(nb), mc(2 * nb), mw(wa_blk), mw(gd_blk),
                  vec, lora, vec, lora, lora, vec, vec, vec, vec, vec, st],
        out_specs=[pl.BlockSpec((None, tb, wl), lambda b, p, s: (b, s, p)), st],
        out_shape=[jax.ShapeDtypeStruct((bsz, t, D_RWKV), BF16),
                   jax.ShapeDtypeStruct((nseq, RWKV_HEADS, RWKV_N, RWKV_N), F32)],
        scratch_shapes=[pltpu.VMEM((2 * npair, RWKV_N, RWKV_N), F32), pltpu.VMEM((8, wl), F32),
                        pltpu.VMEM((8, LANES), F32)],
        compiler_params=_cp(("parallel", "parallel", "arbitrary")), name="rwkv_seq",
    )(z3, z3, z3, z3, z3, shift, shift, shift, shift, shift, mu, mu, mu, mu, mu,
      row(P['rwkv_w0'][j]), P['rwkv_w2'][j], row(P['rwkv_a0'][j]), P['rwkv_a2'][j], P['rwkv_g2'][j],
      row(P['rwkv_kk'][j]), row(P['rwkv_ka'][j]), row(P['rwkv_rk'][j]), row(P['rwkv_lnx_g'][j]),
      row(P['rwkv_lnx_b'][j]), s0)
    return o.reshape(nseq, chunk if indep else t, D_RWKV), so


def _pad_time(z2, tp):
    return jnp.pad(z2[:, None, :], ((0, 0), (0, tp - 1), (0, 0)))


def _mix_layer(l, h, x, grp, pos0, W, P, st, single):
    n = x.shape[0]
    t = grp.t
    bsz = n // t
    tm = grp.tm
    j = l // 2
    step_pad = 8
    new = {}
    if l % 2 == 0:
        z = matmul(h, W['w_in_even'][j], min(n, 2 * tm), D_IN_EVEN // 4)
        if single:
            a_out, new['conv'] = conv_step(z, P['conv_w'][j], st['conv'][j])
            b3, new['hgrn'] = hgrn_seq(_pad_time(z, step_pad), P['hgrn_lb'], P['hgrn_norm'][j], st['hgrn'][j], j,
                                       chunk=step_pad, nchunk=4, nhead=4, t_real=1, indep=True)
            b_out = b3[:, 0]
        else:
            z3 = z.reshape(bsz, t, D_IN_EVEN)
            a3, new['conv'] = conv_seq(z3, P['conv_w'][j], st['conv'][j])
            b3, new['hgrn'] = hgrn_seq(z3, P['hgrn_lb'], P['hgrn_norm'][j], st['hgrn'][j], j, chunk=64, nchunk=8)
            a_out, b_out = a3.reshape(n, D_CONV), b3.reshape(n, D_HV)
        wa, wb = W['w_out_even'][j][:D_CONV], W['w_out_even'][j][D_CONV:]
    else:
        z = matmul(h, W['w_in_odd'][j], tm, D_IN_ODD_PAD // 2)
        shift_in = jnp.pad(st['shift'][j], ((0, 0), (0, D_SHIFT_PAD - D_SHIFT)))[:, None, :]
        kc = st['k'][j].reshape(bsz, WINDOW, D_KV)
        vc = st['v'][j].reshape(bsz, WINDOW, D_KV)
        if single:
            cos, sin = rope_tables(pos0 + jnp.arange(step_pad, dtype=I32))
            a3, nk, nv = attn_seq(_pad_time(z[:, :D_Q + 2 * D_KV], step_pad), P['attn_sinks'][j], cos, sin,
                                  kc, vc, prev_valid=True, t_real=1, tq=step_pad)
            b3, new['rwkv'] = rwkv_seq(_pad_time(z, step_pad), shift_in, P, j, st['rwkv'][j], chunk=step_pad,
                                       nchunk=4, npair=4, t_real=1, indep=True)
            a_out, b_out = a3[:, 0], b3[:, 0]
            new['shift'] = z[:, D_Q + 2 * D_KV:D_IN_ODD]
        else:
            z3 = z.reshape(bsz, t, D_IN_ODD_PAD)
            cos, sin = rope_tables(pos0 + jnp.arange(t, dtype=I32))
            a3, nk, nv = attn_seq(z3, P['attn_sinks'][j], cos, sin, kc, vc, prev_valid=False, t_real=WINDOW)
            b3, new['rwkv'] = rwkv_seq(z3, shift_in, P, j, st['rwkv'][j], chunk=64, nchunk=4, npair=2)
            a_out, b_out = a3.reshape(n, D_Q), b3.reshape(n, D_RWKV)
            new['shift'] = z3[:, t - 1, D_Q + 2 * D_KV:D_IN_ODD]
        new['k'] = nk.reshape(bsz, WINDOW, ATTN_KV_HEADS, HEAD_DIM)
        new['v'] = nv.reshape(bsz, WINDOW, ATTN_KV_HEADS, HEAD_DIM)
        wa, wb = W['w_out_odd'][j][:D_Q], W['w_out_odd'][j][D_Q:]
    return a_out, b_out, wa, wb, new


def kernel(x_prompt, x_sample, c_prompt, c_sample, state_conv, state_hgrn, cache_swa_k, cache_swa_v,
           state_rwkv, state_shift, ada_w, ada_b, ln_g, ln_b, w_in_even, w_out_even, conv_w, hgrn_lb,
           hgrn_norm, w_in_odd, w_out_odd, attn_sinks, rwkv_mu, rwkv_w0, rwkv_w2, rwkv_a0, rwkv_a2,
           rwkv_g2, rwkv_kk, rwkv_ka, rwkv_rk, rwkv_lnx_g, rwkv_lnx_b, moe_w_grp, moe_b_grp, moe_w_exp,
           moe_b_exp, moe_w1, moe_w3, moe_w2):
    P = dict(ln_g=ln_g, ln_b=ln_b, conv_w=conv_w, hgrn_lb=hgrn_lb, hgrn_norm=hgrn_norm, attn_sinks=attn_sinks,
             rwkv_mu=rwkv_mu, rwkv_w0=rwkv_w0, rwkv_w2=rwkv_w2, rwkv_a0=rwkv_a0, rwkv_a2=rwkv_a2,
             rwkv_g2=rwkv_g2, rwkv_kk=rwkv_kk, rwkv_ka=rwkv_ka, rwkv_rk=rwkv_rk.reshape(N_ODD, D_RWKV),
             rwkv_lnx_g=rwkv_lnx_g, rwkv_lnx_b=rwkv_lnx_b)
    bp, tp, d = x_prompt.shape
    bs, ts, _ = x_sample.shape
    n_p, n_s = bp * tp, bs * ts
    router_w = jnp.zeros((DEPTH, d, 2 * LANES), F32)
    router_w = router_w.at[:, :, :N_GROUPS].set(moe_w_grp).at[:, :, LANES:LANES + N_EXPERTS].set(moe_w_exp)
    router_b = jnp.zeros((DEPTH, 1, 2 * LANES), F32)
    router_b = router_b.at[:, 0, :N_GROUPS].set(moe_b_grp).at[:, 0, LANES:LANES + N_EXPERTS].set(moe_b_exp)
    W = dict(w_in_even=w_in_even.astype(BF16), w_out_even=w_out_even.astype(BF16),
             w_in_odd=jnp.pad(w_in_odd.astype(BF16), ((0, 0), (0, 0), (0, D_IN_ODD_PAD - D_IN_ODD))),
             w_out_odd=w_out_odd.astype(BF16))

    mod = ada_mod(jnp.concatenate([c_prompt, c_sample], axis=0), ada_w, ada_b)
    mod = mod.reshape(DEPTH, bp + bs, 6, d).transpose(0, 2, 1, 3)
    grp_p = _Group(n_p, tp, 512, mod[:, :, :bp, None, :], per_row=False)
    grp_s = _Group(n_s, ts, n_s, mod[:, :, bp:], per_row=True)

    zeros = lambda *s: jnp.zeros(s, F32)
    st_p = dict(conv=zeros(N_EVEN, bp, CONV_WIDTH - 1, D_CONV), hgrn=zeros(N_EVEN, bp, HGRN_HEADS, HGRN_DK, HGRN_DV),
                k=zeros(N_ODD, bp, WINDOW, ATTN_KV_HEADS, HEAD_DIM), v=zeros(N_ODD, bp, WINDOW, ATTN_KV_HEADS, HEAD_DIM),
                rwkv=zeros(N_ODD, bp, RWKV_HEADS, RWKV_N, RWKV_N), shift=zeros(N_ODD, bp, D_SHIFT))
    st_s = dict(conv=state_conv, hgrn=state_hgrn, k=cache_swa_k, v=cache_swa_v, rwkv=state_rwkv, shift=state_shift)
    new_p = {k: [] for k in st_p}
    new_s = {k: [] for k in st_s}

    n_all = n_p + n_s
    t_all = jnp.arange(n_all, dtype=I32)
    row_of = lambda k: jnp.where(t_all < n_p, k * n_p + t_all, 2 * n_p + k * n_s + (t_all - n_p))
    dst_of_entry = jnp.stack([row_of(0), row_of(1)], axis=1).reshape(-1)

    x_p, x_s = x_prompt.reshape(n_p, d), x_sample.reshape(n_s, d)
    h_all = zeros(n_all, d)
    h_p, h_s = modulate(x_p, grp_p, 0, 1, 0), modulate(x_s, grp_s, 0, 1, 0)
    for l in range(DEPTH):
        a_p, b_p, wa, wb, np_l = _mix_layer(l, h_p, x_p, grp_p, 0, W, P, st_p, single=False)
        a_s, b_s, _, _, ns_l = _mix_layer(l, h_s, x_s, grp_s, PAST_LEN, W, P, st_s, single=True)
        for k, v in np_l.items():
            new_p[k].append(v)
        for k, v in ns_l.items():
            new_s[k].append(v)
        x_p, h_all = out_proj_ln(a_p, b_p, wa, wb, x_p, grp_p, l, ln_g[l, 0], ln_b[l, 0], n_all, 0, h_all)
        x_s, h_all = out_proj_ln(a_s, b_s, wa, wb, x_s, grp_s, l, ln_g[l, 0], ln_b[l, 0], n_all, n_p, h_all)
        eid_p, gate_p, rank_p, cnt = router(h_all, 0, n_p, router_w[l], router_b[l], 256, zeros(1, LANES))
        eid_s, gate_s, rank_s, cnt = router(h_all, n_p, n_s, router_w[l], router_b[l], n_s, cnt)
        tok, dst, be, n_used, n_blocks = moe_plan(jnp.concatenate([eid_p, eid_s]), jnp.concatenate([rank_p, rank_s]),
                                                  cnt, dst_of_entry, 2 * n_all)
        ys = moe_ffn(h_all, tok, dst, be, n_used, n_blocks, moe_w1, moe_w3, moe_w2, l)
        last = l + 1 == DEPTH
        x_p, h_p = moe_combine_ln(ys, 0, n_p, gate_p, x_p, grp_p, l, ln_g[l, 1], ln_b[l, 1], has_next=not last)
        x_s, h_s = moe_combine_ln(ys, 2 * n_p, 2 * n_p + n_s, gate_s, x_s, grp_s, l, ln_g[l, 1], ln_b[l, 1],
                                  has_next=not last)
    order = ('conv', 'hgrn', 'k', 'v', 'rwkv', 'shift')
    return ((x_p.reshape(bp, tp, d), x_s.reshape(bs, ts, d))
            + tuple(jnp.stack(new_p[k]) for k in order) + tuple(jnp.stack(new_s[k]) for k in order))
```

```python
import functools
import math

import jax
import jax.numpy as jnp
from jax import lax
from jax.experimental import pallas as pl
from jax.experimental.pallas import tpu as pltpu

F32 = jnp.float32
BF16 = jnp.bfloat16
I32 = jnp.int32

D_MODEL = 2048
DEPTH = 4
PAST_LEN = 16384
N_EVEN = (DEPTH + 1) // 2
N_ODD = DEPTH // 2
D_CONV = 1024
CONV_WIDTH = 3
HGRN_HEADS = 8
HGRN_DK = 128
HGRN_DV = 128
D_HK = HGRN_HEADS * HGRN_DK
D_HV = HGRN_HEADS * HGRN_DV
ATTN_HEADS = 16
ATTN_KV_HEADS = 4
HEAD_DIM = 64
WINDOW = 128
ROPE_THETA = 10000.0
D_Q = ATTN_HEADS * HEAD_DIM
D_KV = ATTN_KV_HEADS * HEAD_DIM
RWKV_HEADS = 16
RWKV_N = 64
D_RWKV = RWKV_HEADS * RWKV_N
RWKV_RANK = 64
RWKV_GN_EPS = 64e-5
D_SHIFT = 3 * D_RWKV + 3 * RWKV_RANK
D_IN_EVEN = 3 * D_CONV + 2 * D_HK + 2 * D_HV
D_IN_ODD = D_Q + 2 * D_KV + D_SHIFT
N_GROUPS = 4
EXPERTS_PER_GROUP = 8
N_EXPERTS = N_GROUPS * EXPERTS_PER_GROUP
D_EXPERT = 512
ALPHA = (2 * DEPTH) ** 0.25
LN_EPS = 1e-5
RMS_EPS = 1e-6

LANES = 128
HGRN_SUB = 16
MOE_ROWS = 128
VMEM_LIMIT = 48 * 1024 * 1024
MOE_VMEM_LIMIT = 56 * 1024 * 1024

D_IN_ODD_PAD = -(-D_IN_ODD // LANES) * LANES
D_SHIFT_PAD = D_IN_ODD_PAD - (D_Q + 2 * D_KV)

NN = (((1,), (0,)), ((), ()))
NT = (((1,), (1,)), ((), ()))
TN = (((0,), (0,)), ((), ()))


def _cp(sem, vmem=VMEM_LIMIT):
    return pltpu.CompilerParams(dimension_semantics=sem, vmem_limit_bytes=vmem)


def _dot(a, b, dims=NN):
    return lax.dot_general(a.astype(BF16), b.astype(BF16), dims, preferred_element_type=F32)


def _split2(x):
    hi = x.astype(BF16)
    lo = (x - hi.astype(F32)).astype(BF16)
    return hi, lo


def _dot3(a, b, dims=NN):
    ah, al = _split2(a)
    bh, bl = _split2(b)
    d = lambda x, y: lax.dot_general(x, y, dims, preferred_element_type=F32)
    return d(ah, bh) + (d(ah, bl) + d(al, bh))


def _dot_exact_lhs(a_bf16, b, dims=NN):
    b1 = b.astype(BF16)
    r1 = b - b1.astype(F32)
    b2 = r1.astype(BF16)
    b3 = (r1 - b2.astype(F32)).astype(BF16)
    d = lambda y: lax.dot_general(a_bf16, y, dims, preferred_element_type=F32)
    return d(b1) + (d(b2) + d(b3))


def _dot_exact_rhs(a, b_bf16, dims=NN):
    a1 = a.astype(BF16)
    r1 = a - a1.astype(F32)
    a2 = r1.astype(BF16)
    d = lambda x: lax.dot_general(x, b_bf16, dims, preferred_element_type=F32)
    return d(a1) + d(a2)


def _sigmoid(x):
    return 1.0 / (1.0 + jnp.exp(-x))


def _silu(x):
    return x * _sigmoid(x)


def _softplus(x):
    return jnp.maximum(x, 0.0) + jnp.log(1.0 + jnp.exp(-jnp.abs(x)))


def _iota(shape, axis):
    return lax.broadcasted_iota(I32, shape, axis)


def _layer_norm(u, g, b):
    mu = jnp.mean(u, axis=-1, keepdims=True)
    d = u - mu
    var = jnp.mean(d * d, axis=-1, keepdims=True)
    return d * lax.rsqrt(var + LN_EPS) * g + b


def _ada_body(c_ref, w_ref, b_ref, o_ref):
    o_ref[...] = _dot(_silu(c_ref[...]), w_ref[...]) + b_ref[...]


def ada_mod(c_all, ada_w, ada_b, tn=1024):
    nl, d, n = ada_w.shape
    r = c_all.shape[0]
    return pl.pallas_call(
        _ada_body,
        grid=(nl, n // tn),
        in_specs=[pl.BlockSpec((r, d), lambda l, j: (0, 0)),
                  pl.BlockSpec((None, d, tn), lambda l, j: (l, 0, j)),
                  pl.BlockSpec((None, 1, tn), lambda l, j: (l, 0, j))],
        out_specs=pl.BlockSpec((None, r, tn), lambda l, j: (l, 0, j)),
        out_shape=jax.ShapeDtypeStruct((nl, r, n), F32),
        compiler_params=_cp(("parallel", "parallel")),
        name="ada_mod",
    )(c_all, ada_w, ada_b.reshape(nl, 1, n))


class _Group:
    def __init__(self, n, t, tm, modarr, per_row):
        self.n, self.t, self.tm, self.modarr, self.per_row = n, t, tm, modarr, per_row

    def mod(self, l, c):
        d = self.modarr.shape[-1]
        if self.per_row:
            return pl.BlockSpec((None, None, self.tm, d), lambda i: (l, c, i, 0))
        t, tm = self.t, self.tm
        return pl.BlockSpec((None, None, None, 1, d), lambda i: (l, c, (i * tm) // t, 0, 0))


def _modulate_body(x_ref, sc_ref, sh_ref, h_ref):
    h_ref[...] = (x_ref[...] * (1.0 + sc_ref[...]) + sh_ref[...]).astype(h_ref.dtype)


def modulate(x, grp, l, c_sc, c_sh):
    n, d = x.shape
    tm = grp.tm
    row = pl.BlockSpec((tm, d), lambda i: (i, 0))
    return pl.pallas_call(
        _modulate_body, grid=(n // tm,),
        in_specs=[row, grp.mod(l, c_sc), grp.mod(l, c_sh)],
        out_specs=row, out_shape=jax.ShapeDtypeStruct((n, d), BF16),
        compiler_params=_cp(("parallel",)), name="modulate",
    )(x, grp.modarr, grp.modarr)


def _mm_body(a_ref, w_ref, o_ref):
    o_ref[...] = jnp.dot(a_ref[...], w_ref[...], preferred_element_type=F32)


def matmul(a, w, tm, tn):
    n, k = a.shape
    nn = w.shape[1]
    return pl.pallas_call(
        _mm_body, grid=(n // tm, nn // tn),
        in_specs=[pl.BlockSpec((tm, k), lambda i, j: (i, 0)),
                  pl.BlockSpec((k, tn), lambda i, j: (0, j))],
        out_specs=pl.BlockSpec((tm, tn), lambda i, j: (i, j)),
        out_shape=jax.ShapeDtypeStruct((n, nn), F32),
        compiler_params=_cp(("parallel", "parallel")), name="in_proj",
    )(a, w)


def _outln_body(a_ref, b_ref, wa_ref, wb_ref, x_ref, gt_ref, g_ref, be_ref, sc_ref, sh_ref, *rest):
    xo_ref, ho_ref = rest[-2:]
    y = (jnp.dot(a_ref[...], wa_ref[...], preferred_element_type=F32)
         + jnp.dot(b_ref[...], wb_ref[...], preferred_element_type=F32))
    xn = _layer_norm(ALPHA * x_ref[...] + (1.0 + gt_ref[...]) * y, g_ref[...], be_ref[...])
    xo_ref[...] = xn
    ho_ref[...] = xn * (1.0 + sc_ref[...]) + sh_ref[...]


def out_proj_ln(a, b, wa, wb, x, grp, l, ln_g, ln_b, h_rows, h_row0, h_all=None):
    n, d = x.shape
    ka, kb = a.shape[1], b.shape[1]
    tm = min(grp.tm, 256)
    g2 = _Group(grp.n, grp.t, tm, grp.modarr, grp.per_row)
    row = pl.BlockSpec((tm, d), lambda i: (i, 0))
    vec = pl.BlockSpec((1, d), lambda i: (0, 0))
    in_specs = [pl.BlockSpec((tm, ka), lambda i: (i, 0)), pl.BlockSpec((tm, kb), lambda i: (i, 0)),
                pl.BlockSpec((ka, d), lambda i: (0, 0)), pl.BlockSpec((kb, d), lambda i: (0, 0)),
                row, g2.mod(l, 2), vec, vec, g2.mod(l, 4), g2.mod(l, 3)]
    args = [a, b, wa, wb, x, grp.modarr, ln_g.reshape(1, d), ln_b.reshape(1, d), grp.modarr, grp.modarr]
    aliases = {}
    if h_all is not None:
        in_specs.append(pl.BlockSpec(memory_space=pl.ANY))
        args.append(h_all)
        aliases = {len(args) - 1: 1}
    return pl.pallas_call(
        _outln_body, grid=(n // tm,),
        in_specs=in_specs,
        out_specs=[row, pl.BlockSpec((tm, d), lambda i: (i + h_row0 // tm, 0))],
        out_shape=[jax.ShapeDtypeStruct((n, d), F32), jax.ShapeDtypeStruct((h_rows, d), F32)],
        input_output_aliases=aliases,
        compiler_params=_cp(("parallel",)), name="out_proj_ln",
    )(*args)


def _comb_body(*refs, has_next):
    if has_next:
        y0_ref, y1_ref, gate_ref, x_ref, gt_ref, g_ref, be_ref, sc_ref, sh_ref, xo_ref, ho_ref = refs
    else:
        y0_ref, y1_ref, gate_ref, x_ref, gt_ref, g_ref, be_ref, xo_ref = refs
    gate = gate_ref[...]
    y = y0_ref[...] * gate[:, 0:1] + y1_ref[...] * gate[:, 1:2]
    xn = _layer_norm(ALPHA * x_ref[...] + (1.0 + gt_ref[...]) * y, g_ref[...], be_ref[...])
    xo_ref[...] = xn
    if has_next:
        ho_ref[...] = (xn * (1.0 + sc_ref[...]) + sh_ref[...]).astype(ho_ref.dtype)


def moe_combine_ln(ys, row0, row1, gate, x, grp, l, ln_g, ln_b, has_next):
    n, d = x.shape
    tm = min(grp.tm, 256)
    g2 = _Group(grp.n, grp.t, tm, grp.modarr, grp.per_row)
    nt = n // tm
    row = pl.BlockSpec((tm, d), lambda i: (i, 0))
    vec = pl.BlockSpec((1, d), lambda i: (0, 0))
    in_specs = [pl.BlockSpec((tm, d), lambda i: (i + row0 // tm, 0)),
                pl.BlockSpec((tm, d), lambda i: (i + row1 // tm, 0)), pl.BlockSpec((tm, 2), lambda i: (i, 0)),
                row, g2.mod(l, 5), vec, vec]
    args = [ys, ys, gate, x, grp.modarr, ln_g.reshape(1, d), ln_b.reshape(1, d)]
    out_specs = [row]
    out_shape = [jax.ShapeDtypeStruct((n, d), F32)]
    if has_next:
        in_specs += [g2.mod(l + 1, 1), g2.mod(l + 1, 0)]
        args += [grp.modarr, grp.modarr]
        out_specs.append(row)
        out_shape.append(jax.ShapeDtypeStruct((n, d), BF16))
    res = pl.pallas_call(
        functools.partial(_comb_body, has_next=has_next), grid=(nt,),
        in_specs=in_specs, out_specs=out_specs, out_shape=out_shape,
        compiler_params=_cp(("parallel",)), name="moe_combine_ln",
    )(*args)
    return (res[0], res[1]) if has_next else (res[0], None)


def _router_body(h_ref, w_ref, b_ref, cnt0_ref, eid_ref, gate_ref, rank_ref, cnt_ref, run_ref):
    i = pl.program_id(0)

    @pl.when(i == 0)
    def _():
        run_ref[...] = cnt0_ref[...]

    tm = h_ref.shape[0]
    logits = _dot3(h_ref[...], w_ref[...]) + b_ref[...]
    lg = logits[:, :LANES]
    le = logits[:, LANES:]
    lane = _iota((tm, LANES), 1).astype(F32)
    neg = jnp.float32(-jnp.inf)
    lgm = jnp.where(lane < N_GROUPS, lg, neg)
    mg = jnp.max(lgm, axis=-1, keepdims=True)
    gidx = jnp.min(jnp.where(lgm == mg, lane, float(LANES)), axis=-1, keepdims=True)
    p_g = 1.0 / jnp.sum(jnp.exp(lgm - mg), axis=-1, keepdims=True)
    lo = gidx * EXPERTS_PER_GROUP
    in_grp = (lane >= lo) & (lane < lo + EXPERTS_PER_GROUP)
    lem = jnp.where(in_grp, le, neg)
    v1 = jnp.max(lem, axis=-1, keepdims=True)
    i1 = jnp.min(jnp.where(lem == v1, lane, float(LANES)), axis=-1, keepdims=True)
    lem2 = jnp.where(lane == i1, neg, lem)
    v2 = jnp.max(lem2, axis=-1, keepdims=True)
    i2 = jnp.min(jnp.where(lem2 == v2, lane, float(LANES)), axis=-1, keepdims=True)
    e2 = jnp.exp(v2 - v1)
    g1 = p_g / (1.0 + e2)
    g2 = p_g * e2 / (1.0 + e2)
    oh1 = jnp.where(lane == i1, 1.0, 0.0)
    oh2 = jnp.where(lane == i2, 1.0, 0.0)
    comb = oh1 + oh2
    tri = jnp.where(_iota((tm, tm), 0) > _iota((tm, tm), 1), 1.0, 0.0).astype(BF16)
    before = jnp.dot(tri, comb.astype(BF16), preferred_element_type=F32) + run_ref[...]
    r1 = jnp.sum(before * oh1, axis=-1, keepdims=True)
    r2 = jnp.sum(before * oh2, axis=-1, keepdims=True)
    run_ref[...] = run_ref[...] + jnp.sum(comb, axis=0, keepdims=True)
    eid_ref[:, 0:1] = i1.astype(I32)
    eid_ref[:, 1:2] = i2.astype(I32)
    gate_ref[:, 0:1] = g1
    gate_ref[:, 1:2] = g2
    rank_ref[:, 0:1] = r1.astype(I32)
    rank_ref[:, 1:2] = r2.astype(I32)
    cnt_ref[...] = run_ref[...]


def router(h, row0, n, wr, br, tm, cnt0):
    d = h.shape[1]
    two = pl.BlockSpec((tm, 2), lambda i: (i, 0))
    one = pl.BlockSpec((1, LANES), lambda i: (0, 0))
    return pl.pallas_call(
        _router_body, grid=(n // tm,),
        in_specs=[pl.BlockSpec((tm, d), lambda i: (i + row0 // tm, 0)),
                  pl.BlockSpec((d, 2 * LANES), lambda i: (0, 0)),
                  pl.BlockSpec((1, 2 * LANES), lambda i: (0, 0)), one],
        out_specs=[two, two, two, one],
        out_shape=[jax.ShapeDtypeStruct((n, 2), I32), jax.ShapeDtypeStruct((n, 2), F32),
                   jax.ShapeDtypeStruct((n, 2), I32), jax.ShapeDtypeStruct((1, LANES), F32)],
        scratch_shapes=[pltpu.VMEM((1, LANES), F32)],
        compiler_params=_cp(("arbitrary",)), name="router",
    )(h, wr, br, cnt0)


def moe_plan(eid, rank, cnt, dst_of_entry, spare_row0):
    nk = 2 * eid.shape[0]
    n_blocks = -(-nk // MOE_ROWS) + N_EXPERTS
    n_rows = n_blocks * MOE_ROWS
    counts = cnt[0, :N_EXPERTS].astype(I32)
    padded = (counts + MOE_ROWS - 1) // MOE_ROWS * MOE_ROWS
    pad_end = jnp.cumsum(padded)
    pad_start = pad_end - padded
    dest = (pad_start[eid] + rank).reshape(-1)
    src = jnp.full((n_rows + MOE_ROWS,), -1, I32).at[dest].set(jnp.arange(nk, dtype=I32), unique_indices=True,
                                                               mode='promise_in_bounds')
    valid = src >= 0
    pos = jnp.arange(n_rows + MOE_ROWS, dtype=I32)
    safe = jnp.maximum(src, 0)
    tok = jnp.where(valid, safe >> 1, 0)
    dst = jnp.where(valid, dst_of_entry[safe], spare_row0 + pos % (2 * MOE_ROWS))
    dst = jnp.concatenate([spare_row0 + MOE_ROWS + jnp.arange(MOE_ROWS, dtype=I32), dst])
    n_used = pad_end[-1] // MOE_ROWS
    blk = jnp.arange(n_blocks, dtype=I32)
    be = jnp.minimum(jnp.sum((blk * MOE_ROWS)[:, None] >= pad_end[None, :], axis=-1), N_EXPERTS - 1).astype(I32)
    be = jnp.where(blk < n_used, be, be[jnp.maximum(n_used - 1, 0)])
    return tok, dst, be, n_used.reshape(1).astype(I32), n_blocks


def _moe_body(tok_ref, dst_ref, be_ref, nu_ref, h_hbm, w1_ref, w3_ref, w2_ref, ys_hbm,
              xbuf, ybuf, w1b, w3b, w2b, gsem, ssem, *, spare_row0):
    b = pl.program_id(0)
    n_used = nu_ref[0]
    slot = b % 2

    def gather_rows(blk, sl):
        for i in range(MOE_ROWS):
            pltpu.make_async_copy(h_hbm.at[pl.ds(tok_ref[blk * MOE_ROWS + i], 1)], xbuf.at[sl, pl.ds(i, 1)],
                                  gsem.at[sl]).start()

    def wait_gather(sl):
        pltpu.make_async_copy(xbuf.at[sl], xbuf.at[sl], gsem.at[sl]).wait()

    def scatter_rows(blk, sl):
        for i in range(MOE_ROWS):
            pltpu.make_async_copy(ybuf.at[sl, pl.ds(i, 1)],
                                  ys_hbm.at[pl.ds(dst_ref[(blk + 1) * MOE_ROWS + i], 1)], ssem.at[sl]).start()

    def wait_scatter(sl):
        pltpu.make_async_copy(ybuf.at[sl], ybuf.at[sl], ssem.at[sl]).wait()

    @pl.when(b == 0)
    def _():
        ybuf[...] = jnp.zeros_like(ybuf)
        first_spare = pltpu.make_async_copy(ybuf.at[0], ys_hbm.at[pl.ds(spare_row0, MOE_ROWS)], ssem.at[0])
        first_spare.start()
        first_spare.wait()
        gather_rows(0, 0)

    @pl.when(b < n_used)
    def _():
        @pl.when(jnp.logical_or(b == 0, be_ref[b] != be_ref[jnp.maximum(b - 1, 0)]))
        def _():
            w1b[...] = w1_ref[...].astype(BF16)
            w3b[...] = w3_ref[...].astype(BF16)
            w2b[...] = w2_ref[...].astype(BF16)

        wait_gather(slot)
        x = xbuf[slot].astype(BF16)
        h1 = jnp.dot(x, w1b[...], preferred_element_type=F32)
        h3 = jnp.dot(x, w3b[...], preferred_element_type=F32)
        mid = (_silu(h1) * h3).astype(BF16)
        y = jnp.dot(mid, w2b[...], preferred_element_type=F32)
        gather_rows(b + 1, 1 - slot)
        scatter_rows(b - 1, 1 - slot)

        @pl.when(b >= 1)
        def _():
            wait_scatter(slot)

        ybuf[slot] = y

        @pl.when(b == n_used - 1)
        def _():
            scatter_rows(b, slot)
            wait_gather(1 - slot)
            wait_scatter(1 - slot)
            wait_scatter(slot)


def moe_ffn(h, tok, dst, be, n_used, n_blocks, w1, w3, w2, layer):
    n, d = h.shape
    de = w1.shape[-1]
    wspec_in = pl.BlockSpec((None, None, d, de), lambda b, tok, dst, be, nu: (layer, be[b], 0, 0))
    wspec_out = pl.BlockSpec((None, None, de, d), lambda b, tok, dst, be, nu: (layer, be[b], 0, 0))
    return pl.pallas_call(
        functools.partial(_moe_body, spare_row0=2 * n),
        grid_spec=pltpu.PrefetchScalarGridSpec(
            num_scalar_prefetch=4, grid=(n_blocks,),
            in_specs=[pl.BlockSpec(memory_space=pl.ANY), wspec_in, wspec_in, wspec_out],
            out_specs=pl.BlockSpec(memory_space=pl.ANY),
            scratch_shapes=[pltpu.VMEM((2, MOE_ROWS, d), F32), pltpu.VMEM((2, MOE_ROWS, d), F32),
                            pltpu.VMEM((d, de), BF16), pltpu.VMEM((d, de), BF16), pltpu.VMEM((de, d), BF16),
                            pltpu.SemaphoreType.DMA((2,)), pltpu.SemaphoreType.DMA((2,))]),
        out_shape=jax.ShapeDtypeStruct((2 * n + 2 * MOE_ROWS, d), F32),
        compiler_params=_cp(("arbitrary",), MOE_VMEM_LIMIT), name="moe_ffn",
    )(tok, dst, be, n_used, h, w1, w3, w2)


def _conv_body(va_ref, bg_ref, cg_ref, w_ref, buf_ref, o_ref, nb_ref):
    u = cg_ref[...] * va_ref[...]
    t = u.shape[0]
    row = _iota(u.shape, 0)
    b0 = buf_ref[0:1, :]
    b1 = buf_ref[1:2, :]
    u1 = jnp.where(row == 0, b1, pltpu.roll(u, 1, axis=0))
    u2 = jnp.where(row == 0, b0, jnp.where(row == 1, b1, pltpu.roll(u, 2, axis=0)))
    w = w_ref[...]
    y = w[0:1] * u2 + w[1:2] * u1 + w[2:3] * u
    o_ref[...] = (bg_ref[...] * y).astype(o_ref.dtype)
    nb_ref[...] = u[t - 2:t, :]


def conv_seq(z3, conv_w, buf, tc=256):
    bsz, t, _ = z3.shape
    nc = D_CONV // tc
    col = lambda off: pl.BlockSpec((None, t, tc), lambda b, j: (b, 0, off + j))
    return pl.pallas_call(
        _conv_body, grid=(bsz, nc),
        in_specs=[col(0), col(nc), col(2 * nc),
                  pl.BlockSpec((CONV_WIDTH, tc), lambda b, j: (0, j)),
                  pl.BlockSpec((None, 2, tc), lambda b, j: (b, 0, j))],
        out_specs=[pl.BlockSpec((None, t, tc), lambda b, j: (b, 0, j)),
                   pl.BlockSpec((None, 2, tc), lambda b, j: (b, 0, j))],
        out_shape=[jax.ShapeDtypeStruct((bsz, t, D_CONV), BF16), jax.ShapeDtypeStruct((bsz, 2, D_CONV), F32)],
        compiler_params=_cp(("parallel", "parallel")), name="conv_seq",
    )(z3, z3, z3, conv_w, buf)


def _conv_step_body(va_ref, bg_ref, cg_ref, w_ref, b0_ref, b1_ref, o_ref, u_ref):
    u = cg_ref[...] * va_ref[...]
    w = w_ref[...]
    y = w[0:1] * b0_ref[...] + w[1:2] * b1_ref[...] + w[2:3] * u
    o_ref[...] = (bg_ref[...] * y).astype(o_ref.dtype)
    u_ref[...] = u


def conv_step(z2, conv_w, buf):
    bsz = z2.shape[0]
    col = lambda j: pl.BlockSpec((bsz, D_CONV), lambda i: (0, j))
    full = pl.BlockSpec((bsz, D_CONV), lambda i: (0, 0))
    a, u = pl.pallas_call(
        _conv_step_body, grid=(1,),
        in_specs=[col(0), col(1), col(2), pl.BlockSpec((CONV_WIDTH, D_CONV), lambda i: (0, 0)), full, full],
        out_specs=[full, full],
        out_shape=[jax.ShapeDtypeStruct((bsz, D_CONV), BF16), jax.ShapeDtypeStruct((bsz, D_CONV), F32)],
        compiler_params=_cp(("arbitrary",)), name="conv_step",
    )(z2, z2, z2, conv_w, buf[:, 0], buf[:, 1])
    return a, jnp.stack([buf[:, 1], u], axis=1)


def _hgrn_body(q_ref, f_ref, i_ref, g_ref, lb_ref, nw_ref, s0_ref, o_ref, so_ref, st_ref,
               *, chunk, sub, nchunk, nhead, t_real, layer, indep):
    tstep = pl.program_id(2)
    tb = chunk * nchunk
    dk = HGRN_DK

    if not indep:
        @pl.when(tstep == 0)
        def _():
            for hd in range(nhead):
                st_ref[hd] = s0_ref[hd].T

    lbl = lb_ref[...]
    e = jnp.exp(lbl - jnp.max(lbl, axis=0, keepdims=True))
    sm = e / jnp.sum(e, axis=0, keepdims=True)
    lb = jnp.zeros((1, lbl.shape[1]), F32)
    for r in range(1, layer + 1):
        lb = lb + sm[r:r + 1]
    tri = jnp.where(_iota((chunk, chunk), 0) >= _iota((chunk, chunk), 1), 1.0, 0.0).astype(BF16)
    nw = nw_ref[...]
    neg = jnp.float32(-jnp.inf)

    fr = f_ref[...]
    v = i_ref[...]
    logf = jnp.log(lb + (1.0 - lb) * _sigmoid(fr))
    kin = (1.0 - lb) * _sigmoid(-fr)
    if t_real is not None:
        row = _iota(fr.shape, 0)
        live = ((row & (chunk - 1)) if indep else (tstep * tb + row)) < t_real
        logf = jnp.where(live, logf, 0.0)
        kin = jnp.where(live, kin, 0.0)
    qs = _silu(q_ref[...])
    causal = _iota((sub, sub, 1), 1) <= _iota((sub, sub, 1), 0)
    cuts = [(slice(c * chunk, (c + 1) * chunk), slice(hd * dk, (hd + 1) * dk))
            for c in range(nchunk) for hd in range(nhead)]
    bb_all = [_dot_exact_lhs(tri, logf[c * chunk:(c + 1) * chunk]) for c in range(nchunk)]
    bb_l = [bb_all[c][:, hd * dk:(hd + 1) * dk] for c in range(nchunk) for hd in range(nhead)]
    qs_l = [qs[rows, ls] for rows, ls in cuts]
    kin_l = [kin[rows, ls] for rows, ls in cuts]
    v_l = [v[rows, ls] for rows, ls in cuts]
    bl_l = [bb[chunk - 1:chunk] for bb in bb_l]
    kv_l = [_dot(vc, kc * jnp.exp(bl - bb), TN) for vc, kc, bl, bb in zip(v_l, kin_l, bl_l, bb_l)]
    intra_l = []
    for qc, kc, vc, bb in zip(qs_l, kin_l, v_l, bb_l):
        parts = []
        for blk in range(chunk // sub):
            lo = blk * sub
            qi, bi, ki, vi = qc[lo:lo + sub], bb[lo:lo + sub], kc[lo:lo + sub], vc[lo:lo + sub]
            d = bi[:, None, :] - bi[None, :, :]
            sc = jnp.sum(qi[:, None, :] * ki[None, :, :] * jnp.exp(jnp.where(causal, d, neg)), axis=-1)
            oi = _dot(sc, vi)
            if blk > 0:
                anchor = bb[lo - 1:lo]
                qt = qi * jnp.exp(bi - anchor)
                kt = kc[:lo] * jnp.exp(anchor - bb[:lo])
                oi = oi + _dot(_dot(qt, kt, NT), vc[:lo])
            parts.append(oi)
        intra_l.append(parts[0] if len(parts) == 1 else jnp.concatenate(parts, axis=0))
    if indep:
        st_l = [s0_ref[c, hd].T for c in range(nchunk) for hd in range(nhead)]
        for idx, (st, bl, kv) in enumerate(zip(st_l, bl_l, kv_l)):
            so_ref[idx // nhead, idx % nhead] = (st * jnp.exp(bl) + kv).T
    else:
        st_l = []
        cur = [st_ref[hd] for hd in range(nhead)]
        for c in range(nchunk):
            for hd in range(nhead):
                idx = c * nhead + hd
                st_l.append(cur[hd])
                cur[hd] = cur[hd] * jnp.exp(bl_l[idx]) + kv_l[idx]
        for hd in range(nhead):
            st_ref[hd] = cur[hd]
    o_l = [_dot(qc * jnp.exp(bb), sc, NT) + oi for qc, bb, sc, oi in zip(qs_l, bb_l, st_l, intra_l)]
    o_l = [o * lax.rsqrt(jnp.mean(o * o, axis=-1, keepdims=True) + RMS_EPS) * nw for o in o_l]
    rows_l = [o_l[c * nhead] if nhead == 1 else jnp.concatenate(o_l[c * nhead:(c + 1) * nhead], axis=1)
              for c in range(nchunk)]
    o = rows_l[0] if nchunk == 1 else jnp.concatenate(rows_l, axis=0)
    o_ref[...] = (o * _silu(g_ref[...])).astype(o_ref.dtype)

    if not indep:
        @pl.when(tstep == pl.num_programs(2) - 1)
        def _():
            for hd in range(nhead):
                so_ref[hd] = st_ref[hd].T


def hgrn_seq(z3, hgrn_lb, norm_w, s0, layer, chunk, nchunk, nhead=1, t_real=None, indep=False):
    bsz, t, _ = z3.shape
    tb = chunk * nchunk
    sb = None
    if indep:
        sb = nchunk
        z3 = z3.reshape(bsz // nchunk, tb, z3.shape[2])
        bsz, t = bsz // nchunk, tb
    wl = nhead * LANES
    groups = HGRN_HEADS // nhead
    qoff = 3 * D_CONV // wl
    col = lambda k: pl.BlockSpec((None, tb, wl), lambda b, h, s: (b, s, qoff + k * groups + h))
    st = pl.BlockSpec((sb, nhead, HGRN_DK, HGRN_DV), lambda b, h, s: (b, h, 0, 0))
    body = functools.partial(_hgrn_body, chunk=chunk, sub=min(HGRN_SUB, chunk), nchunk=nchunk, nhead=nhead, t_real=t_real,
                             layer=layer, indep=indep)
    nseq = bsz * (sb or 1)
    o, so = pl.pallas_call(
        body, grid=(bsz, groups, t // tb),
        in_specs=[col(0), col(1), col(2), col(3),
                  pl.BlockSpec((N_EVEN, wl), lambda b, h, s: (0, h)),
                  pl.BlockSpec((1, HGRN_DV), lambda b, h, s: (0, 0)), st],
        out_specs=[pl.BlockSpec((None, tb, wl), lambda b, h, s: (b, s, h)), st],
        out_shape=[jax.ShapeDtypeStruct((bsz, t, D_HV), BF16),
                   jax.ShapeDtypeStruct((nseq, HGRN_HEADS, HGRN_DK, HGRN_DV), F32)],
        scratch_shapes=[pltpu.VMEM((nhead, HGRN_DV, HGRN_DK), F32)],
        compiler_params=_cp(("parallel", "parallel", "arbitrary")), name="hgrn_seq",
    )(z3, z3, z3, z3, hgrn_lb, norm_w.reshape(1, HGRN_DV), s0)
    return o.reshape(nseq, chunk if indep else t, D_HV), so


def _attn_body(sink_ref, zq_ref, zkv_ref, cos_ref, sin_ref, ck_ref, cv_ref, o_ref, nk_ref, nv_ref, kp_ref, vp_ref,
               *, prev_valid, t_real, tq):
    i = pl.program_id(1)
    w = WINDOW

    @pl.when(i == 0)
    def _():
        kp_ref[...] = ck_ref[...]
        vp_ref[...] = cv_ref[...]

    cos = cos_ref[...]
    sin = sin_ref[...]

    def rope(x):
        width = x.shape[1]
        reps = width // LANES
        first = (_iota(x.shape, 1) & (HEAD_DIM - 1)) < (HEAD_DIM // 2)
        rot = jnp.where(first, pltpu.roll(x, width - HEAD_DIM // 2, axis=1), pltpu.roll(x, HEAD_DIM // 2, axis=1))
        return x * jnp.tile(cos, (1, reps)) + rot * jnp.tile(sin, (1, reps))

    kv = zkv_ref[...]
    qr = rope(zq_ref[...]) * (HEAD_DIM ** -0.5)
    kr = rope(kv[:, :D_KV])
    v = kv[:, D_KV:]
    kprev = kp_ref[...]
    vprev = vp_ref[...]

    grp = ATTN_HEADS // ATTN_KV_HEADS
    rows = _iota((grp * tq, w + tq), 0)
    cols = _iota((grp * tq, w + tq), 1)
    delta = (rows & (tq - 1)) + w - cols
    valid = (delta >= 0) & (delta <= w)
    if not prev_valid:
        valid = valid & (cols >= jnp.where(i > 0, 0, w))
    head_of_row = _iota((grp * tq, 1), 0) >> int(math.log2(tq))
    neg = jnp.float32(-jnp.inf)
    outs = []
    for g in range(ATTN_KV_HEADS):
        ls = slice(g * HEAD_DIM, (g + 1) * HEAD_DIM)
        kg = jnp.concatenate([kprev[:, ls], kr[:, ls]], axis=0)
        vg = jnp.concatenate([vprev[:, ls], v[:, ls]], axis=0)
        qg = jnp.concatenate([qr[:, (grp * g + hh) * HEAD_DIM:(grp * g + hh + 1) * HEAD_DIM] for hh in range(grp)],
                             axis=0)
        s = jnp.where(valid, _dot(qg, kg, NT), neg)
        sink = jnp.zeros((grp * tq, 1), F32)
        for hh in range(grp):
            sink = jnp.where(head_of_row == hh, sink_ref[grp * g + hh], sink)
        m = jnp.maximum(jnp.max(s, axis=-1, keepdims=True), sink)
        p = jnp.exp(s - m)
        p = p / (jnp.sum(p, axis=-1, keepdims=True) + jnp.exp(sink - m))
        og = _dot(p, vg)
        outs += [og[hh * tq:(hh + 1) * tq] for hh in range(grp)]
    o_ref[...] = jnp.concatenate(outs, axis=1).astype(o_ref.dtype)
    if t_real == w:
        kp_ref[...] = kr
        vp_ref[...] = v

    @pl.when(i == pl.num_programs(1) - 1)
    def _():
        if t_real == w:
            nk_ref[...] = kr
            nv_ref[...] = v
        else:
            last = _iota(kprev.shape, 0) == w - 1
            nk_ref[...] = jnp.where(last, kr[0:1], pltpu.roll(kprev, w - 1, axis=0))
            nv_ref[...] = jnp.where(last, v[0:1], pltpu.roll(vprev, w - 1, axis=0))


def attn_seq(z3, sinks, cos, sin, cache_k, cache_v, prev_valid, t_real, tq=WINDOW):
    bsz, t, _ = z3.shape
    nb = t // tq
    assert tq == WINDOW or nb == 1
    cache = pl.BlockSpec((None, WINDOW, D_KV), lambda b, i: (b, 0, 0))
    tab = pl.BlockSpec((tq, LANES), lambda b, i: (i, 0))
    body = functools.partial(_attn_body, prev_valid=prev_valid, t_real=t_real, tq=tq)
    return pl.pallas_call(
        body, grid=(bsz, nb),
        in_specs=[pl.BlockSpec(memory_space=pltpu.SMEM),
                  pl.BlockSpec((None, tq, D_Q), lambda b, i: (b, i, 0)),
                  pl.BlockSpec((None, tq, 2 * D_KV), lambda b, i: (b, i, D_Q // (2 * D_KV))),
                  tab, tab, cache, cache],
        out_specs=[pl.BlockSpec((None, tq, D_Q), lambda b, i: (b, i, 0)), cache, cache],
        out_shape=[jax.ShapeDtypeStruct((bsz, t, D_Q), BF16),
                   jax.ShapeDtypeStruct((bsz, WINDOW, D_KV), F32), jax.ShapeDtypeStruct((bsz, WINDOW, D_KV), F32)],
        scratch_shapes=[pltpu.VMEM((WINDOW, D_KV), F32), pltpu.VMEM((WINDOW, D_KV), F32)],
        compiler_params=_cp(("parallel", "arbitrary")), name="attn_seq",
    )(sinks, z3, z3, cos, sin, cache_k, cache_v)


def rope_tables(pos):
    half = HEAD_DIM // 2
    inv = jnp.exp(-math.log(ROPE_THETA) * jnp.arange(half, dtype=F32) / half)
    ang = pos.astype(F32)[:, None] * inv[None, :]
    c, s = jnp.cos(ang), jnp.sin(ang)
    cos = jnp.concatenate([c, c, c, c], axis=1)
    sin = jnp.concatenate([-s, s, -s, s], axis=1)
    return cos, sin


def _rwkv_body(r_ref, k_ref, v_ref, wa_ref, gd_ref, sr_ref, sk_ref, sv_ref, swa_ref, sgd_ref,
               mr_ref, mk_ref, mv_ref, mwa_ref, mgd_ref, w0_ref, w2_ref, a0_ref, a2_ref, g2_ref,
               kkp_ref, ka_ref, rk_ref, lg_ref, lbias_ref, s0_ref, o_ref, so_ref, st_ref, prev_ref, prevw_ref,
               *, chunk, nchunk, npair, t_real, indep):
    tstep = pl.program_id(2)
    n = RWKV_N
    ln = chunk
    tb = chunk * nchunk

    if not indep:
        @pl.when(tstep == 0)
        def _():
            st_ref[...] = s0_ref[...]
            prev_ref[0:1, :] = sr_ref[...]
            prev_ref[1:2, :] = sk_ref[...]
            prev_ref[2:3, :] = sv_ref[...]
            prevw_ref[0:1, :] = swa_ref[...]
            prevw_ref[1:2, :] = sgd_ref[...]

    def mix(x_ref, mu_ref, s_ref, p_ref, idx):
        x = x_ref[...]
        row = _iota(x.shape, 0)
        if indep:
            before = jnp.broadcast_to(s_ref[...], (nchunk, ln, x.shape[1])).reshape(tb, x.shape[1])
            shifted = jnp.where((row & (ln - 1)) == 0, before, pltpu.roll(x, 1, axis=0))
        else:
            shifted = jnp.where(row == 0, p_ref[idx:idx + 1, :], pltpu.roll(x, 1, axis=0))
            p_ref[idx:idx + 1, :] = x[tb - 1:tb, :]
        return x + mu_ref[...] * (shifted - x)

    r = mix(r_ref, mr_ref, sr_ref, prev_ref, 0)
    kr = mix(k_ref, mk_ref, sk_ref, prev_ref, 1)
    vr = mix(v_ref, mv_ref, sv_ref, prev_ref, 2)
    wa = mix(wa_ref, mwa_ref, swa_ref, prevw_ref, 0)
    gd = mix(gd_ref, mgd_ref, sgd_ref, prevw_ref, 1)[:, :RWKV_RANK]
    wd = wa[:, :RWKV_RANK]
    ad = wa[:, RWKV_RANK:]

    w_log = -_softplus(-(w0_ref[...] + _dot3(jnp.tanh(wd), w2_ref[...]))) - 0.5
    logw = -jnp.exp(w_log)
    a = _sigmoid(a0_ref[...] + _dot3(ad, a2_ref[...]))
    gate = _dot3(_sigmoid(gd), g2_ref[...])

    sh = int(math.log2(n))
    seg = jnp.where((_iota((LANES, LANES), 0) >> sh) == (_iota((LANES, LANES), 1) >> sh), 1.0, 0.0).astype(BF16)

    def segsum(x):
        tiles = [_dot_exact_rhs(x[:, i * LANES:(i + 1) * LANES], seg) for i in range(npair)]
        return tiles[0] if npair == 1 else jnp.concatenate(tiles, axis=1)

    kkv = kr * kkp_ref[...]
    kk = kkv / jnp.maximum(jnp.sqrt(segsum(kkv * kkv)), 1e-12)
    kf = kr * (1.0 + (a - 1.0) * ka_ref[...])
    bonus = segsum(r * kf * rk_ref[...]) * vr
    if t_real is not None:
        row = _iota(logw.shape, 0)
        live = ((row & (ln - 1)) if indep else (tstep * tb + row)) < t_real
        zero = jnp.zeros_like(logw)
        logw = jnp.where(live, logw, zero)
        kk = jnp.where(live, kk, zero)
        kf = jnp.where(live, kf, zero)
        vr_s = jnp.where(live, vr, zero)
    else:
        vr_s = vr

    tri = jnp.where(_iota((ln, ln), 0) >= _iota((ln, ln), 1), 1.0, 0.0).astype(BF16)
    rr = _iota((ln, ln), 0)
    cc = _iota((ln, ln), 1)
    strict = rr > cc
    incl = rr >= cc
    eye = jnp.where(rr == cc, 1.0, 0.0)
    eye_n = jnp.where(_iota((n, n), 0) == _iota((n, n), 1), 1.0, 0.0)

    nhead = 2 * npair
    am_l, rm_l, bp_l, kp_l, bl_l, kl_l, v_l, gl_l = [], [], [], [], [], [], [], []
    for ci in range(nchunk):
        rows = slice(ci * ln, (ci + 1) * ln)
        lw = logw[rows]
        c = _dot_exact_lhs(tri, lw)
        gam = jnp.exp(c)
        ginv = jnp.exp(-c)
        am = -kk[rows] * jnp.exp(c - lw)
        rm = r[rows] * gam
        bp = kk[rows] * a[rows] * ginv
        kp = kf[rows] * ginv
        g_last = gam[ln - 1:ln, :]
        bpl = bp * g_last
        kpl = kp * g_last
        vv = vr_s[rows]
        for hd in range(nhead):
            ls = slice(hd * n, (hd + 1) * n)
            am_l.append(am[:, ls]); rm_l.append(rm[:, ls]); bp_l.append(bp[:, ls]); kp_l.append(kp[:, ls])
            bl_l.append(bpl[:, ls]); kl_l.append(kpl[:, ls]); v_l.append(vv[:, ls]); gl_l.append(g_last[:, ls])
    each = lambda fn, *ls: [fn(*xs) for xs in zip(*ls)]
    pw_l = each(lambda am, rm, bp, kp: _dot3(jnp.concatenate([am, rm], axis=0),
                                             jnp.concatenate([bp, kp], axis=0), NT), am_l, rm_l, bp_l, kp_l)
    m_l = each(lambda pw: jnp.where(strict, pw[:ln, :ln], 0.0), pw_l)
    nm_l = each(lambda pw: jnp.where(strict, pw[:ln, ln:], 0.0), pw_l)
    qb_l = each(lambda pw: jnp.where(incl, pw[ln:, :ln], 0.0), pw_l)
    qk_l = each(lambda pw: jnp.where(incl, pw[ln:, ln:], 0.0), pw_l)
    nq_l = each(lambda nm, qk, v: _dot(jnp.concatenate([nm, qk], axis=0), v), nm_l, qk_l, v_l)
    nv_l = [x[:ln] for x in nq_l]
    qkv_l = [x[ln:] for x in nq_l]
    kv_l = each(lambda v, kl: _dot(v, kl, TN), v_l, kl_l)
    tinv_l = each(lambda m: eye + m, m_l)
    p_l = m_l
    for _ in range(int(math.log2(ln)) - 1):
        p_l = each(lambda p: _dot(p, p), p_l)
        tinv_l = each(lambda t, p: t + _dot(t, p), tinv_l, p_l)
    wu_l = each(lambda t, am, nv: _dot(t, jnp.concatenate([am, nv], axis=1)), tinv_l, am_l, nv_l)
    ac_l = each(lambda wu, bl: _dot(wu, bl, TN), wu_l, bl_l)
    a_l = each(lambda ac, gl: eye_n * gl + ac[:n], ac_l, gl_l)
    c_l = each(lambda ac, kv: ac[n:] + kv, ac_l, kv_l)
    ro_l = each(_dot, qb_l, wu_l)
    rt_l = each(lambda rm, ro: rm + ro[:, :n], rm_l, ro_l)
    o0_l = each(lambda ro, qkv: ro[:, n:] + qkv, ro_l, qkv_l)

    o_rows = []
    if indep:
        s_l = [s0_ref[ci, hd] for ci in range(nchunk) for hd in range(nhead)]
        o_all = each(lambda rt, s, o0: _dot(rt, s, NT) + o0, rt_l, s_l, o0_l)
        s_l = each(lambda s, am, cm: _dot(s, am) + cm, s_l, a_l, c_l)
        for ci in range(nchunk):
            o_rows.append(o_all[ci * nhead:(ci + 1) * nhead])
            for hd in range(nhead):
                so_ref[ci, hd] = s_l[ci * nhead + hd]
    else:
        s_l = [st_ref[hd] for hd in range(nhead)]
        for ci in range(nchunk):
            sl = slice(ci * nhead, (ci + 1) * nhead)
            o_rows.append(each(lambda rt, s, o0: _dot(rt, s, NT) + o0, rt_l[sl], s_l, o0_l[sl]))
            s_l = each(lambda s, am, cm: _dot(s, am) + cm, s_l, a_l[sl], c_l[sl])
        for hd in range(nhead):
            st_ref[hd] = s_l[hd]
    cols = [o_rows[0][hd] if nchunk == 1 else jnp.concatenate([o_rows[ci][hd] for ci in range(nchunk)], axis=0)
            for hd in range(nhead)]
    o = jnp.concatenate(cols, axis=1)
    mu_o = segsum(o) * (1.0 / n)
    dlt = o - mu_o
    var_o = segsum(dlt * dlt) * (1.0 / n)
    o = dlt * lax.rsqrt(var_o + RWKV_GN_EPS) * lg_ref[...] + lbias_ref[...]
    o_ref[...] = ((o + bonus) * gate).astype(o_ref.dtype)

    if not indep:
        @pl.when(tstep == pl.num_programs(2) - 1)
        def _():
            so_ref[...] = st_ref[...]


def rwkv_seq(z3, shift, P, j, s0, chunk, nchunk=1, npair=1, t_real=None, indep=False):
    bsz, t, _ = z3.shape
    tb = chunk * nchunk
    sb = None
    if indep:
        sb = nchunk
        z3 = z3.reshape(bsz // nchunk, tb, z3.shape[2])
        bsz, t = bsz // nchunk, tb
    wl = npair * LANES
    groups = D_RWKV // wl
    zoff = (D_Q + 2 * D_KV) // LANES
    nb = D_RWKV // LANES
    wa_blk = 3 * nb
    gd_blk = 3 * nb + 1
    zc = lambda off: pl.BlockSpec((None, tb, wl), lambda b, p, s: (b, s, (zoff + off) // npair + p))
    zw = lambda blk: pl.BlockSpec((None, tb, LANES), lambda b, p, s: (b, s, zoff + blk))
    sc = lambda off: pl.BlockSpec((sb, 1, wl), lambda b, p, s: (b, 0, off // npair + p))
    sw = lambda blk: pl.BlockSpec((sb, 1, LANES), lambda b, p, s: (b, 0, blk))
    mc = lambda off: pl.BlockSpec((1, wl), lambda b, p, s: (0, off // npair + p))
    mw = lambda blk: pl.BlockSpec((1, LANES), lambda b, p, s: (0, blk))
    vec = pl.BlockSpec((1, wl), lambda b, p, s: (0, p))
    lora = pl.BlockSpec((RWKV_RANK, wl), lambda b, p, s: (0, p))
    st = pl.BlockSpec((sb, 2 * npair, RWKV_N, RWKV_N), lambda b, p, s: (b, p, 0, 0))
    mu = jnp.pad(P['rwkv_mu'][j], (0, D_SHIFT_PAD - D_SHIFT)).reshape(1, D_SHIFT_PAD)
    row = lambda x: x.reshape(1, D_RWKV)
    body = functools.partial(_rwkv_body, chunk=chunk, nchunk=nchunk, npair=npair, t_real=t_real, indep=indep)
    nseq = bsz * (sb or 1)
    o, so = pl.pallas_call(
        body, grid=(bsz, groups, t // tb),
        in_specs=[zc(0), zc(nb), zc(2 * nb), zw(wa_blk), zw(gd_blk),
                  sc(0), sc(nb), sc(2 * nb), sw(wa_blk), sw(gd_blk),
                  mc(0), mc(nb), mc(2 * nb), mw(wa_blk), mw(gd_blk),
                  vec, lora, vec, lora, lora, vec, vec, vec, vec, vec, st],
        out_specs=[pl.BlockSpec((None, tb, wl), lambda b, p, s: (b, s, p)), st],
        out_shape=[jax.ShapeDtypeStruct((bsz, t, D_RWKV), BF16),
                   jax.ShapeDtypeStruct((nseq, RWKV_HEADS, RWKV_N, RWKV_N), F32)],
        scratch_shapes=[pltpu.VMEM((2 * npair, RWKV_N, RWKV_N), F32), pltpu.VMEM((8, wl), F32),
                        pltpu.VMEM((8, LANES), F32)],
        compiler_params=_cp(("parallel", "parallel", "arbitrary")), name="rwkv_seq",
    )(z3, z3, z3, z3, z3, shift, shift, shift, shift, shift, mu, mu, mu, mu, mu,
      row(P['rwkv_w0'][j]), P['rwkv_w2'][j], row(P['rwkv_a0'][j]), P['rwkv_a2'][j], P['rwkv_g2'][j],
      row(P['rwkv_kk'][j]), row(P['rwkv_ka'][j]), row(P['rwkv_rk'][j]), row(P['rwkv_lnx_g'][j]),
      row(P['rwkv_lnx_b'][j]), s0)
    return o.reshape(nseq, chunk if indep else t, D_RWKV), so


def _pad_time(z2, tp):
    return jnp.pad(z2[:, None, :], ((0, 0), (0, tp - 1), (0, 0)))


def _mix_layer(l, h, x, grp, pos0, W, P, st, single):
    n = x.shape[0]
    t = grp.t
    bsz = n // t
    tm = grp.tm
    j = l // 2
    step_pad = 8
    new = {}
    if l % 2 == 0:
        z = matmul(h, W['w_in_even'][j], min(n, 2 * tm), D_IN_EVEN // 4)
        if single:
            a_out, new['conv'] = conv_step(z, P['conv_w'][j], st['conv'][j])
            b3, new['hgrn'] = hgrn_seq(_pad_time(z, step_pad), P['hgrn_lb'], P['hgrn_norm'][j], st['hgrn'][j], j,
                                       chunk=step_pad, nchunk=4, nhead=4, t_real=1, indep=True)
            b_out = b3[:, 0]
        else:
            z3 = z.reshape(bsz, t, D_IN_EVEN)
            a3, new['conv'] = conv_seq(z3, P['conv_w'][j], st['conv'][j])
            b3, new['hgrn'] = hgrn_seq(z3, P['hgrn_lb'], P['hgrn_norm'][j], st['hgrn'][j], j, chunk=64, nchunk=8)
            a_out, b_out = a3.reshape(n, D_CONV), b3.reshape(n, D_HV)
        wa, wb = W['w_out_even'][j][:D_CONV], W['w_out_even'][j][D_CONV:]
    else:
        z = matmul(h, W['w_in_odd'][j], tm, D_IN_ODD_PAD // 2)
        shift_in = jnp.pad(st['shift'][j], ((0, 0), (0, D_SHIFT_PAD - D_SHIFT)))[:, None, :]
        kc = st['k'][j].reshape(bsz, WINDOW, D_KV)
        vc = st['v'][j].reshape(bsz, WINDOW, D_KV)
        if single:
            cos, sin = rope_tables(pos0 + jnp.arange(step_pad, dtype=I32))
            a3, nk, nv = attn_seq(_pad_time(z[:, :D_Q + 2 * D_KV], step_pad), P['attn_sinks'][j], cos, sin,
                                  kc, vc, prev_valid=True, t_real=1, tq=step_pad)
            b3, new['rwkv'] = rwkv_seq(_pad_time(z, step_pad), shift_in, P, j, st['rwkv'][j], chunk=step_pad,
                                       nchunk=4, npair=4, t_real=1, indep=True)
            a_out, b_out = a3[:, 0], b3[:, 0]
            new['shift'] = z[:, D_Q + 2 * D_KV:D_IN_ODD]
        else:
            z3 = z.reshape(bsz, t, D_IN_ODD_PAD)
            cos, sin = rope_tables(pos0 + jnp.arange(t, dtype=I32))
            a3, nk, nv = attn_seq(z3, P['attn_sinks'][j], cos, sin, kc, vc, prev_valid=False, t_real=WINDOW)
            b3, new['rwkv'] = rwkv_seq(z3, shift_in, P, j, st['rwkv'][j], chunk=64, nchunk=4, npair=2)
            a_out, b_out = a3.reshape(n, D_Q), b3.reshape(n, D_RWKV)
            new['shift'] = z3[:, t - 1, D_Q + 2 * D_KV:D_IN_ODD]
        new['k'] = nk.reshape(bsz, WINDOW, ATTN_KV_HEADS, HEAD_DIM)
        new['v'] = nv.reshape(bsz, WINDOW, ATTN_KV_HEADS, HEAD_DIM)
        wa, wb = W['w_out_odd'][j][:D_Q], W['w_out_odd'][j][D_Q:]
    return a_out, b_out, wa, wb, new


def kernel(x_prompt, x_sample, c_prompt, c_sample, state_conv, state_hgrn, cache_swa_k, cache_swa_v,
           state_rwkv, state_shift, ada_w, ada_b, ln_g, ln_b, w_in_even, w_out_even, conv_w, hgrn_lb,
           hgrn_norm, w_in_odd, w_out_odd, attn_sinks, rwkv_mu, rwkv_w0, rwkv_w2, rwkv_a0, rwkv_a2,
           rwkv_g2, rwkv_kk, rwkv_ka, rwkv_rk, rwkv_lnx_g, rwkv_lnx_b, moe_w_grp, moe_b_grp, moe_w_exp,
           moe_b_exp, moe_w1, moe_w3, moe_w2):
    P = dict(ln_g=ln_g, ln_b=ln_b, conv_w=conv_w, hgrn_lb=hgrn_lb, hgrn_norm=hgrn_norm, attn_sinks=attn_sinks,
             rwkv_mu=rwkv_mu, rwkv_w0=rwkv_w0, rwkv_w2=rwkv_w2, rwkv_a0=rwkv_a0, rwkv_a2=rwkv_a2,
             rwkv_g2=rwkv_g2, rwkv_kk=rwkv_kk, rwkv_ka=rwkv_ka, rwkv_rk=rwkv_rk.reshape(N_ODD, D_RWKV),
             rwkv_lnx_g=rwkv_lnx_g, rwkv_lnx_b=rwkv_lnx_b)
    bp, tp, d = x_prompt.shape
    bs, ts, _ = x_sample.shape
    n_p, n_s = bp * tp, bs * ts
    router_w = jnp.zeros((DEPTH, d, 2 * LANES), F32)
    router_w = router_w.at[:, :, :N_GROUPS].set(moe_w_grp).at[:, :, LANES:LANES + N_EXPERTS].set(moe_w_exp)
    router_b = jnp.zeros((DEPTH, 1, 2 * LANES), F32)
    router_b = router_b.at[:, 0, :N_GROUPS].set(moe_b_grp).at[:, 0, LANES:LANES + N_EXPERTS].set(moe_b_exp)
    W = dict(w_in_even=w_in_even.astype(BF16), w_out_even=w_out_even.astype(BF16),
             w_in_odd=jnp.pad(w_in_odd.astype(BF16), ((0, 0), (0, 0), (0, D_IN_ODD_PAD - D_IN_ODD))),
             w_out_odd=w_out_odd.astype(BF16))

    mod = ada_mod(jnp.concatenate([c_prompt, c_sample], axis=0), ada_w, ada_b)
    mod = mod.reshape(DEPTH, bp + bs, 6, d).transpose(0, 2, 1, 3)
    grp_p = _Group(n_p, tp, 512, mod[:, :, :bp, None, :], per_row=False)
    grp_s = _Group(n_s, ts, n_s, mod[:, :, bp:], per_row=True)

    zeros = lambda *s: jnp.zeros(s, F32)
    st_p = dict(conv=zeros(N_EVEN, bp, CONV_WIDTH - 1, D_CONV), hgrn=zeros(N_EVEN, bp, HGRN_HEADS, HGRN_DK, HGRN_DV),
                k=zeros(N_ODD, bp, WINDOW, ATTN_KV_HEADS, HEAD_DIM), v=zeros(N_ODD, bp, WINDOW, ATTN_KV_HEADS, HEAD_DIM),
                rwkv=zeros(N_ODD, bp, RWKV_HEADS, RWKV_N, RWKV_N), shift=zeros(N_ODD, bp, D_SHIFT))
    st_s = dict(conv=state_conv, hgrn=state_hgrn, k=cache_swa_k, v=cache_swa_v, rwkv=state_rwkv, shift=state_shift)
    new_p = {k: [] for k in st_p}
    new_s = {k: [] for k in st_s}

    n_all = n_p + n_s
    t_all = jnp.arange(n_all, dtype=I32)
    row_of = lambda k: jnp.where(t_all < n_p, k * n_p + t_all, 2 * n_p + k * n_s + (t_all - n_p))
    dst_of_entry = jnp.stack([row_of(0), row_of(1)], axis=1).reshape(-1)

    x_p, x_s = x_prompt.reshape(n_p, d), x_sample.reshape(n_s, d)
    h_all = zeros(n_all, d)
    h_p, h_s = modulate(x_p, grp_p, 0, 1, 0), modulate(x_s, grp_s, 0, 1, 0)
    for l in range(DEPTH):
        a_p, b_p, wa, wb, np_l = _mix_layer(l, h_p, x_p, grp_p, 0, W, P, st_p, single=False)
        a_s, b_s, _, _, ns_l = _mix_layer(l, h_s, x_s, grp_s, PAST_LEN, W, P, st_s, single=True)
        for k, v in np_l.items():
            new_p[k].append(v)
        for k, v in ns_l.items():
            new_s[k].append(v)
        x_p, h_all = out_proj_ln(a_p, b_p, wa, wb, x_p, grp_p, l, ln_g[l, 0], ln_b[l, 0], n_all, 0, h_all)
        x_s, h_all = out_proj_ln(a_s, b_s, wa, wb, x_s, grp_s, l, ln_g[l, 0], ln_b[l, 0], n_all, n_p, h_all)
        eid_p, gate_p, rank_p, cnt = router(h_all, 0, n_p, router_w[l], router_b[l], 256, zeros(1, LANES))
        eid_s, gate_s, rank_s, cnt = router(h_all, n_p, n_s, router_w[l], router_b[l], n_s, cnt)
        tok, dst, be, n_used, n_blocks = moe_plan(jnp.concatenate([eid_p, eid_s]), jnp.concatenate([rank_p, rank_s]),
                                                  cnt, dst_of_entry, 2 * n_all)
        ys = moe_ffn(h_all, tok, dst, be, n_used, n_blocks, moe_w1, moe_w3, moe_w2, l)
        last = l + 1 == DEPTH
        x_p, h_p = moe_combine_ln(ys, 0, n_p, gate_p, x_p, grp_p, l, ln_g[l, 1], ln_b[l, 1], has_next=not last)
        x_s, h_s = moe_combine_ln(ys, 2 * n_p, 2 * n_p + n_s, gate_s, x_s, grp_s, l, ln_g[l, 1], ln_b[l, 1],
                                  has_next=not last)
    order = ('conv', 'hgrn', 'k', 'v', 'rwkv', 'shift')
    return ((x_p.reshape(bp, tp, d), x_s.reshape(bs, ts, d))
            + tuple(jnp.stack(new_p[k]) for k in order) + tuple(jnp.stack(new_s[k]) for k in order))
```

```python
import functools
import math

import jax
import jax.numpy as jnp
from jax import lax
from jax.experimental import pallas as pl
from jax.experimental.pallas import tpu as pltpu

F32 = jnp.float32
BF16 = jnp.bfloat16
I32 = jnp.int32

D_MODEL = 2048
DEPTH = 4
PAST_LEN = 16384
N_EVEN = (DEPTH + 1) // 2
N_ODD = DEPTH // 2
D_CONV = 1024
CONV_WIDTH = 3
HGRN_HEADS = 8
HGRN_DK = 128
HGRN_DV = 128
D_HK = HGRN_HEADS * HGRN_DK
D_HV = HGRN_HEADS * HGRN_DV
ATTN_HEADS = 16
ATTN_KV_HEADS = 4
HEAD_DIM = 64
WINDOW = 128
ROPE_THETA = 10000.0
D_Q = ATTN_HEADS * HEAD_DIM
D_KV = ATTN_KV_HEADS * HEAD_DIM
RWKV_HEADS = 16
RWKV_N = 64
D_RWKV = RWKV_HEADS * RWKV_N
RWKV_RANK = 64
RWKV_GN_EPS = 64e-5
D_SHIFT = 3 * D_RWKV + 3 * RWKV_RANK
D_IN_EVEN = 3 * D_CONV + 2 * D_HK + 2 * D_HV
D_IN_ODD = D_Q + 2 * D_KV + D_SHIFT
N_GROUPS = 4
EXPERTS_PER_GROUP = 8
N_EXPERTS = N_GROUPS * EXPERTS_PER_GROUP
D_EXPERT = 512
ALPHA = (2 * DEPTH) ** 0.25
LN_EPS = 1e-5
RMS_EPS = 1e-6

LANES = 128
HGRN_SUB = 16
MOE_ROWS = 128
VMEM_LIMIT = 48 * 1024 * 1024
MOE_VMEM_LIMIT = 56 * 1024 * 1024

D_IN_ODD_PAD = -(-D_IN_ODD // LANES) * LANES
D_SHIFT_PAD = D_IN_ODD_PAD - (D_Q + 2 * D_KV)

NN = (((1,), (0,)), ((), ()))
NT = (((1,), (1,)), ((), ()))
TN = (((0,), (0,)), ((), ()))


def _cp(sem, vmem=VMEM_LIMIT):
    return pltpu.CompilerParams(dimension_semantics=sem, vmem_limit_bytes=vmem)


def _dot(a, b, dims=NN):
    return lax.dot_general(a.astype(BF16), b.astype(BF16), dims, preferred_element_type=F32)


def _split2(x):
    hi = x.astype(BF16)
    lo = (x - hi.astype(F32)).astype(BF16)
    return hi, lo


def _dot3(a, b, dims=NN):
    ah, al = _split2(a)
    bh, bl = _split2(b)
    d = lambda x, y: lax.dot_general(x, y, dims, preferred_element_type=F32)
    return d(ah, bh) + (d(ah, bl) + d(al, bh))


def _dot_exact_lhs(a_bf16, b, dims=NN):
    b1 = b.astype(BF16)
    r1 = b - b1.astype(F32)
    b2 = r1.astype(BF16)
    b3 = (r1 - b2.astype(F32)).astype(BF16)
    d = lambda y: lax.dot_general(a_bf16, y, dims, preferred_element_type=F32)
    return d(b1) + (d(b2) + d(b3))


def _dot_exact_rhs(a, b_bf16, dims=NN):
    a1 = a.astype(BF16)
    r1 = a - a1.astype(F32)
    a2 = r1.astype(BF16)
    d = lambda x: lax.dot_general(x, b_bf16, dims, preferred_element_type=F32)
    return d(a1) + d(a2)


def _sigmoid(x):
    return 1.0 / (1.0 + jnp.exp(-x))


def _silu(x):
    return x * _sigmoid(x)


def _softplus(x):
    return jnp.maximum(x, 0.0) + jnp.log(1.0 + jnp.exp(-jnp.abs(x)))


def _iota(shape, axis):
    return lax.broadcasted_iota(I32, shape, axis)


def _layer_norm(u, g, b):
    mu = jnp.mean(u, axis=-1, keepdims=True)
    d = u - mu
    var = jnp.mean(d * d, axis=-1, keepdims=True)
    return d * lax.rsqrt(var + LN_EPS) * g + b


def _ada_body(c_ref, w_ref, b_ref, o_ref):
    o_ref[...] = _dot(_silu(c_ref[...]), w_ref[...]) + b_ref[...]


def ada_mod(c_all, ada_w, ada_b, tn=1024):
    nl, d, n = ada_w.shape
    r = c_all.shape[0]
    return pl.pallas_call(
        _ada_body,
        grid=(nl, n // tn),
        in_specs=[pl.BlockSpec((r, d), lambda l, j: (0, 0)),
                  pl.BlockSpec((None, d, tn), lambda l, j: (l, 0, j)),
                  pl.BlockSpec((None, 1, tn), lambda l, j: (l, 0, j))],
        out_specs=pl.BlockSpec((None, r, tn), lambda l, j: (l, 0, j)),
        out_shape=jax.ShapeDtypeStruct((nl, r, n), F32),
        compiler_params=_cp(("parallel", "parallel")),
        name="ada_mod",
    )(c_all, ada_w, ada_b.reshape(nl, 1, n))


class _Group:
    def __init__(self, n, t, tm, modarr, per_row):
        self.n, self.t, self.tm, self.modarr, self.per_row = n, t, tm, modarr, per_row

    def mod(self, l, c):
        d = self.modarr.shape[-1]
        if self.per_row:
            return pl.BlockSpec((None, None, self.tm, d), lambda i: (l, c, i, 0))
        t, tm = self.t, self.tm
        return pl.BlockSpec((None, None, None, 1, d), lambda i: (l, c, (i * tm) // t, 0, 0))


def _modulate_body(x_ref, sc_ref, sh_ref, h_ref):
    h_ref[...] = (x_ref[...] * (1.0 + sc_ref[...]) + sh_ref[...]).astype(h_ref.dtype)


def modulate(x, grp, l, c_sc, c_sh):
    n, d = x.shape
    tm = grp.tm
    row = pl.BlockSpec((tm, d), lambda i: (i, 0))
    return pl.pallas_call(
        _modulate_body, grid=(n // tm,),
        in_specs=[row, grp.mod(l, c_sc), grp.mod(l, c_sh)],
        out_specs=row, out_shape=jax.ShapeDtypeStruct((n, d), BF16),
        compiler_params=_cp(("parallel",)), name="modulate",
    )(x, grp.modarr, grp.modarr)


def _mm_body(a_ref, w_ref, o_ref):
    o_ref[...] = jnp.dot(a_ref[...], w_ref[...], preferred_element_type=F32)


def matmul(a, w, tm, tn):
    n, k = a.shape
    nn = w.shape[1]
    return pl.pallas_call(
        _mm_body, grid=(n // tm, nn // tn),
        in_specs=[pl.BlockSpec((tm, k), lambda i, j: (i, 0)),
                  pl.BlockSpec((k, tn), lambda i, j: (0, j))],
        out_specs=pl.BlockSpec((tm, tn), lambda i, j: (i, j)),
        out_shape=jax.ShapeDtypeStruct((n, nn), F32),
        compiler_params=_cp(("parallel", "parallel")), name="in_proj",
    )(a, w)


def _outln_body(a_ref, b_ref, wa_ref, wb_ref, x_ref, gt_ref, g_ref, be_ref, sc_ref, sh_ref, *rest):
    xo_ref, ho_ref = rest[-2:]
    y = (jnp.dot(a_ref[...], wa_ref[...], preferred_element_type=F32)
         + jnp.dot(b_ref[...], wb_ref[...], preferred_element_type=F32))
    xn = _layer_norm(ALPHA * x_ref[...] + (1.0 + gt_ref[...]) * y, g_ref[...], be_ref[...])
    xo_ref[...] = xn
    ho_ref[...] = xn * (1.0 + sc_ref[...]) + sh_ref[...]


def out_proj_ln(a, b, wa, wb, x, grp, l, ln_g, ln_b, h_rows, h_row0, h_all=None):
    n, d = x.shape
    ka, kb = a.shape[1], b.shape[1]
    tm = min(grp.tm, 256)
    g2 = _Group(grp.n, grp.t, tm, grp.modarr, grp.per_row)
    row = pl.BlockSpec((tm, d), lambda i: (i, 0))
    vec = pl.BlockSpec((1, d), lambda i: (0, 0))
    in_specs = [pl.BlockSpec((tm, ka), lambda i: (i, 0)), pl.BlockSpec((tm, kb), lambda i: (i, 0)),
                pl.BlockSpec((ka, d), lambda i: (0, 0)), pl.BlockSpec((kb, d), lambda i: (0, 0)),
                row, g2.mod(l, 2), vec, vec, g2.mod(l, 4), g2.mod(l, 3)]
    args = [a, b, wa, wb, x, grp.modarr, ln_g.reshape(1, d), ln_b.reshape(1, d), grp.modarr, grp.modarr]
    aliases = {}
    if h_all is not None:
        in_specs.append(pl.BlockSpec(memory_space=pl.ANY))
        args.append(h_all)
        aliases = {len(args) - 1: 1}
    return pl.pallas_call(
        _outln_body, grid=(n // tm,),
        in_specs=in_specs,
        out_specs=[row, pl.BlockSpec((tm, d), lambda i: (i + h_row0 // tm, 0))],
        out_shape=[jax.ShapeDtypeStruct((n, d), F32), jax.ShapeDtypeStruct((h_rows, d), F32)],
        input_output_aliases=aliases,
        compiler_params=_cp(("parallel",)), name="out_proj_ln",
    )(*args)


def _comb_body(*refs, has_next):
    if has_next:
        y0_ref, y1_ref, gate_ref, x_ref, gt_ref, g_ref, be_ref, sc_ref, sh_ref, xo_ref, ho_ref = refs
    else:
        y0_ref, y1_ref, gate_ref, x_ref, gt_ref, g_ref, be_ref, xo_ref = refs
    gate = gate_ref[...]
    y = y0_ref[...] * gate[:, 0:1] + y1_ref[...] * gate[:, 1:2]
    xn = _layer_norm(ALPHA * x_ref[...] + (1.0 + gt_ref[...]) * y, g_ref[...], be_ref[...])
    xo_ref[...] = xn
    if has_next:
        ho_ref[...] = (xn * (1.0 + sc_ref[...]) + sh_ref[...]).astype(ho_ref.dtype)


def moe_combine_ln(ys, row0, row1, gate, x, grp, l, ln_g, ln_b, has_next):
    n, d = x.shape
    tm = min(grp.tm, 256)
    g2 = _Group(grp.n, grp.t, tm, grp.modarr, grp.per_row)
    nt = n // tm
    row = pl.BlockSpec((tm, d), lambda i: (i, 0))
    vec = pl.BlockSpec((1, d), lambda i: (0, 0))
    in_specs = [pl.BlockSpec((tm, d), lambda i: (i + row0 // tm, 0)),
                pl.BlockSpec((tm, d), lambda i: (i + row1 // tm, 0)), pl.BlockSpec((tm, 2), lambda i: (i, 0)),
                row, g2.mod(l, 5), vec, vec]
    args = [ys, ys, gate, x, grp.modarr, ln_g.reshape(1, d), ln_b.reshape(1, d)]
    out_specs = [row]
    out_shape = [jax.ShapeDtypeStruct((n, d), F32)]
    if has_next:
        in_specs += [g2.mod(l + 1, 1), g2.mod(l + 1, 0)]
        args += [grp.modarr, grp.modarr]
        out_specs.append(row)
        out_shape.append(jax.ShapeDtypeStruct((n, d), BF16))
    res = pl.pallas_call(
        functools.partial(_comb_body, has_next=has_next), grid=(nt,),
        in_specs=in_specs, out_specs=out_specs, out_shape=out_shape,
        compiler_params=_cp(("parallel",)), name="moe_combine_ln",
    )(*args)
    return (res[0], res[1]) if has_next else (res[0], None)


def _router_body(h_ref, w_ref, b_ref, cnt0_ref, eid_ref, gate_ref, rank_ref, cnt_ref, run_ref):
    i = pl.program_id(0)

    @pl.when(i == 0)
    def _():
        run_ref[...] = cnt0_ref[...]

    tm = h_ref.shape[0]
    logits = _dot3(h_ref[...], w_ref[...]) + b_ref[...]
    lg = logits[:, :LANES]
    le = logits[:, LANES:]
    lane = _iota((tm, LANES), 1).astype(F32)
    neg = jnp.float32(-jnp.inf)
    lgm = jnp.where(lane < N_GROUPS, lg, neg)
    mg = jnp.max(lgm, axis=-1, keepdims=True)
    gidx = jnp.min(jnp.where(lgm == mg, lane, float(LANES)), axis=-1, keepdims=True)
    p_g = 1.0 / jnp.sum(jnp.exp(lgm - mg), axis=-1, keepdims=True)
    lo = gidx * EXPERTS_PER_GROUP
    in_grp = (lane >= lo) & (lane < lo + EXPERTS_PER_GROUP)
    lem = jnp.where(in_grp, le, neg)
    v1 = jnp.max(lem, axis=-1, keepdims=True)
    i1 = jnp.min(jnp.where(lem == v1, lane, float(LANES)), axis=-1, keepdims=True)
    lem2 = jnp.where(lane == i1, neg, lem)
    v2 = jnp.max(lem2, axis=-1, keepdims=True)
    i2 = jnp.min(jnp.where(lem2 == v2, lane, float(LANES)), axis=-1, keepdims=True)
    e2 = jnp.exp(v2 - v1)
    g1 = p_g / (1.0 + e2)
    g2 = p_g * e2 / (1.0 + e2)
    oh1 = jnp.where(lane == i1, 1.0, 0.0)
    oh2 = jnp.where(lane == i2, 1.0, 0.0)
    comb = oh1 + oh2
    tri = jnp.where(_iota((tm, tm), 0) > _iota((tm, tm), 1), 1.0, 0.0).astype(BF16)
    before = jnp.dot(tri, comb.astype(BF16), preferred_element_type=F32) + run_ref[...]
    r1 = jnp.sum(before * oh1, axis=-1, keepdims=True)
    r2 = jnp.sum(before * oh2, axis=-1, keepdims=True)
    run_ref[...] = run_ref[...] + jnp.sum(comb, axis=0, keepdims=True)
    eid_ref[:, 0:1] = i1.astype(I32)
    eid_ref[:, 1:2] = i2.astype(I32)
    gate_ref[:, 0:1] = g1
    gate_ref[:, 1:2] = g2
    rank_ref[:, 0:1] = r1.astype(I32)
    rank_ref[:, 1:2] = r2.astype(I32)
    cnt_ref[...] = run_ref[...]


def router(h, row0, n, wr, br, tm, cnt0):
    d = h.shape[1]
    two = pl.BlockSpec((tm, 2), lambda i: (i, 0))
    one = pl.BlockSpec((1, LANES), lambda i: (0, 0))
    return pl.pallas_call(
        _router_body, grid=(n // tm,),
        in_specs=[pl.BlockSpec((tm, d), lambda i: (i + row0 // tm, 0)),
                  pl.BlockSpec((d, 2 * LANES), lambda i: (0, 0)),
                  pl.BlockSpec((1, 2 * LANES), lambda i: (0, 0)), one],
        out_specs=[two, two, two, one],
        out_shape=[jax.ShapeDtypeStruct((n, 2), I32), jax.ShapeDtypeStruct((n, 2), F32),
                   jax.ShapeDtypeStruct((n, 2), I32), jax.ShapeDtypeStruct((1, LANES), F32)],
        scratch_shapes=[pltpu.VMEM((1, LANES), F32)],
        compiler_params=_cp(("arbitrary",)), name="router",
    )(h, wr, br, cnt0)


def moe_plan(eid, rank, cnt, dst_of_entry, spare_row0):
    nk = 2 * eid.shape[0]
    n_blocks = -(-nk // MOE_ROWS) + N_EXPERTS
    n_rows = n_blocks * MOE_ROWS
    counts = cnt[0, :N_EXPERTS].astype(I32)
    padded = (counts + MOE_ROWS - 1) // MOE_ROWS * MOE_ROWS
    pad_end = jnp.cumsum(padded)
    pad_start = pad_end - padded
    dest = (pad_start[eid] + rank).reshape(-1)
    src = jnp.full((n_rows + MOE_ROWS,), -1, I32).at[dest].set(jnp.arange(nk, dtype=I32), unique_indices=True,
                                                               mode='promise_in_bounds')
    valid = src >= 0
    pos = jnp.arange(n_rows + MOE_ROWS, dtype=I32)
    safe = jnp.maximum(src, 0)
    tok = jnp.where(valid, safe >> 1, 0)
    dst = jnp.where(valid, dst_of_entry[safe], 0)
    live = jnp.sum(valid.reshape(n_blocks + 1, MOE_ROWS).astype(I32), axis=1)
    live = jnp.concatenate([jnp.zeros((2,), I32), live])
    n_used = pad_end[-1] // MOE_ROWS
    blk = jnp.arange(n_blocks, dtype=I32)
    be = jnp.minimum(jnp.sum((blk * MOE_ROWS)[:, None] >= pad_end[None, :], axis=-1), N_EXPERTS - 1).astype(I32)
    be = jnp.where(blk < n_used, be, be[jnp.maximum(n_used - 1, 0)])
    return tok, dst, be, jnp.concatenate([n_used.reshape(1).astype(I32), live]), n_blocks


def _moe_body(tok_ref, dst_ref, be_ref, nu_ref, h_hbm, w1_ref, w3_ref, w2_ref, ys_hbm,
              xbuf, ybuf, w1b, w3b, w2b, gsem, ssem):
    b = pl.program_id(0)
    n_used = nu_ref[0]
    slot = b % 2
    live_of = lambda blk: nu_ref[blk + 3]

    def runs(blk, fn):
        n_live = live_of(blk)
        base = jnp.int32(0)
        p = MOE_ROWS
        while p >= 1:
            hit = (n_live & p) != 0

            @pl.when(hit)
            def _(base=base, p=p):
                fn(base, p)
            base = base + jnp.where(hit, p, 0)
            p //= 2

    def gather_rows(blk, sl):
        def go(base, p):
            for i in range(p):
                pltpu.make_async_copy(h_hbm.at[pl.ds(tok_ref[blk * MOE_ROWS + base + i], 1)],
                                      xbuf.at[sl, pl.ds(base + i, 1)], gsem.at[sl]).start()
        runs(blk, go)

    def wait_gather(blk, sl):
        runs(blk, lambda base, p: pltpu.make_async_copy(xbuf.at[sl, pl.ds(0, p)], xbuf.at[sl, pl.ds(0, p)],
                                                        gsem.at[sl]).wait())

    def scatter_rows(blk, sl):
        def go(base, p):
            for i in range(p):
                pltpu.make_async_copy(ybuf.at[sl, pl.ds(base + i, 1)],
                                      ys_hbm.at[pl.ds(dst_ref[blk * MOE_ROWS + base + i], 1)], ssem.at[sl]).start()
        runs(blk, go)

    def wait_scatter(blk, sl):
        runs(blk, lambda base, p: pltpu.make_async_copy(ybuf.at[sl, pl.ds(0, p)], ybuf.at[sl, pl.ds(0, p)],
                                                        ssem.at[sl]).wait())

    @pl.when(b == 0)
    def _():
        xbuf[...] = jnp.zeros_like(xbuf)
        gather_rows(0, 0)

    @pl.when(b < n_used)
    def _():
        @pl.when(jnp.logical_or(b == 0, be_ref[b] != be_ref[jnp.maximum(b - 1, 0)]))
        def _():
            w1b[...] = w1_ref[...].astype(BF16)
            w3b[...] = w3_ref[...].astype(BF16)
            w2b[...] = w2_ref[...].astype(BF16)

        wait_gather(b, slot)
        x = xbuf[slot].astype(BF16)
        h1 = jnp.dot(x, w1b[...], preferred_element_type=F32)
        h3 = jnp.dot(x, w3b[...], preferred_element_type=F32)
        mid = (_silu(h1) * h3).astype(BF16)
        y = jnp.dot(mid, w2b[...], preferred_element_type=F32)
        gather_rows(b + 1, 1 - slot)
        scatter_rows(b - 1, 1 - slot)
        wait_scatter(b - 2, slot)
        ybuf[slot] = y

        @pl.when(b == n_used - 1)
        def _():
            scatter_rows(b, slot)
            wait_scatter(b - 1, 1 - slot)
            wait_scatter(b, slot)


def moe_ffn(h, tok, dst, be, n_used, n_blocks, w1, w3, w2, layer):
    n, d = h.shape
    de = w1.shape[-1]
    wspec_in = pl.BlockSpec((None, None, d, de), lambda b, tok, dst, be, nu: (layer, be[b], 0, 0))
    wspec_out = pl.BlockSpec((None, None, de, d), lambda b, tok, dst, be, nu: (layer, be[b], 0, 0))
    return pl.pallas_call(
        _moe_body,
        grid_spec=pltpu.PrefetchScalarGridSpec(
            num_scalar_prefetch=4, grid=(n_blocks,),
            in_specs=[pl.BlockSpec(memory_space=pl.ANY), wspec_in, wspec_in, wspec_out],
            out_specs=pl.BlockSpec(memory_space=pl.ANY),
            scratch_shapes=[pltpu.VMEM((2, MOE_ROWS, d), F32), pltpu.VMEM((2, MOE_ROWS, d), F32),
                            pltpu.VMEM((d, de), BF16), pltpu.VMEM((d, de), BF16), pltpu.VMEM((de, d), BF16),
                            pltpu.SemaphoreType.DMA((2,)), pltpu.SemaphoreType.DMA((2,))]),
        out_shape=jax.ShapeDtypeStruct((2 * n, d), F32),
        compiler_params=_cp(("arbitrary",), MOE_VMEM_LIMIT), name="moe_ffn",
    )(tok, dst, be, n_used, h, w1, w3, w2)


def _conv_body(va_ref, bg_ref, cg_ref, w_ref, buf_ref, o_ref, nb_ref):
    u = cg_ref[...] * va_ref[...]
    t = u.shape[0]
    row = _iota(u.shape, 0)
    b0 = buf_ref[0:1, :]
    b1 = buf_ref[1:2, :]
    u1 = jnp.where(row == 0, b1, pltpu.roll(u, 1, axis=0))
    u2 = jnp.where(row == 0, b0, jnp.where(row == 1, b1, pltpu.roll(u, 2, axis=0)))
    w = w_ref[...]
    y = w[0:1] * u2 + w[1:2] * u1 + w[2:3] * u
    o_ref[...] = (bg_ref[...] * y).astype(o_ref.dtype)
    nb_ref[...] = u[t - 2:t, :]


def conv_seq(z3, conv_w, buf, tc=256):
    bsz, t, _ = z3.shape
    nc = D_CONV // tc
    col = lambda off: pl.BlockSpec((None, t, tc), lambda b, j: (b, 0, off + j))
    return pl.pallas_call(
        _conv_body, grid=(bsz, nc),
        in_specs=[col(0), col(nc), col(2 * nc),
                  pl.BlockSpec((CONV_WIDTH, tc), lambda b, j: (0, j)),
                  pl.BlockSpec((None, 2, tc), lambda b, j: (b, 0, j))],
        out_specs=[pl.BlockSpec((None, t, tc), lambda b, j: (b, 0, j)),
                   pl.BlockSpec((None, 2, tc), lambda b, j: (b, 0, j))],
        out_shape=[jax.ShapeDtypeStruct((bsz, t, D_CONV), BF16), jax.ShapeDtypeStruct((bsz, 2, D_CONV), F32)],
        compiler_params=_cp(("parallel", "parallel")), name="conv_seq",
    )(z3, z3, z3, conv_w, buf)


def _conv_step_body(va_ref, bg_ref, cg_ref, w_ref, b0_ref, b1_ref, o_ref, u_ref):
    u = cg_ref[...] * va_ref[...]
    w = w_ref[...]
    y = w[0:1] * b0_ref[...] + w[1:2] * b1_ref[...] + w[2:3] * u
    o_ref[...] = (bg_ref[...] * y).astype(o_ref.dtype)
    u_ref[...] = u


def conv_step(z2, conv_w, buf):
    bsz = z2.shape[0]
    col = lambda j: pl.BlockSpec((bsz, D_CONV), lambda i: (0, j))
    full = pl.BlockSpec((bsz, D_CONV), lambda i: (0, 0))
    a, u = pl.pallas_call(
        _conv_step_body, grid=(1,),
        in_specs=[col(0), col(1), col(2), pl.BlockSpec((CONV_WIDTH, D_CONV), lambda i: (0, 0)), full, full],
        out_specs=[full, full],
        out_shape=[jax.ShapeDtypeStruct((bsz, D_CONV), BF16), jax.ShapeDtypeStruct((bsz, D_CONV), F32)],
        compiler_params=_cp(("arbitrary",)), name="conv_step",
    )(z2, z2, z2, conv_w, buf[:, 0], buf[:, 1])
    return a, jnp.stack([buf[:, 1], u], axis=1)


def _hgrn_body(q_ref, f_ref, i_ref, g_ref, lb_ref, nw_ref, s0_ref, o_ref, so_ref, st_ref,
               *, chunk, sub, nchunk, nhead, t_real, layer, indep):
    tstep = pl.program_id(2)
    tb = chunk * nchunk
    dk = HGRN_DK

    if not indep:
        @pl.when(tstep == 0)
        def _():
            for hd in range(nhead):
                st_ref[hd] = s0_ref[hd].T

    lbl = lb_ref[...]
    e = jnp.exp(lbl - jnp.max(lbl, axis=0, keepdims=True))
    sm = e / jnp.sum(e, axis=0, keepdims=True)
    lb = jnp.zeros((1, lbl.shape[1]), F32)
    for r in range(1, layer + 1):
        lb = lb + sm[r:r + 1]
    tri = jnp.where(_iota((chunk, chunk), 0) >= _iota((chunk, chunk), 1), 1.0, 0.0).astype(BF16)
    nw = nw_ref[...]
    neg = jnp.float32(-jnp.inf)

    fr = f_ref[...]
    v = i_ref[...]
    logf = jnp.log(lb + (1.0 - lb) * _sigmoid(fr))
    kin = (1.0 - lb) * _sigmoid(-fr)
    if t_real is not None:
        row = _iota(fr.shape, 0)
        live = ((row & (chunk - 1)) if indep else (tstep * tb + row)) < t_real
        logf = jnp.where(live, logf, 0.0)
        kin = jnp.where(live, kin, 0.0)
    qs = _silu(q_ref[...])
    causal = _iota((sub, sub, 1), 1) <= _iota((sub, sub, 1), 0)
    cuts = [(slice(c * chunk, (c + 1) * chunk), slice(hd * dk, (hd + 1) * dk))
            for c in range(nchunk) for hd in range(nhead)]
    bb_all = [_dot_exact_lhs(tri, logf[c * chunk:(c + 1) * chunk]) for c in range(nchunk)]
    bb_l = [bb_all[c][:, hd * dk:(hd + 1) * dk] for c in range(nchunk) for hd in range(nhead)]
    qs_l = [qs[rows, ls] for rows, ls in cuts]
    kin_l = [kin[rows, ls] for rows, ls in cuts]
    v_l = [v[rows, ls] for rows, ls in cuts]
    bl_l = [bb[chunk - 1:chunk] for bb in bb_l]
    kv_l = [_dot(vc, kc * jnp.exp(bl - bb), TN) for vc, kc, bl, bb in zip(v_l, kin_l, bl_l, bb_l)]
    intra_l = []
    for qc, kc, vc, bb in zip(qs_l, kin_l, v_l, bb_l):
        parts = []
        for blk in range(chunk // sub):
            lo = blk * sub
            qi, bi, ki, vi = qc[lo:lo + sub], bb[lo:lo + sub], kc[lo:lo + sub], vc[lo:lo + sub]
            d = bi[:, None, :] - bi[None, :, :]
            sc = jnp.sum(qi[:, None, :] * ki[None, :, :] * jnp.exp(jnp.where(causal, d, neg)), axis=-1)
            oi = _dot(sc, vi)
            if blk > 0:
                anchor = bb[lo - 1:lo]
                qt = qi * jnp.exp(bi - anchor)
                kt = kc[:lo] * jnp.exp(anchor - bb[:lo])
                oi = oi + _dot(_dot(qt, kt, NT), vc[:lo])
            parts.append(oi)
        intra_l.append(parts[0] if len(parts) == 1 else jnp.concatenate(parts, axis=0))
    if indep:
        st_l = [s0_ref[c, hd].T for c in range(nchunk) for hd in range(nhead)]
        for idx, (st, bl, kv) in enumerate(zip(st_l, bl_l, kv_l)):
            so_ref[idx // nhead, idx % nhead] = (st * jnp.exp(bl) + kv).T
    else:
        st_l = []
        cur = [st_ref[hd] for hd in range(nhead)]
        for c in range(nchunk):
            for hd in range(nhead):
                idx = c * nhead + hd
                st_l.append(cur[hd])
                cur[hd] = cur[hd] * jnp.exp(bl_l[idx]) + kv_l[idx]
        for hd in range(nhead):
            st_ref[hd] = cur[hd]
    o_l = [_dot(qc * jnp.exp(bb), sc, NT) + oi for qc, bb, sc, oi in zip(qs_l, bb_l, st_l, intra_l)]
    o_l = [o * lax.rsqrt(jnp.mean(o * o, axis=-1, keepdims=True) + RMS_EPS) * nw for o in o_l]
    rows_l = [o_l[c * nhead] if nhead == 1 else jnp.concatenate(o_l[c * nhead:(c + 1) * nhead], axis=1)
              for c in range(nchunk)]
    o = rows_l[0] if nchunk == 1 else jnp.concatenate(rows_l, axis=0)
    o_ref[...] = (o * _silu(g_ref[...])).astype(o_ref.dtype)

    if not indep:
        @pl.when(tstep == pl.num_programs(2) - 1)
        def _():
            for hd in range(nhead):
                so_ref[hd] = st_ref[hd].T


def hgrn_seq(z3, hgrn_lb, norm_w, s0, layer, chunk, nchunk, nhead=1, t_real=None, indep=False):
    bsz, t, _ = z3.shape
    tb = chunk * nchunk
    sb = None
    if indep:
        sb = nchunk
        z3 = z3.reshape(bsz // nchunk, tb, z3.shape[2])
        bsz, t = bsz // nchunk, tb
    wl = nhead * LANES
    groups = HGRN_HEADS // nhead
    qoff = 3 * D_CONV // wl
    col = lambda k: pl.BlockSpec((None, tb, wl), lambda b, h, s: (b, s, qoff + k * groups + h))
    st = pl.BlockSpec((sb, nhead, HGRN_DK, HGRN_DV), lambda b, h, s: (b, h, 0, 0))
    body = functools.partial(_hgrn_body, chunk=chunk, sub=min(HGRN_SUB, chunk), nchunk=nchunk, nhead=nhead, t_real=t_real,
                             layer=layer, indep=indep)
    nseq = bsz * (sb or 1)
    o, so = pl.pallas_call(
        body, grid=(bsz, groups, t // tb),
        in_specs=[col(0), col(1), col(2), col(3),
                  pl.BlockSpec((N_EVEN, wl), lambda b, h, s: (0, h)),
                  pl.BlockSpec((1, HGRN_DV), lambda b, h, s: (0, 0)), st],
        out_specs=[pl.BlockSpec((None, tb, wl), lambda b, h, s: (b, s, h)), st],
        out_shape=[jax.ShapeDtypeStruct((bsz, t, D_HV), BF16),
                   jax.ShapeDtypeStruct((nseq, HGRN_HEADS, HGRN_DK, HGRN_DV), F32)],
        scratch_shapes=[pltpu.VMEM((nhead, HGRN_DV, HGRN_DK), F32)],
        compiler_params=_cp(("parallel", "parallel", "arbitrary")), name="hgrn_seq",
    )(z3, z3, z3, z3, hgrn_lb, norm_w.reshape(1, HGRN_DV), s0)
    return o.reshape(nseq, chunk if indep else t, D_HV), so


def _attn_body(sink_ref, zq_ref, zkv_ref, cos_ref, sin_ref, ck_ref, cv_ref, o_ref, nk_ref, nv_ref, kp_ref, vp_ref,
               *, prev_valid, t_real, tq):
    i = pl.program_id(1)
    w = WINDOW

    @pl.when(i == 0)
    def _():
        kp_ref[...] = ck_ref[...]
        vp_ref[...] = cv_ref[...]

    cos = cos_ref[...]
    sin = sin_ref[...]

    def rope(x):
        width = x.shape[1]
        reps = width // LANES
        first = (_iota(x.shape, 1) & (HEAD_DIM - 1)) < (HEAD_DIM // 2)
        rot = jnp.where(first, pltpu.roll(x, width - HEAD_DIM // 2, axis=1), pltpu.roll(x, HEAD_DIM // 2, axis=1))
        return x * jnp.tile(cos, (1, reps)) + rot * jnp.tile(sin, (1, reps))

    kv = zkv_ref[...]
    qr = rope(zq_ref[...]) * (HEAD_DIM ** -0.5)
    kr = rope(kv[:, :D_KV])
    v = kv[:, D_KV:]
    kprev = kp_ref[...]
    vprev = vp_ref[...]

    grp = ATTN_HEADS // ATTN_KV_HEADS
    rows = _iota((grp * tq, w + tq), 0)
    cols = _iota((grp * tq, w + tq), 1)
    delta = (rows & (tq - 1)) + w - cols
    valid = (delta >= 0) & (delta <= w)
    if not prev_valid:
        valid = valid & (cols >= jnp.where(i > 0, 0, w))
    head_of_row = _iota((grp * tq, 1), 0) >> int(math.log2(tq))
    neg = jnp.float32(-jnp.inf)
    outs = []
    for g in range(ATTN_KV_HEADS):
        ls = slice(g * HEAD_DIM, (g + 1) * HEAD_DIM)
        kg = jnp.concatenate([kprev[:, ls], kr[:, ls]], axis=0)
        vg = jnp.concatenate([vprev[:, ls], v[:, ls]], axis=0)
        qg = jnp.concatenate([qr[:, (grp * g + hh) * HEAD_DIM:(grp * g + hh + 1) * HEAD_DIM] for hh in range(grp)],
                             axis=0)
        s = jnp.where(valid, _dot(qg, kg, NT), neg)
        sink = jnp.zeros((grp * tq, 1), F32)
        for hh in range(grp):
            sink = jnp.where(head_of_row == hh, sink_ref[grp * g + hh], sink)
        m = jnp.maximum(jnp.max(s, axis=-1, keepdims=True), sink)
        p = jnp.exp(s - m)
        p = p / (jnp.sum(p, axis=-1, keepdims=True) + jnp.exp(sink - m))
        og = _dot(p, vg)
        outs += [og[hh * tq:(hh + 1) * tq] for hh in range(grp)]
    o_ref[...] = jnp.concatenate(outs, axis=1).astype(o_ref.dtype)
    if t_real == w:
        kp_ref[...] = kr
        vp_ref[...] = v

    @pl.when(i == pl.num_programs(1) - 1)
    def _():
        if t_real == w:
            nk_ref[...] = kr
            nv_ref[...] = v
        else:
            last = _iota(kprev.shape, 0) == w - 1
            nk_ref[...] = jnp.where(last, kr[0:1], pltpu.roll(kprev, w - 1, axis=0))
            nv_ref[...] = jnp.where(last, v[0:1], pltpu.roll(vprev, w - 1, axis=0))


def attn_seq(z3, sinks, cos, sin, cache_k, cache_v, prev_valid, t_real, tq=WINDOW):
    bsz, t, _ = z3.shape
    nb = t // tq
    assert tq == WINDOW or nb == 1
    cache = pl.BlockSpec((None, WINDOW, D_KV), lambda b, i: (b, 0, 0))
    tab = pl.BlockSpec((tq, LANES), lambda b, i: (i, 0))
    body = functools.partial(_attn_body, prev_valid=prev_valid, t_real=t_real, tq=tq)
    return pl.pallas_call(
        body, grid=(bsz, nb),
        in_specs=[pl.BlockSpec(memory_space=pltpu.SMEM),
                  pl.BlockSpec((None, tq, D_Q), lambda b, i: (b, i, 0)),
                  pl.BlockSpec((None, tq, 2 * D_KV), lambda b, i: (b, i, D_Q // (2 * D_KV))),
                  tab, tab, cache, cache],
        out_specs=[pl.BlockSpec((None, tq, D_Q), lambda b, i: (b, i, 0)), cache, cache],
        out_shape=[jax.ShapeDtypeStruct((bsz, t, D_Q), BF16),
                   jax.ShapeDtypeStruct((bsz, WINDOW, D_KV), F32), jax.ShapeDtypeStruct((bsz, WINDOW, D_KV), F32)],
        scratch_shapes=[pltpu.VMEM((WINDOW, D_KV), F32), pltpu.VMEM((WINDOW, D_KV), F32)],
        compiler_params=_cp(("parallel", "arbitrary")), name="attn_seq",
    )(sinks, z3, z3, cos, sin, cache_k, cache_v)


def rope_tables(pos):
    half = HEAD_DIM // 2
    inv = jnp.exp(-math.log(ROPE_THETA) * jnp.arange(half, dtype=F32) / half)
    ang = pos.astype(F32)[:, None] * inv[None, :]
    c, s = jnp.cos(ang), jnp.sin(ang)
    cos = jnp.concatenate([c, c, c, c], axis=1)
    sin = jnp.concatenate([-s, s, -s, s], axis=1)
    return cos, sin


def _rwkv_body(r_ref, k_ref, v_ref, wa_ref, gd_ref, sr_ref, sk_ref, sv_ref, swa_ref, sgd_ref,
               mr_ref, mk_ref, mv_ref, mwa_ref, mgd_ref, w0_ref, w2_ref, a0_ref, a2_ref, g2_ref,
               kkp_ref, ka_ref, rk_ref, lg_ref, lbias_ref, s0_ref, o_ref, so_ref, st_ref, prev_ref, prevw_ref,
               *, chunk, nchunk, npair, t_real, indep):
    tstep = pl.program_id(2)
    n = RWKV_N
    ln = chunk
    tb = chunk * nchunk

    if not indep:
        @pl.when(tstep == 0)
        def _():
            st_ref[...] = s0_ref[...]
            prev_ref[0:1, :] = sr_ref[...]
            prev_ref[1:2, :] = sk_ref[...]
            prev_ref[2:3, :] = sv_ref[...]
            prevw_ref[0:1, :] = swa_ref[...]
            prevw_ref[1:2, :] = sgd_ref[...]

    def mix(x_ref, mu_ref, s_ref, p_ref, idx):
        x = x_ref[...]
        row = _iota(x.shape, 0)
        if indep:
            before = jnp.broadcast_to(s_ref[...], (nchunk, ln, x.shape[1])).reshape(tb, x.shape[1])
            shifted = jnp.where((row & (ln - 1)) == 0, before, pltpu.roll(x, 1, axis=0))
        else:
            shifted = jnp.where(row == 0, p_ref[idx:idx + 1, :], pltpu.roll(x, 1, axis=0))
            p_ref[idx:idx + 1, :] = x[tb - 1:tb, :]
        return x + mu_ref[...] * (shifted - x)

    r = mix(r_ref, mr_ref, sr_ref, prev_ref, 0)
    kr = mix(k_ref, mk_ref, sk_ref, prev_ref, 1)
    vr = mix(v_ref, mv_ref, sv_ref, prev_ref, 2)
    wa = mix(wa_ref, mwa_ref, swa_ref, prevw_ref, 0)
    gd = mix(gd_ref, mgd_ref, sgd_ref, prevw_ref, 1)[:, :RWKV_RANK]
    wd = wa[:, :RWKV_RANK]
    ad = wa[:, RWKV_RANK:]

    w_log = -_softplus(-(w0_ref[...] + _dot3(jnp.tanh(wd), w2_ref[...]))) - 0.5
    logw = -jnp.exp(w_log)
    a = _sigmoid(a0_ref[...] + _dot3(ad, a2_ref[...]))
    gate = _dot3(_sigmoid(gd), g2_ref[...])

    sh = int(math.log2(n))
    seg = jnp.where((_iota((LANES, LANES), 0) >> sh) == (_iota((LANES, LANES), 1) >> sh), 1.0, 0.0).astype(BF16)

    def segsum(x):
        tiles = [_dot_exact_rhs(x[:, i * LANES:(i + 1) * LANES], seg) for i in range(npair)]
        return tiles[0] if npair == 1 else jnp.concatenate(tiles, axis=1)

    kkv = kr * kkp_ref[...]
    kk = kkv / jnp.maximum(jnp.sqrt(segsum(kkv * kkv)), 1e-12)
    kf = kr * (1.0 + (a - 1.0) * ka_ref[...])
    bonus = segsum(r * kf * rk_ref[...]) * vr
    if t_real is not None:
        row = _iota(logw.shape, 0)
        live = ((row & (ln - 1)) if indep else (tstep * tb + row)) < t_real
        zero = jnp.zeros_like(logw)
        logw = jnp.where(live, logw, zero)
        kk = jnp.where(live, kk, zero)
        kf = jnp.where(live, kf, zero)
        vr_s = jnp.where(live, vr, zero)
    else:
        vr_s = vr

    tri = jnp.where(_iota((ln, ln), 0) >= _iota((ln, ln), 1), 1.0, 0.0).astype(BF16)
    rr = _iota((ln, ln), 0)
    cc = _iota((ln, ln), 1)
    strict = rr > cc
    incl = rr >= cc
    eye = jnp.where(rr == cc, 1.0, 0.0)
    eye_n = jnp.where(_iota((n, n), 0) == _iota((n, n), 1), 1.0, 0.0)

    nhead = 2 * npair
    am_l, rm_l, bp_l, kp_l, bl_l, kl_l, v_l, gl_l = [], [], [], [], [], [], [], []
    for ci in range(nchunk):
        rows = slice(ci * ln, (ci + 1) * ln)
        lw = logw[rows]
        c = _dot_exact_lhs(tri, lw)
        gam = jnp.exp(c)
        ginv = jnp.exp(-c)
        am = -kk[rows] * jnp.exp(c - lw)
        rm = r[rows] * gam
        bp = kk[rows] * a[rows] * ginv
        kp = kf[rows] * ginv
        g_last = gam[ln - 1:ln, :]
        bpl = bp * g_last
        kpl = kp * g_last
        vv = vr_s[rows]
        for hd in range(nhead):
            ls = slice(hd * n, (hd + 1) * n)
            am_l.append(am[:, ls]); rm_l.append(rm[:, ls]); bp_l.append(bp[:, ls]); kp_l.append(kp[:, ls])
            bl_l.append(bpl[:, ls]); kl_l.append(kpl[:, ls]); v_l.append(vv[:, ls]); gl_l.append(g_last[:, ls])
    each = lambda fn, *ls: [fn(*xs) for xs in zip(*ls)]
    pw_l = each(lambda am, rm, bp, kp: _dot3(jnp.concatenate([am, rm], axis=0),
                                             jnp.concatenate([bp, kp], axis=0), NT), am_l, rm_l, bp_l, kp_l)
    m_l = each(lambda pw: jnp.where(strict, pw[:ln, :ln], 0.0), pw_l)
    nm_l = each(lambda pw: jnp.where(strict, pw[:ln, ln:], 0.0), pw_l)
    qb_l = each(lambda pw: jnp.where(incl, pw[ln:, :ln], 0.0), pw_l)
    qk_l = each(lambda pw: jnp.where(incl, pw[ln:, ln:], 0.0), pw_l)
    nq_l = each(lambda nm, qk, v: _dot(jnp.concatenate([nm, qk], axis=0), v), nm_l, qk_l, v_l)
    nv_l = [x[:ln] for x in nq_l]
    qkv_l = [x[ln:] for x in nq_l]
    kv_l = each(lambda v, kl: _dot(v, kl, TN), v_l, kl_l)
    tinv_l = each(lambda m: eye + m, m_l)
    p_l = m_l
    for _ in range(int(math.log2(ln)) - 1):
        p_l = each(lambda p: _dot(p, p), p_l)
        tinv_l = each(lambda t, p: t + _dot(t, p), tinv_l, p_l)
    wu_l = each(lambda t, am, nv: _dot(t, jnp.concatenate([am, nv], axis=1)), tinv_l, am_l, nv_l)
    ac_l = each(lambda wu, bl: _dot(wu, bl, TN), wu_l, bl_l)
    a_l = each(lambda ac, gl: eye_n * gl + ac[:n], ac_l, gl_l)
    c_l = each(lambda ac, kv: ac[n:] + kv, ac_l, kv_l)
    ro_l = each(_dot, qb_l, wu_l)
    rt_l = each(lambda rm, ro: rm + ro[:, :n], rm_l, ro_l)
    o0_l = each(lambda ro, qkv: ro[:, n:] + qkv, ro_l, qkv_l)

    o_rows = []
    if indep:
        s_l = [s0_ref[ci, hd] for ci in range(nchunk) for hd in range(nhead)]
        o_all = each(lambda rt, s, o0: _dot(rt, s, NT) + o0, rt_l, s_l, o0_l)
        s_l = each(lambda s, am, cm: _dot(s, am) + cm, s_l, a_l, c_l)
        for ci in range(nchunk):
            o_rows.append(o_all[ci * nhead:(ci + 1) * nhead])
            for hd in range(nhead):
                so_ref[ci, hd] = s_l[ci * nhead + hd]
    else:
        s_l = [st_ref[hd] for hd in range(nhead)]
        for ci in range(nchunk):
            sl = slice(ci * nhead, (ci + 1) * nhead)
            o_rows.append(each(lambda rt, s, o0: _dot(rt, s, NT) + o0, rt_l[sl], s_l, o0_l[sl]))
            s_l = each(lambda s, am, cm: _dot(s, am) + cm, s_l, a_l[sl], c_l[sl])
        for hd in range(nhead):
            st_ref[hd] = s_l[hd]
    cols = [o_rows[0][hd] if nchunk == 1 else jnp.concatenate([o_rows[ci][hd] for ci in range(nchunk)], axis=0)
            for hd in range(nhead)]
    o = jnp.concatenate(cols, axis=1)
    mu_o = segsum(o) * (1.0 / n)
    dlt = o - mu_o
    var_o = segsum(dlt * dlt) * (1.0 / n)
    o = dlt * lax.rsqrt(var_o + RWKV_GN_EPS) * lg_ref[...] + lbias_ref[...]
    o_ref[...] = ((o + bonus) * gate).astype(o_ref.dtype)

    if not indep:
        @pl.when(tstep == pl.num_programs(2) - 1)
        def _():
            so_ref[...] = st_ref[...]


def rwkv_seq(z3, shift, P, j, s0, chunk, nchunk=1, npair=1, t_real=None, indep=False):
    bsz, t, _ = z3.shape
    tb = chunk * nchunk
    sb = None
    if indep:
        sb = nchunk
        z3 = z3.reshape(bsz // nchunk, tb, z3.shape[2])
        bsz, t = bsz // nchunk, tb
    wl = npair * LANES
    groups = D_RWKV // wl
    zoff = (D_Q + 2 * D_KV) // LANES
    nb = D_RWKV // LANES
    wa_blk = 3 * nb
    gd_blk = 3 * nb + 1
    zc = lambda off: pl.BlockSpec((None, tb, wl), lambda b, p, s: (b, s, (zoff + off) // npair + p))
    zw = lambda blk: pl.BlockSpec((None, tb, LANES), lambda b, p, s: (b, s, zoff + blk))
    sc = lambda off: pl.BlockSpec((sb, 1, wl), lambda b, p, s: (b, 0, off // npair + p))
    sw = lambda blk: pl.BlockSpec((sb, 1, LANES), lambda b, p, s: (b, 0, blk))
    mc = lambda off: pl.BlockSpec((1, wl), lambda b, p, s: (0, off // npair + p))
    mw = lambda blk: pl.BlockSpec((1, LANES), lambda b, p, s: (0, blk))
    vec = pl.BlockSpec((1, wl), lambda b, p, s: (0, p))
    lora = pl.BlockSpec((RWKV_RANK, wl), lambda b, p, s: (0, p))
    st = pl.BlockSpec((sb, 2 * npair, RWKV_N, RWKV_N), lambda b, p, s: (b, p, 0, 0))
    mu = jnp.pad(P['rwkv_mu'][j], (0, D_SHIFT_PAD - D_SHIFT)).reshape(1, D_SHIFT_PAD)
    row = lambda x: x.reshape(1, D_RWKV)
    body = functools.partial(_rwkv_body, chunk=chunk, nchunk=nchunk, npair=npair, t_real=t_real, indep=indep)
    nseq = bsz * (sb or 1)
    o, so = pl.pallas_call(
        body, grid=(bsz, groups, t // tb),
        in_specs=[zc(0), zc(nb), zc(2 * nb), zw(wa_blk), zw(gd_blk),
                  sc(0), sc(nb), sc(2 * nb), sw(wa_blk), sw(gd_blk),
                  mc(0), mc(nb), mc(2 * nb), mw(wa_blk), mw(gd_blk),
                  vec, lora, vec, lora, lora, vec, vec, vec, vec, vec, st],
        out_specs=[pl.BlockSpec((None, tb, wl), lambda b, p, s: (b, s, p)), st],
        out_shape=[jax.ShapeDtypeStruct((bsz, t, D_RWKV), BF16),
                   jax.ShapeDtypeStruct((nseq, RWKV_HEADS, RWKV_N, RWKV_N), F32)],
        scratch_shapes=[pltpu.VMEM((2 * npair, RWKV_N, RWKV_N), F32), pltpu.VMEM((8, wl), F32),
                        pltpu.VMEM((8, LANES), F32)],
        compiler_params=_cp(("parallel", "parallel", "arbitrary")), name="rwkv_seq",
    )(z3, z3, z3, z3, z3, shift, shift, shift, shift, shift, mu, mu, mu, mu, mu,
      row(P['rwkv_w0'][j]), P['rwkv_w2'][j], row(P['rwkv_a0'][j]), P['rwkv_a2'][j], P['rwkv_g2'][j],
      row(P['rwkv_kk'][j]), row(P['rwkv_ka'][j]), row(P['rwkv_rk'][j]), row(P['rwkv_lnx_g'][j]),
      row(P['rwkv_lnx_b'][j]), s0)
    return o.reshape(nseq, chunk if indep else t, D_RWKV), so


def _pad_time(z2, tp):
    return jnp.pad(z2[:, None, :], ((0, 0), (0, tp - 1), (0, 0)))


def _mix_layer(l, h, x, grp, pos0, W, P, st, single):
    n = x.shape[0]
    t = grp.t
    bsz = n // t
    tm = grp.tm
    j = l // 2
    step_pad = 8
    new = {}
    if l % 2 == 0:
        z = matmul(h, W['w_in_even'][j], min(n, 2 * tm), D_IN_EVEN // 4)
        if single:
            a_out, new['conv'] = conv_step(z, P['conv_w'][j], st['conv'][j])
            b3, new['hgrn'] = hgrn_seq(_pad_time(z, step_pad), P['hgrn_lb'], P['hgrn_norm'][j], st['hgrn'][j], j,
                                       chunk=step_pad, nchunk=4, nhead=4, t_real=1, indep=True)
            b_out = b3[:, 0]
        else:
            z3 = z.reshape(bsz, t, D_IN_EVEN)
            a3, new['conv'] = conv_seq(z3, P['conv_w'][j], st['conv'][j])
            b3, new['hgrn'] = hgrn_seq(z3, P['hgrn_lb'], P['hgrn_norm'][j], st['hgrn'][j], j, chunk=64, nchunk=8)
            a_out, b_out = a3.reshape(n, D_CONV), b3.reshape(n, D_HV)
        wa, wb = W['w_out_even'][j][:D_CONV], W['w_out_even'][j][D_CONV:]
    else:
        z = matmul(h, W['w_in_odd'][j], tm, D_IN_ODD_PAD // 2)
        shift_in = jnp.pad(st['shift'][j], ((0, 0), (0, D_SHIFT_PAD - D_SHIFT)))[:, None, :]
        kc = st['k'][j].reshape(bsz, WINDOW, D_KV)
        vc = st['v'][j].reshape(bsz, WINDOW, D_KV)
        if single:
            cos, sin = rope_tables(pos0 + jnp.arange(step_pad, dtype=I32))
            a3, nk, nv = attn_seq(_pad_time(z[:, :D_Q + 2 * D_KV], step_pad), P['attn_sinks'][j], cos, sin,
                                  kc, vc, prev_valid=True, t_real=1, tq=step_pad)
            b3, new['rwkv'] = rwkv_seq(_pad_time(z, step_pad), shift_in, P, j, st['rwkv'][j], chunk=step_pad,
                                       nchunk=4, npair=4, t_real=1, indep=True)
            a_out, b_out = a3[:, 0], b3[:, 0]
            new['shift'] = z[:, D_Q + 2 * D_KV:D_IN_ODD]
        else:
            z3 = z.reshape(bsz, t, D_IN_ODD_PAD)
            cos, sin = rope_tables(pos0 + jnp.arange(t, dtype=I32))
            a3, nk, nv = attn_seq(z3, P['attn_sinks'][j], cos, sin, kc, vc, prev_valid=False, t_real=WINDOW)
            b3, new['rwkv'] = rwkv_seq(z3, shift_in, P, j, st['rwkv'][j], chunk=64, nchunk=4, npair=2)
            a_out, b_out = a3.reshape(n, D_Q), b3.reshape(n, D_RWKV)
            new['shift'] = z3[:, t - 1, D_Q + 2 * D_KV:D_IN_ODD]
        new['k'] = nk.reshape(bsz, WINDOW, ATTN_KV_HEADS, HEAD_DIM)
        new['v'] = nv.reshape(bsz, WINDOW, ATTN_KV_HEADS, HEAD_DIM)
        wa, wb = W['w_out_odd'][j][:D_Q], W['w_out_odd'][j][D_Q:]
    return a_out, b_out, wa, wb, new


def kernel(x_prompt, x_sample, c_prompt, c_sample, state_conv, state_hgrn, cache_swa_k, cache_swa_v,
           state_rwkv, state_shift, ada_w, ada_b, ln_g, ln_b, w_in_even, w_out_even, conv_w, hgrn_lb,
           hgrn_norm, w_in_odd, w_out_odd, attn_sinks, rwkv_mu, rwkv_w0, rwkv_w2, rwkv_a0, rwkv_a2,
           rwkv_g2, rwkv_kk, rwkv_ka, rwkv_rk, rwkv_lnx_g, rwkv_lnx_b, moe_w_grp, moe_b_grp, moe_w_exp,
           moe_b_exp, moe_w1, moe_w3, moe_w2):
    P = dict(ln_g=ln_g, ln_b=ln_b, conv_w=conv_w, hgrn_lb=hgrn_lb, hgrn_norm=hgrn_norm, attn_sinks=attn_sinks,
             rwkv_mu=rwkv_mu, rwkv_w0=rwkv_w0, rwkv_w2=rwkv_w2, rwkv_a0=rwkv_a0, rwkv_a2=rwkv_a2,
             rwkv_g2=rwkv_g2, rwkv_kk=rwkv_kk, rwkv_ka=rwkv_ka, rwkv_rk=rwkv_rk.reshape(N_ODD, D_RWKV),
             rwkv_lnx_g=rwkv_lnx_g, rwkv_lnx_b=rwkv_lnx_b)
    bp, tp, d = x_prompt.shape
    bs, ts, _ = x_sample.shape
    n_p, n_s = bp * tp, bs * ts
    router_w = jnp.zeros((DEPTH, d, 2 * LANES), F32)
    router_w = router_w.at[:, :, :N_GROUPS].set(moe_w_grp).at[:, :, LANES:LANES + N_EXPERTS].set(moe_w_exp)
    router_b = jnp.zeros((DEPTH, 1, 2 * LANES), F32)
    router_b = router_b.at[:, 0, :N_GROUPS].set(moe_b_grp).at[:, 0, LANES:LANES + N_EXPERTS].set(moe_b_exp)
    W = dict(w_in_even=w_in_even.astype(BF16), w_out_even=w_out_even.astype(BF16),
             w_in_odd=jnp.pad(w_in_odd.astype(BF16), ((0, 0), (0, 0), (0, D_IN_ODD_PAD - D_IN_ODD))),
             w_out_odd=w_out_odd.astype(BF16))

    mod = ada_mod(jnp.concatenate([c_prompt, c_sample], axis=0), ada_w, ada_b)
    mod = mod.reshape(DEPTH, bp + bs, 6, d).transpose(0, 2, 1, 3)
    grp_p = _Group(n_p, tp, 512, mod[:, :, :bp, None, :], per_row=False)
    grp_s = _Group(n_s, ts, n_s, mod[:, :, bp:], per_row=True)

    zeros = lambda *s: jnp.zeros(s, F32)
    st_p = dict(conv=zeros(N_EVEN, bp, CONV_WIDTH - 1, D_CONV), hgrn=zeros(N_EVEN, bp, HGRN_HEADS, HGRN_DK, HGRN_DV),
                k=zeros(N_ODD, bp, WINDOW, ATTN_KV_HEADS, HEAD_DIM), v=zeros(N_ODD, bp, WINDOW, ATTN_KV_HEADS, HEAD_DIM),
                rwkv=zeros(N_ODD, bp, RWKV_HEADS, RWKV_N, RWKV_N), shift=zeros(N_ODD, bp, D_SHIFT))
    st_s = dict(conv=state_conv, hgrn=state_hgrn, k=cache_swa_k, v=cache_swa_v, rwkv=state_rwkv, shift=state_shift)
    new_p = {k: [] for k in st_p}
    new_s = {k: [] for k in st_s}

    n_all = n_p + n_s
    t_all = jnp.arange(n_all, dtype=I32)
    row_of = lambda k: jnp.where(t_all < n_p, k * n_p + t_all, 2 * n_p + k * n_s + (t_all - n_p))
    dst_of_entry = jnp.stack([row_of(0), row_of(1)], axis=1).reshape(-1)

    x_p, x_s = x_prompt.reshape(n_p, d), x_sample.reshape(n_s, d)
    h_all = zeros(n_all, d)
    h_p, h_s = modulate(x_p, grp_p, 0, 1, 0), modulate(x_s, grp_s, 0, 1, 0)
    for l in range(DEPTH):
        a_p, b_p, wa, wb, np_l = _mix_layer(l, h_p, x_p, grp_p, 0, W, P, st_p, single=False)
        a_s, b_s, _, _, ns_l = _mix_layer(l, h_s, x_s, grp_s, PAST_LEN, W, P, st_s, single=True)
        for k, v in np_l.items():
            new_p[k].append(v)
        for k, v in ns_l.items():
            new_s[k].append(v)
        x_p, h_all = out_proj_ln(a_p, b_p, wa, wb, x_p, grp_p, l, ln_g[l, 0], ln_b[l, 0], n_all, 0, h_all)
        x_s, h_all = out_proj_ln(a_s, b_s, wa, wb, x_s, grp_s, l, ln_g[l, 0], ln_b[l, 0], n_all, n_p, h_all)
        eid_p, gate_p, rank_p, cnt = router(h_all, 0, n_p, router_w[l], router_b[l], 256, zeros(1, LANES))
        eid_s, gate_s, rank_s, cnt = router(h_all, n_p, n_s, router_w[l], router_b[l], n_s, cnt)
        tok, dst, be, n_used, n_blocks = moe_plan(jnp.concatenate([eid_p, eid_s]), jnp.concatenate([rank_p, rank_s]),
                                                  cnt, dst_of_entry, 2 * n_all)
        ys = moe_ffn(h_all, tok, dst, be, n_used, n_blocks, moe_w1, moe_w3, moe_w2, l)
        last = l + 1 == DEPTH
        x_p, h_p = moe_combine_ln(ys, 0, n_p, gate_p, x_p, grp_p, l, ln_g[l, 1], ln_b[l, 1], has_next=not last)
        x_s, h_s = moe_combine_ln(ys, 2 * n_p, 2 * n_p + n_s, gate_s, x_s, grp_s, l, ln_g[l, 1], ln_b[l, 1],
                                  has_next=not last)
    order = ('conv', 'hgrn', 'k', 'v', 'rwkv', 'shift')
    return ((x_p.reshape(bp, tp, d), x_s.reshape(bs, ts, d))
            + tuple(jnp.stack(new_p[k]) for k in order) + tuple(jnp.stack(new_s[k]) for k in order))
```
